```python
import jax
import jax.numpy as jnp
from jax import lax
import numpy as np

D_MODEL = 1024
BATCH = 8
SEQ = 4096
DEPTH = 2
DEC_BATCH = 128
DEC_SEQ = 4
PAST_LEN = 16384
PAGE_SIZE = 128

N_A_LAYERS = DEPTH // 2
N_B_LAYERS = DEPTH - N_A_LAYERS
RET_HEADS = 8
RET_DK = D_MODEL // RET_HEADS
RET_DV = 2 * RET_DK
RET_CHUNK = 128
MLA_HEADS = 8
QK_NOPE = 128
QK_ROPE = 64
V_HEAD = 128
KV_LORA = D_MODEL // 4
Q_LORA = 3 * KV_LORA
Q_BLOCK = 128
SM_SCALE = (QK_NOPE + QK_ROPE) ** -0.5
N_EXPERTS = 32
TOP_K = 4
D_EXPERT = D_MODEL
SWIGLU_LIMIT = 7.0
SWIGLU_ALPHA = 1.702
MOE_BLOCK = 128
ROPE_BASE = 10000.0
EPS = 1e-6
NEG_INF = -1e30

kernel_name = 'yoco_retention_mla_moe_adaln_step'


def rms_norm(x, g):
    xf = x.astype(jnp.float32)
    y = xf * lax.rsqrt(jnp.mean(xf * xf, axis=-1, keepdims=True) + EPS)
    return (y * g.astype(jnp.float32)).astype(x.dtype)


def adaln(c, w, b):
    return jax.nn.silu(c) @ w + b


def modulate(xn, shift, scale):
    return xn * (1.0 + scale[:, None, :]) + shift[:, None, :]


def rope(x, pos):
    d = x.shape[-1]
    half = d // 2
    inv = ROPE_BASE ** (-jnp.arange(half, dtype=jnp.float32) / half)
    ang = pos.astype(jnp.float32)[:, None] * inv[None, :]
    bshape = (1, pos.shape[0]) + (1,) * (x.ndim - 3) + (half,)
    cos = jnp.cos(ang).reshape(bshape)
    sin = jnp.sin(ang).reshape(bshape)
    xf = x.astype(jnp.float32)
    x1, x2 = xf[..., :half], xf[..., half:]
    return jnp.concatenate([x1 * cos - x2 * sin, x1 * sin + x2 * cos], axis=-1).astype(x.dtype)


def retention_log_decay():
    return jnp.log1p(-jnp.exp2(-5.0 - jnp.arange(RET_HEADS, dtype=jnp.float32)))


def retention_project(hn, w_in, pos):
    B, S, _ = hn.shape
    hk = RET_HEADS * RET_DK
    hv = RET_HEADS * RET_DV
    z = hn @ w_in
    q = rope(z[..., :hk].reshape(B, S, RET_HEADS, RET_DK), pos).astype(jnp.float32)
    k = (rope(z[..., hk:2 * hk].reshape(B, S, RET_HEADS, RET_DK), pos) * (RET_DK ** -0.5)).astype(jnp.float32)
    v = z[..., 2 * hk:2 * hk + hv].reshape(B, S, RET_HEADS, RET_DV).astype(jnp.float32)
    g = z[..., 2 * hk + hv:]
    return q, k, v, g


def retention_chunk(state, q, k, v, log_decay):
    C = q.shape[2]
    idx = jnp.arange(C, dtype=jnp.float32)
    diff = idx[:, None] - idx[None, :]
    causal = diff >= 0
    lg = log_decay[:, None, None]
    decay = jnp.where(causal[None], jnp.exp(jnp.where(causal, diff, 0.0)[None] * lg), 0.0)
    q_decay = jnp.exp((idx + 1.0)[None, :, None] * lg)
    k_decay = jnp.exp((C - 1.0 - idx)[None, :, None] * lg)
    scores = jnp.einsum('bhqd,bhkd->bhqk', q, k) * decay
    o = jnp.einsum('bhqk,bhkv->bhqv', scores, v) + jnp.einsum('bhqd,bhdv->bhqv', q * q_decay, state)
    new_state = jnp.exp(C * lg) * state + jnp.einsum('bhkd,bhkv->bhdv', k * k_decay, v)
    return new_state, o


def retention_out(o, g, gn, w_o):
    B, S = o.shape[:2]
    mu = jnp.mean(o, axis=-1, keepdims=True)
    var = jnp.mean(jnp.square(o - mu), axis=-1, keepdims=True)
    on = (o - mu) * lax.rsqrt(var + EPS) * gn.astype(jnp.float32)
    return (jax.nn.silu(g) * on.reshape(B, S, -1).astype(g.dtype)) @ w_o


def retention_prompt(hn, w_in, gn, w_o, pos, log_decay):
    B, S, _ = hn.shape
    q, k, v, g = retention_project(hn, w_in, pos)
    nc = S // RET_CHUNK

    def to_chunks(t):
        return t.reshape(B, nc, RET_CHUNK, RET_HEADS, t.shape[-1]).transpose(1, 0, 3, 2, 4)

    s0 = jnp.zeros((B, RET_HEADS, RET_DK, RET_DV), jnp.float32)
    s_fin, o = lax.scan(lambda st, qkv: retention_chunk(st, qkv[0], qkv[1], qkv[2], log_decay),
                        s0, (to_chunks(q), to_chunks(k), to_chunks(v)))
    o = o.transpose(1, 0, 3, 2, 4).reshape(B, S, RET_HEADS, RET_DV)
    return retention_out(o, g, gn, w_o), s_fin


def retention_sample(hn, state, w_in, gn, w_o, pos, log_decay):
    q, k, v, g = retention_project(hn, w_in, pos)
    tr = lambda t: t.transpose(0, 2, 1, 3)
    s_new, o = retention_chunk(state.astype(jnp.float32), tr(q), tr(k), tr(v), log_decay)
    return retention_out(tr(o), g, gn, w_o), s_new


def mla_kv_side(h, c, w_ada_kv, b_ada_kv, g_kv_in, w_kv_a, g_ckv, pos):
    shift, scale = jnp.split(adaln(c, w_ada_kv, b_ada_kv), 2, axis=-1)
    hn = modulate(rms_norm(h, g_kv_in), shift, scale)
    z = hn @ w_kv_a
    ckv = rms_norm(z[..., :KV_LORA], g_ckv)
    kr = rope(z[..., KV_LORA:], pos)
    return ckv, kr


def mla_query(hn, w_dq, g_cq, w_uq, pos):
    B, S, _ = hn.shape
    q = (rms_norm(hn @ w_dq, g_cq) @ w_uq).reshape(B, S, MLA_HEADS, QK_NOPE + QK_ROPE)
    return q[..., :QK_NOPE], rope(q[..., QK_NOPE:], pos)


def mla_attend_prompt(q_nope, q_rope, k_nope, k_rope, v, w_o):
    B, S = q_nope.shape[:2]
    nb = S // Q_BLOCK
    kpos = jnp.arange(S)

    def blocks(t):
        return t.reshape((B, nb, Q_BLOCK) + t.shape[2:]).swapaxes(0, 1)

    def one_block(args):
        qn, qr, i = args
        s = (jnp.einsum('bqhd,bkhd->bhqk', qn, k_nope)
             + jnp.einsum('bqhr,bkr->bhqk', qr, k_rope)).astype(jnp.float32) * SM_SCALE
        qpos = i * Q_BLOCK + jnp.arange(Q_BLOCK)
        s = jnp.where((kpos[None, :] <= qpos[:, None])[None, None], s, NEG_INF)
        p = jax.nn.softmax(s, axis=-1).astype(v.dtype)
        return jnp.einsum('bhqk,bkhd->bqhd', p, v)

    o = lax.map(one_block, (blocks(q_nope), blocks(q_rope), jnp.arange(nb)))
    return o.swapaxes(0, 1).reshape(B, S, -1) @ w_o


def mla_attend_sample(q_nope, q_rope, ckv_new, kr_new, cache_ckv, cache_krope, page_table, w_uk, w_uv, w_o):
    B, T = q_nope.shape[:2]
    q_lat = jnp.einsum('bthd,chd->bthc', q_nope, w_uk)
    tril = jnp.tril(jnp.ones((T, T), dtype=bool))

    def one_seq(args):
        pages, ql, qr, ckn, krn = args
        ck = cache_ckv[pages].reshape(-1, KV_LORA)
        kr = cache_krope[pages].reshape(-1, QK_ROPE)
        n_past = ck.shape[0]
        s_past = jnp.einsum('thc,kc->htk', ql, ck) + jnp.einsum('thr,kr->htk', qr, kr)
        s_new = jnp.einsum('thc,kc->htk', ql, ckn) + jnp.einsum('thr,kr->htk', qr, krn)
        s = jnp.concatenate([s_past, s_new], axis=-1).astype(jnp.float32) * SM_SCALE
        valid = jnp.concatenate([jnp.ones((T, n_past), dtype=bool), tril], axis=-1)
        s = jnp.where(valid[None], s, NEG_INF)
        p = jax.nn.softmax(s, axis=-1).astype(ck.dtype)
        return (jnp.einsum('htk,kc->thc', p[..., :n_past], ck)
                + jnp.einsum('htk,kc->thc', p[..., n_past:], ckn))

    ctx = lax.map(one_seq, (page_table, q_lat, q_rope, ckv_new, kr_new))
    o = jnp.einsum('bthc,chd->bthd', ctx, w_uv).reshape(B, T, -1)
    return o @ w_o


def moe(xn, w_router, b_router, w_gu, b_gu, w_dn, b_dn):
    shape = xn.shape
    d = shape[-1]
    x2 = xn.reshape(-1, d)
    n_tok = x2.shape[0]
    logits = (x2 @ w_router + b_router).astype(jnp.float32)
    top_v, top_i = lax.top_k(logits, TOP_K)
    gates = jax.nn.softmax(top_v, axis=-1)
    n_asg = n_tok * TOP_K
    flat_e = top_i.reshape(-1)
    order = jnp.argsort(flat_e, stable=True)
    sorted_e = flat_e[order]
    counts = jnp.bincount(flat_e, length=N_EXPERTS)
    padded = (counts + MOE_BLOCK - 1) // MOE_BLOCK * MOE_BLOCK
    pad_end = jnp.cumsum(padded)
    pad_start = pad_end - padded
    grp_start = jnp.cumsum(counts) - counts
    dest = pad_start[sorted_e] + jnp.arange(n_asg) - grp_start[sorted_e]
    n_blocks = -(-n_asg // MOE_BLOCK) + N_EXPERTS
    tok = order // TOP_K
    row_tok = jnp.full((n_blocks * MOE_BLOCK,), n_tok, dtype=tok.dtype).at[dest].set(tok)
    blk_exp = jnp.minimum(jnp.searchsorted(pad_end, jnp.arange(n_blocks) * MOE_BLOCK, side='right'), N_EXPERTS - 1)
    x_ext = jnp.concatenate([x2, jnp.zeros((1, d), x2.dtype)], axis=0)
    x_blk = x_ext[row_tok].reshape(n_blocks, MOE_BLOCK, d)

    def expert_block(args):
        xb, e = args
        h = xb @ w_gu[e] + b_gu[e]
        gate = jnp.minimum(h[:, :D_EXPERT], SWIGLU_LIMIT)
        up = jnp.clip(h[:, D_EXPERT:], -SWIGLU_LIMIT, SWIGLU_LIMIT)
        act = (up + 1.0) * gate * jax.nn.sigmoid(SWIGLU_ALPHA * gate)
        return act @ w_dn[e] + b_dn[e]

    y_blk = lax.map(expert_block, (x_blk, blk_exp)).reshape(n_blocks * MOE_BLOCK, d)
    w = gates.reshape(-1)[order].astype(x2.dtype)
    y = jnp.zeros_like(x2).at[tok].add(y_blk[dest] * w[:, None])
    return y.reshape(shape)


def setup_inputs(seed: int = 0) -> dict:
    key = jax.random.key(seed)
    ks = iter(jax.random.split(key, 48))
    f32 = jnp.float32
    D = D_MODEL

    def nrm(shape, scale):
        return jax.random.normal(next(ks), shape, dtype=f32) * scale

    def gain(shape):
        return 1.0 + nrm(shape, 0.05)

    n_pages = PAST_LEN // PAGE_SIZE
    n_pool = (DEC_BATCH * n_pages * 5) // 4
    hk = RET_HEADS * RET_DK
    hv = RET_HEADS * RET_DV
    page_table = jax.random.permutation(next(ks), n_pool)[:DEC_BATCH * n_pages].reshape(DEC_BATCH, n_pages).astype(jnp.int32)
    return dict(
        x_prompt=nrm((BATCH, SEQ, D), 1.0),
        x_sample=nrm((DEC_BATCH, DEC_SEQ, D), 1.0),
        c_prompt=nrm((BATCH, D), 1.0),
        c_sample=nrm((DEC_BATCH, D), 1.0),
        state_ret=nrm((N_A_LAYERS, DEC_BATCH, RET_HEADS, RET_DK, RET_DV), 0.5),
        cache_ckv=nrm((n_pool, PAGE_SIZE, KV_LORA), 1.0),
        cache_krope=nrm((n_pool, PAGE_SIZE, QK_ROPE), 1.0),
        page_table=page_table,
        w_ada=nrm((DEPTH, D, 6 * D), 0.5 * D ** -0.5),
        b_ada=nrm((DEPTH, 6 * D), 0.02),
        g_mix=gain((DEPTH, D)),
        g_ffn=gain((DEPTH, D)),
        ret_w_in=nrm((N_A_LAYERS, D, 2 * hk + 2 * hv), D ** -0.5),
        ret_gn=gain((N_A_LAYERS, RET_HEADS, RET_DV)),
        ret_w_o=nrm((N_A_LAYERS, hv, D), hv ** -0.5),
        w_ada_kv=nrm((D, 2 * D), 0.5 * D ** -0.5),
        b_ada_kv=nrm((2 * D,), 0.02),
        g_kv_in=gain((D,)),
        mla_w_kv_a=nrm((D, KV_LORA + QK_ROPE), D ** -0.5),
        g_ckv=gain((KV_LORA,)),
        mla_w_uk=nrm((KV_LORA, MLA_HEADS, QK_NOPE), KV_LORA ** -0.5),
        mla_w_uv=nrm((KV_LORA, MLA_HEADS, V_HEAD), KV_LORA ** -0.5),
        mla_w_dq=nrm((N_B_LAYERS, D, Q_LORA), D ** -0.5),
        g_cq=gain((N_B_LAYERS, Q_LORA)),
        mla_w_uq=nrm((N_B_LAYERS, Q_LORA, MLA_HEADS * (QK_NOPE + QK_ROPE)), Q_LORA ** -0.5),
        mla_w_o=nrm((N_B_LAYERS, MLA_HEADS * V_HEAD, D), (MLA_HEADS * V_HEAD) ** -0.5),
        w_router=nrm((DEPTH, D, N_EXPERTS), D ** -0.5),
        b_router=nrm((DEPTH, N_EXPERTS), 0.01),
        w_gu=nrm((DEPTH, N_EXPERTS, D, 2 * D_EXPERT), D ** -0.5),
        b_gu=nrm((DEPTH, N_EXPERTS, 2 * D_EXPERT), 0.02),
        w_dn=nrm((DEPTH, N_EXPERTS, D_EXPERT, D), D_EXPERT ** -0.5),
        b_dn=nrm((DEPTH, N_EXPERTS, D), 0.02),
        w_ada_f=nrm((D, 2 * D), 0.5 * D ** -0.5),
        b_ada_f=nrm((2 * D,), 0.02),
        g_final=gain((D,)),
    )


def reference(x_prompt, x_sample, c_prompt, c_sample, state_ret, cache_ckv, cache_krope, page_table,
              w_ada, b_ada, g_mix, g_ffn, ret_w_in, ret_gn, ret_w_o,
              w_ada_kv, b_ada_kv, g_kv_in, mla_w_kv_a, g_ckv, mla_w_uk, mla_w_uv,
              mla_w_dq, g_cq, mla_w_uq, mla_w_o,
              w_router, b_router, w_gu, b_gu, w_dn, b_dn,
              w_ada_f, b_ada_f, g_final):
    past_len = page_table.shape[1] * cache_ckv.shape[1]
    pos_p = jnp.arange(x_prompt.shape[1], dtype=jnp.int32)
    pos_s = past_len + jnp.arange(x_sample.shape[1], dtype=jnp.int32)
    log_decay = retention_log_decay()
    hp, hs = x_prompt, x_sample
    ret_p, ret_s = [], []
    for l in range(DEPTH):
        sh1p, sc1p, ga1p, sh2p, sc2p, ga2p = jnp.split(adaln(c_prompt, w_ada[l], b_ada[l]), 6, axis=-1)
        sh1s, sc1s, ga1s, sh2s, sc2s, ga2s = jnp.split(adaln(c_sample, w_ada[l], b_ada[l]), 6, axis=-1)
        np_ = modulate(rms_norm(hp, g_mix[l]), sh1p, sc1p)
        ns = modulate(rms_norm(hs, g_mix[l]), sh1s, sc1s)
        if l < N_A_LAYERS:
            op, st_p = retention_prompt(np_, ret_w_in[l], ret_gn[l], ret_w_o[l], pos_p, log_decay)
            os_, st_s = retention_sample(ns, state_ret[l], ret_w_in[l], ret_gn[l], ret_w_o[l], pos_s, log_decay)
            ret_p.append(st_p.astype(state_ret.dtype))
            ret_s.append(st_s.astype(state_ret.dtype))
        else:
            b = l - N_A_LAYERS
            if b == 0:
                ckv_p, kr_p = mla_kv_side(hp, c_prompt, w_ada_kv, b_ada_kv, g_kv_in, mla_w_kv_a, g_ckv, pos_p)
                ckv_s, kr_s = mla_kv_side(hs, c_sample, w_ada_kv, b_ada_kv, g_kv_in, mla_w_kv_a, g_ckv, pos_s)
                k_nope_p = jnp.einsum('bsc,chd->bshd', ckv_p, mla_w_uk)
                v_p = jnp.einsum('bsc,chd->bshd', ckv_p, mla_w_uv)
            qn_p, qr_p = mla_query(np_, mla_w_dq[b], g_cq[b], mla_w_uq[b], pos_p)
            op = mla_attend_prompt(qn_p, qr_p, k_nope_p, kr_p, v_p, mla_w_o[b])
            qn_s, qr_s = mla_query(ns, mla_w_dq[b], g_cq[b], mla_w_uq[b], pos_s)
            os_ = mla_attend_sample(qn_s, qr_s, ckv_s, kr_s, cache_ckv, cache_krope, page_table,
                                    mla_w_uk, mla_w_uv, mla_w_o[b])
        hp = hp + ga1p[:, None, :] * op
        hs = hs + ga1s[:, None, :] * os_
        hp = hp + ga2p[:, None, :] * moe(modulate(rms_norm(hp, g_ffn[l]), sh2p, sc2p),
                                          w_router[l], b_router[l], w_gu[l], b_gu[l], w_dn[l], b_dn[l])
        hs = hs + ga2s[:, None, :] * moe(modulate(rms_norm(hs, g_ffn[l]), sh2s, sc2s),
                                          w_router[l], b_router[l], w_gu[l], b_gu[l], w_dn[l], b_dn[l])
    shf_p, scf_p = jnp.split(adaln(c_prompt, w_ada_f, b_ada_f), 2, axis=-1)
    shf_s, scf_s = jnp.split(adaln(c_sample, w_ada_f, b_ada_f), 2, axis=-1)
    y_prompt = modulate(rms_norm(hp, g_final), shf_p, scf_p)
    y_sample = modulate(rms_norm(hs, g_final), shf_s, scf_s)
    state_ret_prompt = jnp.stack(ret_p, axis=0)
    state_ret_sample = jnp.stack(ret_s, axis=0)
    return (y_prompt, y_sample, state_ret_prompt, state_ret_sample, ckv_p, kr_p, ckv_s, kr_s)
```

```python
import functools
import math

import jax
import jax.numpy as jnp
from jax import lax
from jax.experimental import pallas as pl
from jax.experimental.pallas import tpu as pltpu

F32 = jnp.float32
BF16 = jnp.bfloat16

D_MODEL = 1024
RET_HEADS = 8
RET_DK = 128
RET_DV = 256
RET_CHUNK = 128
MLA_HEADS = 8
QK_NOPE = 128
QK_ROPE = 64
QK_HEAD = QK_NOPE + QK_ROPE
V_HEAD = 128
KV_LORA = 256
Q_LORA = 768
SM_SCALE = QK_HEAD ** -0.5
N_EXPERTS = 32
TOP_K = 4
D_EXPERT = 1024
SWIGLU_LIMIT = 7.0
SWIGLU_ALPHA = 1.702
ROPE_BASE = 10000.0
EPS = 1e-6
NEG_INF = -1e30

LANES = 128
TM = 512
MOE_ROWS = 256
ATT_T = 512
VMEM_LIMIT = 56 * 1024 * 1024

LOG_DECAY = tuple(math.log1p(-2.0 ** (-5.0 - h)) for h in range(RET_HEADS))


def _cparams(sem):
    return pltpu.CompilerParams(dimension_semantics=sem, vmem_limit_bytes=VMEM_LIMIT)


def _rms(x, g):
    return x * lax.rsqrt(jnp.mean(x * x, axis=-1, keepdims=True) + EPS) * g


def _dot(a, b):
    return jnp.dot(a, b, preferred_element_type=F32)


def _dot_nt(a, b):
    return lax.dot_general(a, b, (((1,), (1,)), ((), ())), preferred_element_type=F32)


def _dot_tn(a, b):
    return lax.dot_general(a, b, (((0,), (0,)), ((), ())), preferred_element_type=F32)


def _rope_group(x, cos, sin_signed, group):
    half = group // 2
    lane = lax.broadcasted_iota(jnp.int32, x.shape, 1)
    x_up = pltpu.roll(x, LANES - half, 1)
    x_dn = pltpu.roll(x, half, 1)
    rot = jnp.where((lane & (group - 1)) < half, x_up, x_dn)
    return x * cos + rot * sin_signed


def _pick(is_prompt, p_ref, s_ref):
    return jnp.where(is_prompt, p_ref[0], s_ref[...])


def _mod_specs(mod_p, mod_s, chunk, npt, seq_tiles):
    dm = D_MODEL
    sp = pl.BlockSpec((1, 1, dm), lambda i, *_: (jnp.minimum(i, npt - 1) // seq_tiles, 0, chunk))
    ss = pl.BlockSpec((TM, dm), lambda i, *_: (0, chunk))
    return [sp, ss], [mod_p, mod_s]


def _norm_mod(i, npt, x_ref, g_ref, shp, shs, scp, scs):
    is_p = i < npt
    xn = _rms(x_ref[...], g_ref[...])
    return xn * (1.0 + _pick(is_p, scp, scs)) + _pick(is_p, shp, shs)


def _adaln_kernel(c_ref, w_ref, b_ref, o_ref):
    c = c_ref[...]
    a = (c * jax.nn.sigmoid(c)).astype(BF16)
    o_ref[...] = _dot(a, w_ref[...].astype(BF16)) + b_ref[...]


def _adaln(c_all, w, b, layer=None):
    n = w.shape[-1]
    tn = 512
    rows = c_all.shape[0]
    if layer is None:
        w_spec = pl.BlockSpec((D_MODEL, tn), lambda j: (0, j))
        b_spec = pl.BlockSpec((1, tn), lambda j: (0, j))
        b = b.reshape(1, n)
    else:
        w_spec = pl.BlockSpec((None, D_MODEL, tn), lambda j: (layer, 0, j))
        b_spec = pl.BlockSpec((None, 1, tn), lambda j: (layer, 0, j))
        b = b.reshape(b.shape[0], 1, n)
    return pl.pallas_call(
        _adaln_kernel,
        grid=(n // tn,),
        in_specs=[pl.BlockSpec((rows, D_MODEL), lambda j: (0, 0)), w_spec, b_spec],
        out_specs=pl.BlockSpec((rows, tn), lambda j: (0, j)),
        out_shape=jax.ShapeDtypeStruct((rows, n), F32),
        compiler_params=_cparams(("arbitrary",)),
        name="adaln",
    )(c_all, w, b)


def _inproj_kernel(x_ref, g_ref, shp, shs, scp, scs, w_ref, cos_ref, sin_ref, o_ref, xn_scr,
                   *, npt, tn, n_rope):
    i = pl.program_id(0)
    j = pl.program_id(1)

    @pl.when(j == 0)
    def _():
        xn_scr[...] = _norm_mod(i, npt, x_ref, g_ref, shp, shs, scp, scs).astype(BF16)

    acc = _dot(xn_scr[...], w_ref[...])

    @pl.when(j < n_rope)
    def _():
        cos = cos_ref[0]
        sin = sin_ref[0]
        for c in range(tn // LANES):
            sl = slice(c * LANES, (c + 1) * LANES)
            o_ref[:, sl] = _rope_group(acc[:, sl], cos, sin, RET_DK).astype(BF16)

    @pl.when(j >= n_rope)
    def _():
        o_ref[...] = acc.astype(BF16)


def _ret_inproj(cfg, h, g, mods, w, cos_tab, sin_tab):
    t, npt, seq_tiles = cfg["T"], cfg["NPT"], cfg["SEQ_TILES"]
    n = w.shape[1]
    tn = 1024
    n_rope = (2 * RET_HEADS * RET_DK) // tn
    (shp, shs), (scp, scs) = mods
    sh_specs, sh_args = _mod_specs(shp, shs, 0, npt, seq_tiles)
    sc_specs, sc_args = _mod_specs(scp, scs, 1, npt, seq_tiles)
    tab_spec = pl.BlockSpec(
        (1, TM, LANES),
        lambda i, j: (jnp.minimum(j, n_rope - 1), jnp.where(i < npt, i % seq_tiles, seq_tiles), 0))
    return pl.pallas_call(
        functools.partial(_inproj_kernel, npt=npt, tn=tn, n_rope=n_rope),
        grid=(t // TM, n // tn),
        in_specs=[pl.BlockSpec((TM, D_MODEL), lambda i, j: (i, 0)),
                  pl.BlockSpec((1, D_MODEL), lambda i, j: (0, 0)),
                  *sh_specs, *sc_specs,
                  pl.BlockSpec((D_MODEL, tn), lambda i, j: (0, j)),
                  tab_spec, tab_spec],
        out_specs=pl.BlockSpec((TM, tn), lambda i, j: (i, j)),
        out_shape=jax.ShapeDtypeStruct((t, n), BF16),
        scratch_shapes=[pltpu.VMEM((TM, D_MODEL), BF16)],
        compiler_params=_cparams(("arbitrary", "arbitrary")),
        name="ret_inproj",
    )(h, g.reshape(1, D_MODEL), *sh_args, *sc_args, w, cos_tab, sin_tab)


def _group_norm_gate(o, gn_row, g_bf16):
    mu = jnp.mean(o, axis=-1, keepdims=True)
    var = jnp.mean(jnp.square(o - mu), axis=-1, keepdims=True)
    on = (o - mu) * lax.rsqrt(var + EPS) * gn_row
    gg = g_bf16.astype(F32)
    return (gg * jax.nn.sigmoid(gg) * on).astype(BF16)


def _ret_prompt_kernel(q_ref, k_ref, v_ref, g_ref, gn_ref, a_ref, st_ref):
    c = pl.program_id(1)
    ch = RET_CHUNK

    @pl.when(c == 0)
    def _():
        st_ref[...] = jnp.zeros_like(st_ref)

    row = lax.broadcasted_iota(jnp.int32, (ch, ch), 0)
    col = lax.broadcasted_iota(jnp.int32, (ch, ch), 1)
    causal = row >= col
    diff = jnp.where(causal, (row - col).astype(F32), 0.0)
    ridx = lax.broadcasted_iota(jnp.int32, (ch, 1), 0).astype(F32)
    for h in range(RET_HEADS):
        lg = LOG_DECAY[h]
        ks = slice(h * RET_DK, (h + 1) * RET_DK)
        vs = slice(h * RET_DV, (h + 1) * RET_DV)
        q = q_ref[:, ks]
        k = k_ref[:, ks]
        v = v_ref[:, vs]
        st = st_ref[0, h]
        decay = jnp.where(causal, jnp.exp(diff * lg), 0.0)
        s = _dot_nt(q, k) * decay
        o = _dot(s.astype(BF16), v) + jnp.exp((ridx + 1.0) * lg) * _dot(q, st.astype(BF16))
        kd = (k.astype(F32) * jnp.exp((ch - 1.0 - ridx) * lg)).astype(BF16)
        st_ref[0, h] = math.exp(ch * lg) * st + _dot_tn(kd, v)
        a_ref[:, vs] = _group_norm_gate(o, gn_ref[h:h + 1, :], g_ref[:, vs])


def _ret_prompt(cfg, z, gn):
    b, s = cfg["B"], cfg["S"]
    nc = s // RET_CHUNK
    hk = RET_HEADS * RET_DK
    hv = RET_HEADS * RET_DV
    return pl.pallas_call(
        _ret_prompt_kernel,
        grid=(b, nc),
        in_specs=[pl.BlockSpec((RET_CHUNK, hk), lambda bi, c: (bi * nc + c, 0)),
                  pl.BlockSpec((RET_CHUNK, hk), lambda bi, c: (bi * nc + c, 1)),
                  pl.BlockSpec((RET_CHUNK, hv), lambda bi, c: (bi * nc + c, 1)),
                  pl.BlockSpec((RET_CHUNK, hv), lambda bi, c: (bi * nc + c, 2)),
                  pl.BlockSpec((RET_HEADS, RET_DV), lambda bi, c: (0, 0))],
        out_specs=[pl.BlockSpec((RET_CHUNK, hv), lambda bi, c: (bi * nc + c, 0)),
                   pl.BlockSpec((1, RET_HEADS, RET_DK, RET_DV), lambda bi, c: (bi, 0, 0, 0))],
        out_shape=[jax.ShapeDtypeStruct((b * s, hv), BF16),
                   jax.ShapeDtypeStruct((b, RET_HEADS, RET_DK, RET_DV), F32)],
        compiler_params=_cparams(("arbitrary", "arbitrary")),
        name="ret_prompt",
    )(z, z, z, z, gn)


RS_SEQ = 4


def _ret_sample_kernel(q_ref, k_ref, v_ref, g_ref, gn_ref, st_in_ref, a_ref, st_out_ref, *, ds):
    rows = RS_SEQ * ds
    shift = ds.bit_length() - 1
    row = lax.broadcasted_iota(jnp.int32, (rows, rows), 0)
    col = lax.broadcasted_iota(jnp.int32, (rows, rows), 1)
    causal = jnp.where(row >= col, (row >> shift) - (col >> shift), -1) == 0
    diff = jnp.where(causal, (row - col).astype(F32), 0.0)
    r1 = lax.broadcasted_iota(jnp.int32, (rows, 1), 0)
    t_idx = (r1 & (ds - 1)).astype(F32)
    seq_of_row = r1 >> shift
    for h in range(RET_HEADS):
        lg = LOG_DECAY[h]
        ks = slice(h * RET_DK, (h + 1) * RET_DK)
        vs = slice(h * RET_DV, (h + 1) * RET_DV)
        q = q_ref[:, ks]
        k = k_ref[:, ks]
        v = v_ref[:, vs]
        decay = jnp.where(causal, jnp.exp(diff * lg), 0.0)
        s = _dot_nt(q, k) * decay
        o = _dot(s.astype(BF16), v)
        qd = jnp.exp((t_idx + 1.0) * lg)
        kd = k.astype(F32) * jnp.exp((ds - 1.0 - t_idx) * lg)
        for sq in range(RS_SEQ):
            st = st_in_ref[sq, h]
            mine = seq_of_row == sq
            o = o + jnp.where(mine, qd * _dot(q, st.astype(BF16)), 0.0)
            kds = jnp.where(mine, kd, 0.0).astype(BF16)
            st_out_ref[sq, h] = math.exp(ds * lg) * st + _dot_tn(kds, v)
        a_ref[:, vs] = _group_norm_gate(o, gn_ref[h:h + 1, :], g_ref[:, vs])


def _ret_sample(cfg, z, gn, state):
    db, ds, npr = cfg["DB"], cfg["DS"], cfg["NP"]
    rows = RS_SEQ * ds
    assert ds & (ds - 1) == 0 and db % RS_SEQ == 0 and npr % rows == 0
    base = npr // rows
    hk = RET_HEADS * RET_DK
    hv = RET_HEADS * RET_DV
    st_spec = pl.BlockSpec((RS_SEQ, RET_HEADS, RET_DK, RET_DV), lambda i: (i, 0, 0, 0))
    return pl.pallas_call(
        functools.partial(_ret_sample_kernel, ds=ds),
        grid=(db // RS_SEQ,),
        in_specs=[pl.BlockSpec((rows, hk), lambda i: (base + i, 0)),
                  pl.BlockSpec((rows, hk), lambda i: (base + i, 1)),
                  pl.BlockSpec((rows, hv), lambda i: (base + i, 1)),
                  pl.BlockSpec((rows, hv), lambda i: (base + i, 2)),
                  pl.BlockSpec((RET_HEADS, RET_DV), lambda i: (0, 0)),
                  st_spec],
        out_specs=[pl.BlockSpec((rows, hv), lambda i: (i, 0)), st_spec],
        out_shape=[jax.ShapeDtypeStruct((db * ds, hv), BF16),
                   jax.ShapeDtypeStruct((db, RET_HEADS, RET_DK, RET_DV), F32)],
        compiler_params=_cparams(("arbitrary",)),
        name="ret_sample",
    )(z, z, z, z, gn, state)


def _mm_res_kernel(ap_ref, as_ref, w_ref, res_ref, gp, gs, o_ref, *, npt):
    i = pl.program_id(0)
    gate = _pick(i < npt, gp, gs)

    def run(a_ref):
        o_ref[...] = res_ref[...] + gate * _dot(a_ref[...], w_ref[...])

    pl.when(i < npt)(lambda: run(ap_ref))
    pl.when(i >= npt)(lambda: run(as_ref))


def _mm_res(cfg, a_p, a_s, w, res, gate_mods, chunk):
    t, npt, seq_tiles = cfg["T"], cfg["NPT"], cfg["SEQ_TILES"]
    k = w.shape[0]
    g_specs, g_args = _mod_specs(*gate_mods, chunk, npt, seq_tiles)
    return pl.pallas_call(
        functools.partial(_mm_res_kernel, npt=npt),
        grid=(t // TM,),
        in_specs=[pl.BlockSpec((TM, k), lambda i: (jnp.minimum(i, npt - 1), 0)),
                  pl.BlockSpec((TM, k), lambda i: (0, 0)),
                  pl.BlockSpec((k, D_MODEL), lambda i: (0, 0)),
                  pl.BlockSpec((TM, D_MODEL), lambda i: (i, 0)),
                  *g_specs],
        out_specs=pl.BlockSpec((TM, D_MODEL), lambda i: (i, 0)),
        out_shape=jax.ShapeDtypeStruct((t, D_MODEL), F32),
        compiler_params=_cparams(("arbitrary",)),
        name="mm_res",
    )(a_p, a_s, w, res, *g_args)


KV_A_PAD = KV_LORA + LANES


def _kv_kernel(x_ref, g_ref, shp, shs, scp, scs, wa_ref, gck_ref, cos_ref, sin_ref, wukv_ref,
               ckv_ref, kr_ref, kcat_ref, v_ref, *, npt):
    i = pl.program_id(0)
    xn = _norm_mod(i, npt, x_ref, g_ref, shp, shs, scp, scs).astype(BF16)
    z = _dot(xn, wa_ref[...])
    ckv = _rms(z[:, :KV_LORA], gck_ref[...])
    ckv_ref[...] = ckv
    kr = _rope_group(z[:, KV_LORA:], cos_ref[...], sin_ref[...], QK_ROPE)[:, :QK_ROPE]
    kr_ref[...] = kr
    up = _dot(ckv.astype(BF16), wukv_ref[...])
    kr16 = kr.astype(BF16)
    for h in range(MLA_HEADS):
        kcat_ref[h, :, :QK_NOPE] = up[:, h * QK_NOPE:(h + 1) * QK_NOPE].astype(BF16)
        kcat_ref[h, :, QK_NOPE:] = kr16
    v_ref[...] = up[:, MLA_HEADS * QK_NOPE:].astype(BF16)


def _mla_kv(cfg, h, g, mods, wa_pad, g_ckv, cos_tab, sin_tab, w_ukv):
    t, npt, seq_tiles = cfg["T"], cfg["NPT"], cfg["SEQ_TILES"]
    (shp, shs), (scp, scs) = mods
    sh_specs, sh_args = _mod_specs(shp, shs, 0, npt, seq_tiles)
    sc_specs, sc_args = _mod_specs(scp, scs, 1, npt, seq_tiles)
    tab_spec = pl.BlockSpec((TM, LANES), lambda i: (jnp.where(i < npt, i % seq_tiles, seq_tiles), 0))
    n_up = w_ukv.shape[1]
    return pl.pallas_call(
        functools.partial(_kv_kernel, npt=npt),
        grid=(t // TM,),
        in_specs=[pl.BlockSpec((TM, D_MODEL), lambda i: (i, 0)),
                  pl.BlockSpec((1, D_MODEL), lambda i: (0, 0)),
                  *sh_specs, *sc_specs,
                  pl.BlockSpec((D_MODEL, KV_A_PAD), lambda i: (0, 0)),
                  pl.BlockSpec((1, KV_LORA), lambda i: (0, 0)),
                  tab_spec, tab_spec,
                  pl.BlockSpec((KV_LORA, n_up), lambda i: (0, 0))],
        out_specs=[pl.BlockSpec((TM, KV_LORA), lambda i: (i, 0)),
                   pl.BlockSpec((TM, QK_ROPE), lambda i: (i, 0)),
                   pl.BlockSpec((MLA_HEADS, TM, QK_HEAD), lambda i: (0, i, 0)),
                   pl.BlockSpec((TM, MLA_HEADS * V_HEAD), lambda i: (i, 0))],
        out_shape=[jax.ShapeDtypeStruct((t, KV_LORA), F32),
                   jax.ShapeDtypeStruct((t, QK_ROPE), F32),
                   jax.ShapeDtypeStruct((MLA_HEADS, t, QK_HEAD), BF16),
                   jax.ShapeDtypeStruct((t, MLA_HEADS * V_HEAD), BF16)],
        compiler_params=_cparams(("arbitrary",)),
        name="mla_kv",
    )(h, g.reshape(1, D_MODEL), *sh_args, *sc_args, wa_pad, g_ckv.reshape(1, KV_LORA),
      cos_tab, sin_tab, w_ukv)


def _dq_kernel(x_ref, g_ref, shp, shs, scp, scs, w_ref, gcq_ref, o_ref, *, npt):
    i = pl.program_id(0)
    xn = _norm_mod(i, npt, x_ref, g_ref, shp, shs, scp, scs).astype(BF16)
    o_ref[...] = _rms(_dot(xn, w_ref[...]), gcq_ref[...]).astype(BF16)


def _mla_dq(cfg, h, g, mods, w_dq, g_cq):
    t, npt, seq_tiles = cfg["T"], cfg["NPT"], cfg["SEQ_TILES"]
    (shp, shs), (scp, scs) = mods
    sh_specs, sh_args = _mod_specs(shp, shs, 0, npt, seq_tiles)
    sc_specs, sc_args = _mod_specs(scp, scs, 1, npt, seq_tiles)
    return pl.pallas_call(
        functools.partial(_dq_kernel, npt=npt),
        grid=(t // TM,),
        in_specs=[pl.BlockSpec((TM, D_MODEL), lambda i: (i, 0)),
                  pl.BlockSpec((1, D_MODEL), lambda i: (0, 0)),
                  *sh_specs, *sc_specs,
                  pl.BlockSpec((D_MODEL, Q_LORA), lambda i: (0, 0)),
                  pl.BlockSpec((1, Q_LORA), lambda i: (0, 0))],
        out_specs=pl.BlockSpec((TM, Q_LORA), lambda i: (i, 0)),
        out_shape=jax.ShapeDtypeStruct((t, Q_LORA), BF16),
        compiler_params=_cparams(("arbitrary",)),
        name="mla_dq",
    )(h, g.reshape(1, D_MODEL), *sh_args, *sc_args, w_dq, g_cq.reshape(1, Q_LORA))


def _uq_kernel(cq_ref, w_ref, cos_ref, sin_ref, o_ref):
    z = _dot(cq_ref[...], w_ref[...])
    n_nope = MLA_HEADS * QK_NOPE
    for h in range(MLA_HEADS):
        o_ref[h, :, :QK_NOPE] = z[:, h * QK_NOPE:(h + 1) * QK_NOPE].astype(BF16)
    cos = cos_ref[...]
    sin = sin_ref[...]
    heads_per_group = LANES // QK_ROPE
    for c in range(MLA_HEADS // heads_per_group):
        r = _rope_group(z[:, n_nope + c * LANES:n_nope + (c + 1) * LANES], cos, sin, QK_ROPE).astype(BF16)
        for u in range(heads_per_group):
            o_ref[c * heads_per_group + u, :, QK_NOPE:] = r[:, u * QK_ROPE:(u + 1) * QK_ROPE]


def _mla_uq(cfg, cq, w_uq_perm, cos_tab, sin_tab):
    t, npt, seq_tiles = cfg["T"], cfg["NPT"], cfg["SEQ_TILES"]
    n = w_uq_perm.shape[1]
    tab_spec = pl.BlockSpec((TM, LANES), lambda i: (jnp.where(i < npt, i % seq_tiles, seq_tiles), 0))
    return pl.pallas_call(
        _uq_kernel,
        grid=(t // TM,),
        in_specs=[pl.BlockSpec((TM, Q_LORA), lambda i: (i, 0)),
                  pl.BlockSpec((Q_LORA, n), lambda i: (0, 0)),
                  tab_spec, tab_spec],
        out_specs=pl.BlockSpec((MLA_HEADS, TM, QK_HEAD), lambda i: (0, i, 0)),
        out_shape=jax.ShapeDtypeStruct((MLA_HEADS, t, QK_HEAD), BF16),
        compiler_params=_cparams(("arbitrary",)),
        name="mla_uq",
    )(cq, w_uq_perm, cos_tab, sin_tab)


def _flash_kernel(q_ref, k_ref, v_ref, o_ref):
    qi = pl.program_id(2)
    q = q_ref[0]
    tq = q.shape[0]

    def step(ki, carry, masked):
        m, l, acc = carry
        start = pl.multiple_of(ki * ATT_T, ATT_T)
        k = k_ref[0, pl.ds(start, ATT_T), :]
        v = v_ref[pl.ds(start, ATT_T), :]
        s = _dot_nt(q, k) * SM_SCALE
        if masked:
            row = lax.broadcasted_iota(jnp.int32, s.shape, 0)
            col = lax.broadcasted_iota(jnp.int32, s.shape, 1)
            s = jnp.where(col <= row, s, NEG_INF)
        m_new = jnp.maximum(m, jnp.max(s, axis=-1, keepdims=True))
        alpha = jnp.exp(m - m_new)
        p = jnp.exp(s - m_new)
        l = alpha * l + jnp.sum(p, axis=-1, keepdims=True)
        acc = alpha * acc + _dot(p.astype(BF16), v)
        return m_new, l, acc

    init = (jnp.full((tq, 1), NEG_INF, F32), jnp.zeros((tq, 1), F32), jnp.zeros((tq, V_HEAD), F32))
    carry = lax.fori_loop(0, qi, lambda ki, c: step(ki, c, False), init)
    _, l, acc = step(qi, carry, True)
    o_ref[...] = (acc / l).astype(BF16)


def _flash(cfg, q_cat, k_cat, v):
    b, s, npr = cfg["B"], cfg["S"], cfg["NP"]
    nq = s // ATT_T
    return pl.pallas_call(
        _flash_kernel,
        grid=(b, MLA_HEADS, nq),
        in_specs=[pl.BlockSpec((1, ATT_T, QK_HEAD), lambda bi, h, qi: (h, bi * nq + qi, 0)),
                  pl.BlockSpec((1, s, QK_HEAD), lambda bi, h, qi: (h, bi, 0)),
                  pl.BlockSpec((s, V_HEAD), lambda bi, h, qi: (bi, h))],
        out_specs=pl.BlockSpec((ATT_T, V_HEAD), lambda bi, h, qi: (bi * nq + qi, h)),
        out_shape=jax.ShapeDtypeStruct((npr, MLA_HEADS * V_HEAD), BF16),
        compiler_params=_cparams(("arbitrary", "arbitrary", "arbitrary")),
        name="mla_flash",
    )(q_cat, k_cat, v)


def _bmm_kernel(x_ref, w_ref, o_ref):
    o_ref[0] = _dot(x_ref[0], w_ref[0]).astype(o_ref.dtype)


def _head_bmm(x, w):
    hh, m, k = x.shape
    n = w.shape[2]
    return pl.pallas_call(
        _bmm_kernel,
        grid=(hh,),
        in_specs=[pl.BlockSpec((1, m, k), lambda h: (h, 0, 0)),
                  pl.BlockSpec((1, k, n), lambda h: (h, 0, 0))],
        out_specs=pl.BlockSpec((1, m, n), lambda h: (h, 0, 0)),
        out_shape=jax.ShapeDtypeStruct((hh, m, n), BF16),
        compiler_params=_cparams(("arbitrary",)),
        name="head_bmm",
    )(x, w)


NEW_PAD = 16


def _decode_kernel(pt_ref, ql_ref, qr_ref, ckn_ref, krn_ref, ck_hbm, kr_hbm, o_ref,
                   ckbuf, krbuf, sem, *, n_pages, npg, page, ds):
    b = pl.program_id(0)
    nch = n_pages // npg
    nk = npg * page

    def start(c, slot):
        for p in range(npg):
            pg = pt_ref[b, c * npg + p]
            pltpu.make_async_copy(ck_hbm.at[pg], ckbuf.at[slot, p], sem.at[0, slot]).start()
            pltpu.make_async_copy(kr_hbm.at[pg], krbuf.at[slot, p], sem.at[1, slot]).start()

    def wait(slot):
        pltpu.make_async_copy(ck_hbm.at[pl.ds(0, npg)], ckbuf.at[slot], sem.at[0, slot]).wait()
        pltpu.make_async_copy(kr_hbm.at[pl.ds(0, npg)], krbuf.at[slot], sem.at[1, slot]).wait()

    ql = ql_ref[0]
    qr = qr_ref[0]
    nq = ql.shape[0]

    def update(carry, s, vals):
        m, l, acc = carry
        m_new = jnp.maximum(m, jnp.max(s, axis=-1, keepdims=True))
        alpha = jnp.exp(m - m_new)
        p = jnp.exp(s - m_new)
        l = alpha * l + jnp.sum(p, axis=-1, keepdims=True)
        acc = alpha * acc + _dot(p.astype(BF16), vals)
        return m_new, l, acc

    start(0, 0)

    def body(c, carry):
        slot = c & 1

        @pl.when(c + 1 < nch)
        def _():
            start(c + 1, 1 - slot)

        wait(slot)
        ck = ckbuf[slot].reshape(nk, KV_LORA).astype(BF16)
        kr = krbuf[slot].reshape(nk, QK_ROPE).astype(BF16)
        s = (_dot_nt(ql, ck) + _dot_nt(qr, kr)) * SM_SCALE
        return update(carry, s, ck)

    init = (jnp.full((nq, 1), NEG_INF, F32), jnp.zeros((nq, 1), F32), jnp.zeros((nq, KV_LORA), F32))
    carry = lax.fori_loop(0, nch, body, init)
    ckn = ckn_ref[0].astype(BF16)
    krn = krn_ref[0].astype(BF16)
    s = (_dot_nt(ql, ckn) + _dot_nt(qr, krn)) * SM_SCALE
    row_t = lax.broadcasted_iota(jnp.int32, s.shape, 0) & (ds - 1)
    col = lax.broadcasted_iota(jnp.int32, s.shape, 1)
    s = jnp.where(col <= row_t, s, NEG_INF)
    _, l, acc = update(carry, s, ckn)
    o_ref[0] = (acc / l).astype(BF16)


def _decode(cfg, page_table, ql, qr, ckn, krn, cache_ckv, cache_krope):
    db, ds = cfg["DB"], cfg["DS"]
    n_pages = page_table.shape[1]
    page = cache_ckv.shape[1]
    npg = math.gcd(n_pages, 16)
    nq = ql.shape[1]
    grid_spec = pltpu.PrefetchScalarGridSpec(
        num_scalar_prefetch=1,
        grid=(db,),
        in_specs=[pl.BlockSpec((1, nq, KV_LORA), lambda b, pt: (b, 0, 0)),
                  pl.BlockSpec((1, nq, QK_ROPE), lambda b, pt: (b, 0, 0)),
                  pl.BlockSpec((1, NEW_PAD, KV_LORA), lambda b, pt: (b, 0, 0)),
                  pl.BlockSpec((1, NEW_PAD, QK_ROPE), lambda b, pt: (b, 0, 0)),
                  pl.BlockSpec(memory_space=pl.ANY),
                  pl.BlockSpec(memory_space=pl.ANY)],
        out_specs=pl.BlockSpec((1, nq, KV_LORA), lambda b, pt: (b, 0, 0)),
        scratch_shapes=[pltpu.VMEM((2, npg, page, KV_LORA), F32),
                        pltpu.VMEM((2, npg, page, QK_ROPE), F32),
                        pltpu.SemaphoreType.DMA((2, 2))],
    )
    return pl.pallas_call(
        functools.partial(_decode_kernel, n_pages=n_pages, npg=npg, page=page, ds=ds),
        grid_spec=grid_spec,
        out_shape=jax.ShapeDtypeStruct((db, nq, KV_LORA), BF16),
        compiler_params=_cparams(("arbitrary",)),
        name="mla_decode",
    )(page_table, ql, qr, ckn, krn, cache_ckv, cache_krope)


def _router_kernel(x_ref, g_ref, shp, shs, scp, scs, wr_ref, br_ref, xn_ref, idx_ref, gate_ref, *, npt):
    i = pl.program_id(0)
    xn = _norm_mod(i, npt, x_ref, g_ref, shp, shs, scp, scs)
    xn_ref[...] = xn
    logits = jnp.dot(xn, wr_ref[...], preferred_element_type=F32,
                     precision=lax.Precision.HIGHEST) + br_ref[...]
    lane_e = lax.broadcasted_iota(jnp.int32, logits.shape, 1).astype(F32)
    lane_o = lax.broadcasted_iota(jnp.int32, (logits.shape[0], LANES), 1)
    idx_out = jnp.zeros((logits.shape[0], LANES), F32)
    val_out = jnp.zeros((logits.shape[0], LANES), F32)
    vals = []
    for kk in range(TOP_K):
        m = jnp.max(logits, axis=-1, keepdims=True)
        sel = jnp.min(jnp.where(logits == m, lane_e, float(N_EXPERTS)), axis=-1, keepdims=True)
        logits = jnp.where(lane_e == sel, -jnp.inf, logits)
        idx_out = jnp.where(lane_o == kk, sel, idx_out)
        vals.append(m)
    es = [jnp.exp(v - vals[0]) for v in vals]
    den = es[0] + es[1] + es[2] + es[3]
    for kk in range(TOP_K):
        val_out = jnp.where(lane_o == kk, es[kk] / den, val_out)
    idx_ref[...] = idx_out.astype(jnp.int32)
    gate_ref[...] = val_out


def _router(cfg, h, g, mods, w_r, b_r):
    t, npt, seq_tiles = cfg["T"], cfg["NPT"], cfg["SEQ_TILES"]
    (shp, shs), (scp, scs) = mods
    sh_specs, sh_args = _mod_specs(shp, shs, 3, npt, seq_tiles)
    sc_specs, sc_args = _mod_specs(scp, scs, 4, npt, seq_tiles)
    return pl.pallas_call(
        functools.partial(_router_kernel, npt=npt),
        grid=(t // TM,),
        in_specs=[pl.BlockSpec((TM, D_MODEL), lambda i: (i, 0)),
                  pl.BlockSpec((1, D_MODEL), lambda i: (0, 0)),
                  *sh_specs, *sc_specs,
                  pl.BlockSpec((D_MODEL, N_EXPERTS), lambda i: (0, 0)),
                  pl.BlockSpec((1, N_EXPERTS), lambda i: (0, 0))],
        out_specs=[pl.BlockSpec((TM, D_MODEL), lambda i: (i, 0)),
                   pl.BlockSpec((TM, LANES), lambda i: (i, 0)),
                   pl.BlockSpec((TM, LANES), lambda i: (i, 0))],
        out_shape=[jax.ShapeDtypeStruct((t, D_MODEL), F32),
                   jax.ShapeDtypeStruct((t, LANES), jnp.int32),
                   jax.ShapeDtypeStruct((t, LANES), F32)],
        compiler_params=_cparams(("arbitrary",)),
        name="moe_router",
    )(h, g.reshape(1, D_MODEL), *sh_args, *sc_args, w_r, b_r.reshape(1, N_EXPERTS))


def _expert_kernel(be_ref, bv_ref, idx_ref, x_hbm, wgu_ref, bgu_ref, wdn_ref, bdn_ref, y_hbm,
                   xbuf, ybuf, sem, *, n_tok):
    i = pl.program_id(0)

    n_valid = bv_ref[i]

    @pl.when(n_valid > 0)
    def _():
        def gather(r, carry):
            tok = jnp.minimum(idx_ref[0, 0, r] >> 2, n_tok - 1)
            pltpu.make_async_copy(x_hbm.at[pl.ds(tok, 1)], xbuf.at[pl.ds(r, 1)], sem.at[0]).start()
            return carry

        lax.fori_loop(0, MOE_ROWS, gather, 0)
        pltpu.make_async_copy(x_hbm.at[pl.ds(0, MOE_ROWS)], xbuf, sem.at[0]).wait()
        xb = xbuf[...].astype(BF16)
        hgu = _dot(xb, wgu_ref[0]) + bgu_ref[0]
        gate = jnp.minimum(hgu[:, :D_EXPERT], SWIGLU_LIMIT)
        up = jnp.clip(hgu[:, D_EXPERT:], -SWIGLU_LIMIT, SWIGLU_LIMIT)
        act = (up + 1.0) * gate * jax.nn.sigmoid(SWIGLU_ALPHA * gate)
        ybuf[...] = _dot(act.astype(BF16), wdn_ref[0]) + bdn_ref[0]

        def scatter(r, carry):
            dst = idx_ref[0, 0, r]
            pltpu.make_async_copy(ybuf.at[pl.ds(r, 1)], y_hbm.at[pl.ds(dst, 1)], sem.at[1]).start()
            return carry

        lax.fori_loop(0, n_valid, scatter, 0)
        for bit in range(MOE_ROWS.bit_length()):
            rows = 1 << bit

            @pl.when((n_valid & rows) != 0)
            def _():
                pltpu.make_async_copy(ybuf.at[pl.ds(0, rows)], y_hbm.at[pl.ds(0, rows)], sem.at[1]).wait()


def _experts(cfg, xn, blk_exp, blk_valid, dst_rows, w_gu, b_gu, w_dn, b_dn):
    t = cfg["T"]
    n_blocks = blk_exp.shape[0]
    n_slots = t * TOP_K
    grid_spec = pltpu.PrefetchScalarGridSpec(
        num_scalar_prefetch=2,
        grid=(n_blocks,),
        in_specs=[pl.BlockSpec((1, 1, MOE_ROWS), lambda i, be, bv: (i, 0, 0), memory_space=pltpu.SMEM),
                  pl.BlockSpec(memory_space=pl.ANY),
                  pl.BlockSpec((1, D_MODEL, 2 * D_EXPERT), lambda i, be, bv: (be[i], 0, 0)),
                  pl.BlockSpec((1, 1, 2 * D_EXPERT), lambda i, be, bv: (be[i], 0, 0)),
                  pl.BlockSpec((1, D_EXPERT, D_MODEL), lambda i, be, bv: (be[i], 0, 0)),
                  pl.BlockSpec((1, 1, D_MODEL), lambda i, be, bv: (be[i], 0, 0))],
        out_specs=pl.BlockSpec(memory_space=pl.ANY),
        scratch_shapes=[pltpu.VMEM((MOE_ROWS, D_MODEL), F32),
                        pltpu.VMEM((MOE_ROWS, D_MODEL), F32),
                        pltpu.SemaphoreType.DMA((2,))],
    )
    return pl.pallas_call(
        functools.partial(_expert_kernel, n_tok=t),
        grid_spec=grid_spec,
        out_shape=jax.ShapeDtypeStruct((n_slots, D_MODEL), F32),
        compiler_params=_cparams(("arbitrary",)),
        name="moe_experts",
    )(blk_exp, blk_valid, dst_rows.reshape(n_blocks, 1, MOE_ROWS), xn, w_gu,
      b_gu.reshape(N_EXPERTS, 1, 2 * D_EXPERT), w_dn, b_dn.reshape(N_EXPERTS, 1, D_MODEL))


def _combine_kernel(h_ref, y_ref, gates_ref, gp, gs, o_ref, *, npt):
    i = pl.program_id(0)
    gates = gates_ref[...]
    y = gates[:, 0:1] * y_ref[:, 0:D_MODEL]
    for kk in range(1, TOP_K):
        y = y + gates[:, kk:kk + 1] * y_ref[:, kk * D_MODEL:(kk + 1) * D_MODEL]
    o_ref[...] = h_ref[...] + _pick(i < npt, gp, gs) * y


def _combine(cfg, h, yk, gates, gate_mods):
    t, npt, seq_tiles = cfg["T"], cfg["NPT"], cfg["SEQ_TILES"]
    g_specs, g_args = _mod_specs(*gate_mods, 5, npt, seq_tiles)
    y2 = yk.reshape(yk.shape[0] // TOP_K, TOP_K * D_MODEL)
    return pl.pallas_call(
        functools.partial(_combine_kernel, npt=npt),
        grid=(t // TM,),
        in_specs=[pl.BlockSpec((TM, D_MODEL), lambda i: (i, 0)),
                  pl.BlockSpec((TM, TOP_K * D_MODEL), lambda i: (i, 0)),
                  pl.BlockSpec((TM, LANES), lambda i: (i, 0)),
                  *g_specs],
        out_specs=pl.BlockSpec((TM, D_MODEL), lambda i: (i, 0)),
        out_shape=jax.ShapeDtypeStruct((t, D_MODEL), F32),
        compiler_params=_cparams(("arbitrary",)),
        name="moe_combine",
    )(h, y2, gates, *g_args)


def _dispatch_tables(cfg, top_i):
    t = cfg["T"]
    n_asg = t * TOP_K
    flat_e = top_i.reshape(-1)
    order = jnp.argsort(flat_e, stable=True).astype(jnp.int32)
    sorted_e = flat_e[order]
    counts = jnp.bincount(flat_e, length=N_EXPERTS).astype(jnp.int32)
    padded = (counts + MOE_ROWS - 1) // MOE_ROWS * MOE_ROWS
    pad_end = jnp.cumsum(padded)
    pad_start = pad_end - padded
    grp_start = jnp.cumsum(counts) - counts
    dest = pad_start[sorted_e] + jnp.arange(n_asg, dtype=jnp.int32) - grp_start[sorted_e]
    n_blocks = -(-n_asg // MOE_ROWS) + N_EXPERTS
    dst_rows = jnp.zeros((n_blocks * MOE_ROWS,), jnp.int32).at[dest].set(order)
    blk_start = jnp.arange(n_blocks, dtype=jnp.int32) * MOE_ROWS
    blk_exp = jnp.minimum(jnp.searchsorted(pad_end, blk_start, side="right"), N_EXPERTS - 1).astype(jnp.int32)
    blk_valid = jnp.clip((pad_start + counts)[blk_exp] - blk_start, 0, MOE_ROWS).astype(jnp.int32)
    return blk_exp, blk_valid, dst_rows


def _moe(cfg, h, g, mods, gate_mods, w_r, b_r, w_gu, b_gu, w_dn, b_dn):
    xn, idx, gates = _router(cfg, h, g, mods, w_r, b_r)
    blk_exp, blk_valid, dst_rows = _dispatch_tables(cfg, idx[:, :TOP_K])
    yk = _experts(cfg, xn, blk_exp, blk_valid, dst_rows, w_gu, b_gu, w_dn, b_dn)
    return _combine(cfg, h, yk, gates, gate_mods)


def _final_kernel(x_ref, g_ref, shp, shs, scp, scs, o_ref, *, npt):
    i = pl.program_id(0)
    o_ref[...] = _norm_mod(i, npt, x_ref, g_ref, shp, shs, scp, scs)


def _final(cfg, h, g, mods):
    t, npt, seq_tiles = cfg["T"], cfg["NPT"], cfg["SEQ_TILES"]
    (shp, shs), (scp, scs) = mods
    sh_specs, sh_args = _mod_specs(shp, shs, 0, npt, seq_tiles)
    sc_specs, sc_args = _mod_specs(scp, scs, 1, npt, seq_tiles)
    return pl.pallas_call(
        functools.partial(_final_kernel, npt=npt),
        grid=(t // TM,),
        in_specs=[pl.BlockSpec((TM, D_MODEL), lambda i: (i, 0)),
                  pl.BlockSpec((1, D_MODEL), lambda i: (0, 0)),
                  *sh_specs, *sc_specs],
        out_specs=pl.BlockSpec((TM, D_MODEL), lambda i: (i, 0)),
        out_shape=jax.ShapeDtypeStruct((t, D_MODEL), F32),
        compiler_params=_cparams(("arbitrary",)),
        name="final_norm",
    )(h, g.reshape(1, D_MODEL), *sh_args, *sc_args)


def _rope_tables(pos, group):
    half = group // 2
    inv = ROPE_BASE ** (-jnp.arange(half, dtype=F32) / half)
    ang = pos.astype(F32)[:, None] * inv[None, :]
    cos = jnp.cos(ang)
    sin = jnp.sin(ang)
    reps = LANES // group
    cos_t = jnp.tile(jnp.concatenate([cos, cos], axis=-1), (1, reps))
    sin_t = jnp.tile(jnp.concatenate([-sin, sin], axis=-1), (1, reps))
    return cos_t, sin_t


def _split_mods(cfg, ada):
    b, ds = cfg["B"], cfg["DS"]
    return ada[:b].reshape(b, 1, ada.shape[1]), jnp.repeat(ada[b:], ds, axis=0)


def kernel(x_prompt, x_sample, c_prompt, c_sample, state_ret, cache_ckv, cache_krope, page_table,
           w_ada, b_ada, g_mix, g_ffn, ret_w_in, ret_gn, ret_w_o,
           w_ada_kv, b_ada_kv, g_kv_in, mla_w_kv_a, g_ckv, mla_w_uk, mla_w_uv,
           mla_w_dq, g_cq, mla_w_uq, mla_w_o,
           w_router, b_router, w_gu, b_gu, w_dn, b_dn,
           w_ada_f, b_ada_f, g_final):
    b, s, d = x_prompt.shape
    db, ds, _ = x_sample.shape
    assert d == D_MODEL and db * ds == TM and s % TM == 0 and s % ATT_T == 0
    npr = b * s
    cfg = dict(B=b, S=s, DB=db, DS=ds, NP=npr, T=npr + TM, NPT=npr // TM, SEQ_TILES=s // TM)
    past_len = page_table.shape[1] * cache_ckv.shape[1]

    h = jnp.concatenate([x_prompt.reshape(npr, d), x_sample.reshape(TM, d)], axis=0)
    c_all = jnp.concatenate([c_prompt, c_sample], axis=0)

    pos = jnp.concatenate([jnp.arange(s, dtype=jnp.int32),
                           jnp.tile(past_len + jnp.arange(ds, dtype=jnp.int32), db)])
    cos_r, sin_r = _rope_tables(pos, RET_DK)
    k_scale = RET_DK ** -0.5
    ret_cos = jnp.stack([cos_r, cos_r * k_scale])
    ret_sin = jnp.stack([sin_r, sin_r * k_scale])
    mla_cos, mla_sin = _rope_tables(pos, QK_ROPE)

    mods_kv = _split_mods(cfg, _adaln(c_all, w_ada_kv, b_ada_kv))
    mods_f = _split_mods(cfg, _adaln(c_all, w_ada_f, b_ada_f))

    n_a = state_ret.shape[0]
    depth = w_ada.shape[0]
    ret_p, ret_s = [], []
    outs_kv = None
    for l in range(depth):
        mods = _split_mods(cfg, _adaln(c_all, w_ada, b_ada, layer=l))
        mm = (mods, mods)
        if l < n_a:
            z = _ret_inproj(cfg, h, g_mix[l], mm, ret_w_in[l].astype(BF16), ret_cos, ret_sin)
            a_p, st_p = _ret_prompt(cfg, z, ret_gn[l])
            a_s, st_s = _ret_sample(cfg, z, ret_gn[l], state_ret[l])
            ret_p.append(st_p)
            ret_s.append(st_s)
            h = _mm_res(cfg, a_p, a_s, ret_w_o[l].astype(BF16), h, mods, 2)
        else:
            bl = l - n_a
            if bl == 0:
                wa_pad = jnp.pad(mla_w_kv_a, ((0, 0), (0, KV_A_PAD - mla_w_kv_a.shape[1]))).astype(BF16)
                w_ukv = jnp.concatenate([mla_w_uk.reshape(KV_LORA, -1), mla_w_uv.reshape(KV_LORA, -1)],
                                        axis=1).astype(BF16)
                ckv, kr, k_cat, v_all = _mla_kv(cfg, h, g_kv_in, (mods_kv, mods_kv), wa_pad, g_ckv,
                                                mla_cos, mla_sin, w_ukv)
                outs_kv = (ckv, kr)
                ckn = jnp.pad(ckv[npr:].reshape(db, ds, KV_LORA), ((0, 0), (0, NEW_PAD - ds), (0, 0)))
                krn = jnp.pad(kr[npr:].reshape(db, ds, QK_ROPE), ((0, 0), (0, NEW_PAD - ds), (0, 0)))
                w_uk_t = jnp.transpose(mla_w_uk, (1, 2, 0)).astype(BF16)
                w_uv_h = jnp.transpose(mla_w_uv, (1, 0, 2)).astype(BF16)
            cq = _mla_dq(cfg, h, g_mix[l], mm, mla_w_dq[bl].astype(BF16), g_cq[bl])
            w_uq = mla_w_uq[bl].reshape(Q_LORA, MLA_HEADS, QK_HEAD)
            w_uq_perm = jnp.concatenate([w_uq[:, :, :QK_NOPE].reshape(Q_LORA, -1),
                                         w_uq[:, :, QK_NOPE:].reshape(Q_LORA, -1)], axis=1).astype(BF16)
            q_cat = _mla_uq(cfg, cq, w_uq_perm, mla_cos, mla_sin)
            o_p = _flash(cfg, q_cat, k_cat, v_all)
            q_s = q_cat[:, npr:, :]
            ql = _head_bmm(q_s[:, :, :QK_NOPE], w_uk_t)
            to_seq = lambda x: x.reshape(MLA_HEADS, db, ds, x.shape[-1]).transpose(1, 0, 2, 3).reshape(
                db, MLA_HEADS * ds, x.shape[-1])
            ctx = _decode(cfg, page_table, to_seq(ql), to_seq(q_s[:, :, QK_NOPE:]), ckn, krn,
                          cache_ckv, cache_krope)
            ctx_h = ctx.reshape(db, MLA_HEADS, ds, KV_LORA).transpose(1, 0, 2, 3).reshape(MLA_HEADS, TM, KV_LORA)
            o_s = _head_bmm(ctx_h, w_uv_h).transpose(1, 0, 2).reshape(TM, MLA_HEADS * V_HEAD)
            h = _mm_res(cfg, o_p, o_s, mla_w_o[bl].astype(BF16), h, mods, 2)
        h = _moe(cfg, h, g_ffn[l], mm, mods, w_router[l], b_router[l],
                 w_gu[l].astype(BF16), b_gu[l], w_dn[l].astype(BF16), b_dn[l])

    y = _final(cfg, h, g_final, (mods_f, mods_f))
    ckv, kr = outs_kv
    return (y[:npr].reshape(b, s, d), y[npr:].reshape(db, ds, d),
            jnp.stack(ret_p, axis=0), jnp.stack(ret_s, axis=0),
            ckv[:npr].reshape(b, s, KV_LORA), kr[:npr].reshape(b, s, QK_ROPE),
            ckv[npr:].reshape(db, ds, KV_LORA), kr[npr:].reshape(db, ds, QK_ROPE))
```

```python
import functools
import math

import jax
import jax.numpy as jnp
from jax import lax
from jax.experimental import pallas as pl
from jax.experimental.pallas import tpu as pltpu

F32 = jnp.float32
BF16 = jnp.bfloat16

D_MODEL = 1024
RET_HEADS = 8
RET_DK = 128
RET_DV = 256
RET_CHUNK = 128
MLA_HEADS = 8
QK_NOPE = 128
QK_ROPE = 64
QK_HEAD = QK_NOPE + QK_ROPE
V_HEAD = 128
KV_LORA = 256
Q_LORA = 768
SM_SCALE = QK_HEAD ** -0.5
N_EXPERTS = 32
TOP_K = 4
D_EXPERT = 1024
SWIGLU_LIMIT = 7.0
SWIGLU_ALPHA = 1.702
ROPE_BASE = 10000.0
EPS = 1e-6
NEG_INF = -1e30

LANES = 128
ROW_TILE = D_MODEL // LANES
TM = 512
MOE_ROWS = 256
ATT_T = 512
VMEM_LIMIT = 56 * 1024 * 1024

LOG_DECAY = tuple(math.log1p(-2.0 ** (-5.0 - h)) for h in range(RET_HEADS))


def _cparams(sem):
    return pltpu.CompilerParams(dimension_semantics=sem, vmem_limit_bytes=VMEM_LIMIT)


def _rms(x, g):
    return x * lax.rsqrt(jnp.mean(x * x, axis=-1, keepdims=True) + EPS) * g


def _dot(a, b):
    return jnp.dot(a, b, preferred_element_type=F32)


def _dot_nt(a, b):
    return lax.dot_general(a, b, (((1,), (1,)), ((), ())), preferred_element_type=F32)


def _dot_tn(a, b):
    return lax.dot_general(a, b, (((0,), (0,)), ((), ())), preferred_element_type=F32)


def _rope_group(x, cos, sin_signed, group):
    half = group // 2
    lane = lax.broadcasted_iota(jnp.int32, x.shape, 1)
    x_up = pltpu.roll(x, LANES - half, 1)
    x_dn = pltpu.roll(x, half, 1)
    rot = jnp.where((lane & (group - 1)) < half, x_up, x_dn)
    return x * cos + rot * sin_signed


def _pick(is_prompt, p_ref, s_ref):
    return jnp.where(is_prompt, p_ref[0], s_ref[...])


def _mod_specs(mod_p, mod_s, chunk, npt, seq_tiles):
    dm = D_MODEL
    sp = pl.BlockSpec((1, 1, dm), lambda i, *_: (jnp.minimum(i, npt - 1) // seq_tiles, 0, chunk))
    ss = pl.BlockSpec((TM, dm), lambda i, *_: (0, chunk))
    return [sp, ss], [mod_p, mod_s]


def _norm_mod(i, npt, x_ref, g_ref, shp, shs, scp, scs):
    is_p = i < npt
    xn = _rms(x_ref[...], g_ref[...])
    return xn * (1.0 + _pick(is_p, scp, scs)) + _pick(is_p, shp, shs)


def _adaln_kernel(c_ref, w_ref, b_ref, o_ref):
    c = c_ref[...]
    a = (c * jax.nn.sigmoid(c)).astype(BF16)
    o_ref[...] = _dot(a, w_ref[...].astype(BF16)) + b_ref[...]


def _adaln(c_all, w, b, layer=None):
    n = w.shape[-1]
    tn = 512
    rows = c_all.shape[0]
    if layer is None:
        w_spec = pl.BlockSpec((D_MODEL, tn), lambda j: (0, j))
        b_spec = pl.BlockSpec((1, tn), lambda j: (0, j))
        b = b.reshape(1, n)
    else:
        w_spec = pl.BlockSpec((None, D_MODEL, tn), lambda j: (layer, 0, j))
        b_spec = pl.BlockSpec((None, 1, tn), lambda j: (layer, 0, j))
        b = b.reshape(b.shape[0], 1, n)
    return pl.pallas_call(
        _adaln_kernel,
        grid=(n // tn,),
        in_specs=[pl.BlockSpec((rows, D_MODEL), lambda j: (0, 0)), w_spec, b_spec],
        out_specs=pl.BlockSpec((rows, tn), lambda j: (0, j)),
        out_shape=jax.ShapeDtypeStruct((rows, n), F32),
        compiler_params=_cparams(("arbitrary",)),
        name="adaln",
    )(c_all, w, b)


def _inproj_kernel(x_ref, g_ref, shp, shs, scp, scs, w_ref, cos_ref, sin_ref, o_ref, xn_scr,
                   *, npt, tn, n_rope):
    i = pl.program_id(0)
    j = pl.program_id(1)

    @pl.when(j == 0)
    def _():
        xn_scr[...] = _norm_mod(i, npt, x_ref, g_ref, shp, shs, scp, scs).astype(BF16)

    acc = _dot(xn_scr[...], w_ref[...])

    @pl.when(j < n_rope)
    def _():
        cos = cos_ref[0]
        sin = sin_ref[0]
        for c in range(tn // LANES):
            sl = slice(c * LANES, (c + 1) * LANES)
            o_ref[:, sl] = _rope_group(acc[:, sl], cos, sin, RET_DK).astype(BF16)

    @pl.when(j >= n_rope)
    def _():
        o_ref[...] = acc.astype(BF16)


def _ret_inproj(cfg, h, g, mods, w, cos_tab, sin_tab):
    t, npt, seq_tiles = cfg["T"], cfg["NPT"], cfg["SEQ_TILES"]
    n = w.shape[1]
    tn = 1024
    n_rope = (2 * RET_HEADS * RET_DK) // tn
    (shp, shs), (scp, scs) = mods
    sh_specs, sh_args = _mod_specs(shp, shs, 0, npt, seq_tiles)
    sc_specs, sc_args = _mod_specs(scp, scs, 1, npt, seq_tiles)
    tab_spec = pl.BlockSpec(
        (1, TM, LANES),
        lambda i, j: (jnp.minimum(j, n_rope - 1), jnp.where(i < npt, i % seq_tiles, seq_tiles), 0))
    return pl.pallas_call(
        functools.partial(_inproj_kernel, npt=npt, tn=tn, n_rope=n_rope),
        grid=(t // TM, n // tn),
        in_specs=[pl.BlockSpec((TM, D_MODEL), lambda i, j: (i, 0)),
                  pl.BlockSpec((1, D_MODEL), lambda i, j: (0, 0)),
                  *sh_specs, *sc_specs,
                  pl.BlockSpec((D_MODEL, tn), lambda i, j: (0, j)),
                  tab_spec, tab_spec],
        out_specs=pl.BlockSpec((TM, tn), lambda i, j: (i, j)),
        out_shape=jax.ShapeDtypeStruct((t, n), BF16),
        scratch_shapes=[pltpu.VMEM((TM, D_MODEL), BF16)],
        compiler_params=_cparams(("arbitrary", "arbitrary")),
        name="ret_inproj",
    )(h, g.reshape(1, D_MODEL), *sh_args, *sc_args, w, cos_tab, sin_tab)


def _group_norm_gate(o, gn_row, g_bf16):
    mu = jnp.mean(o, axis=-1, keepdims=True)
    var = jnp.mean(jnp.square(o - mu), axis=-1, keepdims=True)
    on = (o - mu) * lax.rsqrt(var + EPS) * gn_row
    gg = g_bf16.astype(F32)
    return (gg * jax.nn.sigmoid(gg) * on).astype(BF16)


def _ret_prompt_kernel(q_ref, k_ref, v_ref, g_ref, gn_ref, a_ref, st_ref):
    c = pl.program_id(1)
    ch = RET_CHUNK

    @pl.when(c == 0)
    def _():
        st_ref[...] = jnp.zeros_like(st_ref)

    row = lax.broadcasted_iota(jnp.int32, (ch, ch), 0)
    col = lax.broadcasted_iota(jnp.int32, (ch, ch), 1)
    causal = row >= col
    diff = jnp.where(causal, (row - col).astype(F32), 0.0)
    ridx = lax.broadcasted_iota(jnp.int32, (ch, 1), 0).astype(F32)
    for h in range(RET_HEADS):
        lg = LOG_DECAY[h]
        ks = slice(h * RET_DK, (h + 1) * RET_DK)
        vs = slice(h * RET_DV, (h + 1) * RET_DV)
        q = q_ref[:, ks]
        k = k_ref[:, ks]
        v = v_ref[:, vs]
        st = st_ref[0, h]
        decay = jnp.where(causal, jnp.exp(diff * lg), 0.0)
        s = _dot_nt(q, k) * decay
        o = _dot(s.astype(BF16), v) + jnp.exp((ridx + 1.0) * lg) * _dot(q, st.astype(BF16))
        kd = (k.astype(F32) * jnp.exp((ch - 1.0 - ridx) * lg)).astype(BF16)
        st_ref[0, h] = math.exp(ch * lg) * st + _dot_tn(kd, v)
        a_ref[:, vs] = _group_norm_gate(o, gn_ref[h:h + 1, :], g_ref[:, vs])


def _ret_prompt(cfg, z, gn):
    b, s = cfg["B"], cfg["S"]
    nc = s // RET_CHUNK
    hk = RET_HEADS * RET_DK
    hv = RET_HEADS * RET_DV
    return pl.pallas_call(
        _ret_prompt_kernel,
        grid=(b, nc),
        in_specs=[pl.BlockSpec((RET_CHUNK, hk), lambda bi, c: (bi * nc + c, 0)),
                  pl.BlockSpec((RET_CHUNK, hk), lambda bi, c: (bi * nc + c, 1)),
                  pl.BlockSpec((RET_CHUNK, hv), lambda bi, c: (bi * nc + c, 1)),
                  pl.BlockSpec((RET_CHUNK, hv), lambda bi, c: (bi * nc + c, 2)),
                  pl.BlockSpec((RET_HEADS, RET_DV), lambda bi, c: (0, 0))],
        out_specs=[pl.BlockSpec((RET_CHUNK, hv), lambda bi, c: (bi * nc + c, 0)),
                   pl.BlockSpec((1, RET_HEADS, RET_DK, RET_DV), lambda bi, c: (bi, 0, 0, 0))],
        out_shape=[jax.ShapeDtypeStruct((b * s, hv), BF16),
                   jax.ShapeDtypeStruct((b, RET_HEADS, RET_DK, RET_DV), F32)],
        compiler_params=_cparams(("arbitrary", "arbitrary")),
        name="ret_prompt",
    )(z, z, z, z, gn)


RS_SEQ = 4


def _ret_sample_kernel(q_ref, k_ref, v_ref, g_ref, gn_ref, st_in_ref, a_ref, st_out_ref, *, ds):
    rows = RS_SEQ * ds
    shift = ds.bit_length() - 1
    row = lax.broadcasted_iota(jnp.int32, (rows, rows), 0)
    col = lax.broadcasted_iota(jnp.int32, (rows, rows), 1)
    causal = jnp.where(row >= col, (row >> shift) - (col >> shift), -1) == 0
    diff = jnp.where(causal, (row - col).astype(F32), 0.0)
    r1 = lax.broadcasted_iota(jnp.int32, (rows, 1), 0)
    t_idx = (r1 & (ds - 1)).astype(F32)
    seq_of_row = r1 >> shift
    for h in range(RET_HEADS):
        lg = LOG_DECAY[h]
        ks = slice(h * RET_DK, (h + 1) * RET_DK)
        vs = slice(h * RET_DV, (h + 1) * RET_DV)
        q = q_ref[:, ks]
        k = k_ref[:, ks]
        v = v_ref[:, vs]
        decay = jnp.where(causal, jnp.exp(diff * lg), 0.0)
        s = _dot_nt(q, k) * decay
        o = _dot(s.astype(BF16), v)
        qd = jnp.exp((t_idx + 1.0) * lg)
        kd = k.astype(F32) * jnp.exp((ds - 1.0 - t_idx) * lg)
        for sq in range(RS_SEQ):
            st = st_in_ref[sq, h]
            mine = seq_of_row == sq
            o = o + jnp.where(mine, qd * _dot(q, st.astype(BF16)), 0.0)
            kds = jnp.where(mine, kd, 0.0).astype(BF16)
            st_out_ref[sq, h] = math.exp(ds * lg) * st + _dot_tn(kds, v)
        a_ref[:, vs] = _group_norm_gate(o, gn_ref[h:h + 1, :], g_ref[:, vs])


def _ret_sample(cfg, z, gn, state):
    db, ds, npr = cfg["DB"], cfg["DS"], cfg["NP"]
    rows = RS_SEQ * ds
    assert ds & (ds - 1) == 0 and db % RS_SEQ == 0 and npr % rows == 0
    base = npr // rows
    hk = RET_HEADS * RET_DK
    hv = RET_HEADS * RET_DV
    st_spec = pl.BlockSpec((RS_SEQ, RET_HEADS, RET_DK, RET_DV), lambda i: (i, 0, 0, 0))
    return pl.pallas_call(
        functools.partial(_ret_sample_kernel, ds=ds),
        grid=(db // RS_SEQ,),
        in_specs=[pl.BlockSpec((rows, hk), lambda i: (base + i, 0)),
                  pl.BlockSpec((rows, hk), lambda i: (base + i, 1)),
                  pl.BlockSpec((rows, hv), lambda i: (base + i, 1)),
                  pl.BlockSpec((rows, hv), lambda i: (base + i, 2)),
                  pl.BlockSpec((RET_HEADS, RET_DV), lambda i: (0, 0)),
                  st_spec],
        out_specs=[pl.BlockSpec((rows, hv), lambda i: (i, 0)), st_spec],
        out_shape=[jax.ShapeDtypeStruct((db * ds, hv), BF16),
                   jax.ShapeDtypeStruct((db, RET_HEADS, RET_DK, RET_DV), F32)],
        compiler_params=_cparams(("arbitrary",)),
        name="ret_sample",
    )(z, z, z, z, gn, state)


def _mm_res_kernel(ap_ref, as_ref, w_ref, res_ref, gp, gs, o_ref, *, npt):
    i = pl.program_id(0)
    gate = _pick(i < npt, gp, gs)

    def run(a_ref):
        o_ref[...] = res_ref[...] + gate * _dot(a_ref[...], w_ref[...])

    pl.when(i < npt)(lambda: run(ap_ref))
    pl.when(i >= npt)(lambda: run(as_ref))


def _mm_res(cfg, a_p, a_s, w, res, gate_mods, chunk):
    t, npt, seq_tiles = cfg["T"], cfg["NPT"], cfg["SEQ_TILES"]
    k = w.shape[0]
    g_specs, g_args = _mod_specs(*gate_mods, chunk, npt, seq_tiles)
    return pl.pallas_call(
        functools.partial(_mm_res_kernel, npt=npt),
        grid=(t // TM,),
        in_specs=[pl.BlockSpec((TM, k), lambda i: (jnp.minimum(i, npt - 1), 0)),
                  pl.BlockSpec((TM, k), lambda i: (0, 0)),
                  pl.BlockSpec((k, D_MODEL), lambda i: (0, 0)),
                  pl.BlockSpec((TM, D_MODEL), lambda i: (i, 0)),
                  *g_specs],
        out_specs=pl.BlockSpec((TM, D_MODEL), lambda i: (i, 0)),
        out_shape=jax.ShapeDtypeStruct((t, D_MODEL), F32),
        compiler_params=_cparams(("arbitrary",)),
        name="mm_res",
    )(a_p, a_s, w, res, *g_args)


KV_A_PAD = KV_LORA + LANES


def _kv_kernel(x_ref, g_ref, shp, shs, scp, scs, wa_ref, gck_ref, cos_ref, sin_ref, wukv_ref,
               ckv_ref, kr_ref, kcat_ref, v_ref, *, npt):
    i = pl.program_id(0)
    xn = _norm_mod(i, npt, x_ref, g_ref, shp, shs, scp, scs).astype(BF16)
    z = _dot(xn, wa_ref[...])
    ckv = _rms(z[:, :KV_LORA], gck_ref[...])
    ckv_ref[...] = ckv
    kr = _rope_group(z[:, KV_LORA:], cos_ref[...], sin_ref[...], QK_ROPE)[:, :QK_ROPE]
    kr_ref[...] = kr
    up = _dot(ckv.astype(BF16), wukv_ref[...])
    kr16 = kr.astype(BF16)
    for h in range(MLA_HEADS):
        kcat_ref[h, :, :QK_NOPE] = up[:, h * QK_NOPE:(h + 1) * QK_NOPE].astype(BF16)
        kcat_ref[h, :, QK_NOPE:] = kr16
    v_ref[...] = up[:, MLA_HEADS * QK_NOPE:].astype(BF16)


def _mla_kv(cfg, h, g, mods, wa_pad, g_ckv, cos_tab, sin_tab, w_ukv):
    t, npt, seq_tiles = cfg["T"], cfg["NPT"], cfg["SEQ_TILES"]
    (shp, shs), (scp, scs) = mods
    sh_specs, sh_args = _mod_specs(shp, shs, 0, npt, seq_tiles)
    sc_specs, sc_args = _mod_specs(scp, scs, 1, npt, seq_tiles)
    tab_spec = pl.BlockSpec((TM, LANES), lambda i: (jnp.where(i < npt, i % seq_tiles, seq_tiles), 0))
    n_up = w_ukv.shape[1]
    return pl.pallas_call(
        functools.partial(_kv_kernel, npt=npt),
        grid=(t // TM,),
        in_specs=[pl.BlockSpec((TM, D_MODEL), lambda i: (i, 0)),
                  pl.BlockSpec((1, D_MODEL), lambda i: (0, 0)),
                  *sh_specs, *sc_specs,
                  pl.BlockSpec((D_MODEL, KV_A_PAD), lambda i: (0, 0)),
                  pl.BlockSpec((1, KV_LORA), lambda i: (0, 0)),
                  tab_spec, tab_spec,
                  pl.BlockSpec((KV_LORA, n_up), lambda i: (0, 0))],
        out_specs=[pl.BlockSpec((TM, KV_LORA), lambda i: (i, 0)),
                   pl.BlockSpec((TM, QK_ROPE), lambda i: (i, 0)),
                   pl.BlockSpec((MLA_HEADS, TM, QK_HEAD), lambda i: (0, i, 0)),
                   pl.BlockSpec((TM, MLA_HEADS * V_HEAD), lambda i: (i, 0))],
        out_shape=[jax.ShapeDtypeStruct((t, KV_LORA), F32),
                   jax.ShapeDtypeStruct((t, QK_ROPE), F32),
                   jax.ShapeDtypeStruct((MLA_HEADS, t, QK_HEAD), BF16),
                   jax.ShapeDtypeStruct((t, MLA_HEADS * V_HEAD), BF16)],
        compiler_params=_cparams(("arbitrary",)),
        name="mla_kv",
    )(h, g.reshape(1, D_MODEL), *sh_args, *sc_args, wa_pad, g_ckv.reshape(1, KV_LORA),
      cos_tab, sin_tab, w_ukv)


def _dq_kernel(x_ref, g_ref, shp, shs, scp, scs, w_ref, gcq_ref, o_ref, *, npt):
    i = pl.program_id(0)
    xn = _norm_mod(i, npt, x_ref, g_ref, shp, shs, scp, scs).astype(BF16)
    o_ref[...] = _rms(_dot(xn, w_ref[...]), gcq_ref[...]).astype(BF16)


def _mla_dq(cfg, h, g, mods, w_dq, g_cq):
    t, npt, seq_tiles = cfg["T"], cfg["NPT"], cfg["SEQ_TILES"]
    (shp, shs), (scp, scs) = mods
    sh_specs, sh_args = _mod_specs(shp, shs, 0, npt, seq_tiles)
    sc_specs, sc_args = _mod_specs(scp, scs, 1, npt, seq_tiles)
    return pl.pallas_call(
        functools.partial(_dq_kernel, npt=npt),
        grid=(t // TM,),
        in_specs=[pl.BlockSpec((TM, D_MODEL), lambda i: (i, 0)),
                  pl.BlockSpec((1, D_MODEL), lambda i: (0, 0)),
                  *sh_specs, *sc_specs,
                  pl.BlockSpec((D_MODEL, Q_LORA), lambda i: (0, 0)),
                  pl.BlockSpec((1, Q_LORA), lambda i: (0, 0))],
        out_specs=pl.BlockSpec((TM, Q_LORA), lambda i: (i, 0)),
        out_shape=jax.ShapeDtypeStruct((t, Q_LORA), BF16),
        compiler_params=_cparams(("arbitrary",)),
        name="mla_dq",
    )(h, g.reshape(1, D_MODEL), *sh_args, *sc_args, w_dq, g_cq.reshape(1, Q_LORA))


def _uq_kernel(cq_ref, w_ref, cos_ref, sin_ref, o_ref):
    z = _dot(cq_ref[...], w_ref[...])
    n_nope = MLA_HEADS * QK_NOPE
    for h in range(MLA_HEADS):
        o_ref[h, :, :QK_NOPE] = z[:, h * QK_NOPE:(h + 1) * QK_NOPE].astype(BF16)
    cos = cos_ref[...]
    sin = sin_ref[...]
    heads_per_group = LANES // QK_ROPE
    for c in range(MLA_HEADS // heads_per_group):
        r = _rope_group(z[:, n_nope + c * LANES:n_nope + (c + 1) * LANES], cos, sin, QK_ROPE).astype(BF16)
        for u in range(heads_per_group):
            o_ref[c * heads_per_group + u, :, QK_NOPE:] = r[:, u * QK_ROPE:(u + 1) * QK_ROPE]


def _mla_uq(cfg, cq, w_uq_perm, cos_tab, sin_tab):
    t, npt, seq_tiles = cfg["T"], cfg["NPT"], cfg["SEQ_TILES"]
    n = w_uq_perm.shape[1]
    tab_spec = pl.BlockSpec((TM, LANES), lambda i: (jnp.where(i < npt, i % seq_tiles, seq_tiles), 0))
    return pl.pallas_call(
        _uq_kernel,
        grid=(t // TM,),
        in_specs=[pl.BlockSpec((TM, Q_LORA), lambda i: (i, 0)),
                  pl.BlockSpec((Q_LORA, n), lambda i: (0, 0)),
                  tab_spec, tab_spec],
        out_specs=pl.BlockSpec((MLA_HEADS, TM, QK_HEAD), lambda i: (0, i, 0)),
        out_shape=jax.ShapeDtypeStruct((MLA_HEADS, t, QK_HEAD), BF16),
        compiler_params=_cparams(("arbitrary",)),
        name="mla_uq",
    )(cq, w_uq_perm, cos_tab, sin_tab)


def _flash_kernel(q_ref, k_ref, v_ref, o_ref):
    qi = pl.program_id(2)
    q = q_ref[0]
    tq = q.shape[0]

    def step(ki, carry, masked):
        m, l, acc = carry
        start = pl.multiple_of(ki * ATT_T, ATT_T)
        k = k_ref[0, pl.ds(start, ATT_T), :]
        v = v_ref[pl.ds(start, ATT_T), :]
        s = _dot_nt(q, k) * SM_SCALE
        if masked:
            row = lax.broadcasted_iota(jnp.int32, s.shape, 0)
            col = lax.broadcasted_iota(jnp.int32, s.shape, 1)
            s = jnp.where(col <= row, s, NEG_INF)
        m_new = jnp.maximum(m, jnp.max(s, axis=-1, keepdims=True))
        alpha = jnp.exp(m - m_new)
        p = jnp.exp(s - m_new)
        l = alpha * l + jnp.sum(p, axis=-1, keepdims=True)
        acc = alpha * acc + _dot(p.astype(BF16), v)
        return m_new, l, acc

    init = (jnp.full((tq, 1), NEG_INF, F32), jnp.zeros((tq, 1), F32), jnp.zeros((tq, V_HEAD), F32))
    carry = lax.fori_loop(0, qi, lambda ki, c: step(ki, c, False), init)
    _, l, acc = step(qi, carry, True)
    o_ref[...] = (acc / l).astype(BF16)


def _flash(cfg, q_cat, k_cat, v):
    b, s, npr = cfg["B"], cfg["S"], cfg["NP"]
    nq = s // ATT_T
    return pl.pallas_call(
        _flash_kernel,
        grid=(b, MLA_HEADS, nq),
        in_specs=[pl.BlockSpec((1, ATT_T, QK_HEAD), lambda bi, h, qi: (h, bi * nq + qi, 0)),
                  pl.BlockSpec((1, s, QK_HEAD), lambda bi, h, qi: (h, bi, 0)),
                  pl.BlockSpec((s, V_HEAD), lambda bi, h, qi: (bi, h))],
        out_specs=pl.BlockSpec((ATT_T, V_HEAD), lambda bi, h, qi: (bi * nq + qi, h)),
        out_shape=jax.ShapeDtypeStruct((npr, MLA_HEADS * V_HEAD), BF16),
        compiler_params=_cparams(("arbitrary", "arbitrary", "arbitrary")),
        name="mla_flash",
    )(q_cat, k_cat, v)


def _bmm_kernel(x_ref, w_ref, o_ref):
    o_ref[0] = _dot(x_ref[0], w_ref[0]).astype(o_ref.dtype)


def _head_bmm(x, w):
    hh, m, k = x.shape
    n = w.shape[2]
    return pl.pallas_call(
        _bmm_kernel,
        grid=(hh,),
        in_specs=[pl.BlockSpec((1, m, k), lambda h: (h, 0, 0)),
                  pl.BlockSpec((1, k, n), lambda h: (h, 0, 0))],
        out_specs=pl.BlockSpec((1, m, n), lambda h: (h, 0, 0)),
        out_shape=jax.ShapeDtypeStruct((hh, m, n), BF16),
        compiler_params=_cparams(("arbitrary",)),
        name="head_bmm",
    )(x, w)


NEW_PAD = 16


DEC_SEQS = 2


def _decode_kernel(pt_ref, ql_ref, qr_ref, ckn_ref, krn_ref, ck_hbm, kr_hbm, o_ref,
                   ckbuf, krbuf, sem, *, n_pages, npg, page, ds):
    g = pl.program_id(0)
    nch = n_pages // npg
    nk = npg * page

    def start(c, slot):
        for u in range(DEC_SEQS):
            for p in range(npg):
                pg = pt_ref[g * DEC_SEQS + u, c * npg + p]
                pltpu.make_async_copy(ck_hbm.at[pg], ckbuf.at[slot, u, pl.ds(p * page, page)],
                                      sem.at[0, slot]).start()
                pltpu.make_async_copy(kr_hbm.at[pg], krbuf.at[slot, u, :, pl.ds(p * page, page)],
                                      sem.at[1, slot]).start()

    def wait(slot):
        pltpu.make_async_copy(ckbuf.at[slot], ckbuf.at[slot], sem.at[0, slot]).wait()
        pltpu.make_async_copy(krbuf.at[slot], krbuf.at[slot], sem.at[1, slot]).wait()

    qls = [ql_ref[u] for u in range(DEC_SEQS)]
    qrs = [qr_ref[u] for u in range(DEC_SEQS)]
    nq = qls[0].shape[0]

    def update(carry, s, vals):
        m, l, acc = carry
        m_new = jnp.maximum(m, jnp.max(s, axis=-1, keepdims=True))
        alpha = jnp.exp(m - m_new)
        p = jnp.exp(s - m_new)
        l = alpha * l + jnp.sum(p, axis=-1, keepdims=True)
        acc = alpha * acc + _dot(p.astype(BF16), vals)
        return m_new, l, acc

    start(0, 0)

    def body(c, carry):
        slot = c & 1

        @pl.when(c + 1 < nch)
        def _():
            start(c + 1, 1 - slot)

        wait(slot)
        out = []
        for u in range(DEC_SEQS):
            ck = ckbuf[slot, u].astype(BF16)
            kr_t = krbuf[slot, u].astype(BF16)
            s = (_dot_nt(qls[u], ck) + _dot(qrs[u], kr_t)) * SM_SCALE
            out.append(update(carry[u], s, ck))
        return tuple(out)

    init = tuple((jnp.full((nq, 1), NEG_INF, F32), jnp.zeros((nq, 1), F32), jnp.zeros((nq, KV_LORA), F32))
                 for _ in range(DEC_SEQS))
    carry = lax.fori_loop(0, nch, body, init)
    for u in range(DEC_SEQS):
        ckn = ckn_ref[u].astype(BF16)
        krn = krn_ref[u].astype(BF16)
        s = (_dot_nt(qls[u], ckn) + _dot_nt(qrs[u], krn)) * SM_SCALE
        row_t = lax.broadcasted_iota(jnp.int32, s.shape, 0) & (ds - 1)
        col = lax.broadcasted_iota(jnp.int32, s.shape, 1)
        s = jnp.where(col <= row_t, s, NEG_INF)
        _, l, acc = update(carry[u], s, ckn)
        o_ref[u] = (acc / l).astype(BF16)


def _decode(cfg, page_table, ql, qr, ckn, krn, cache_ckv, cache_krope_t):
    db, ds = cfg["DB"], cfg["DS"]
    n_pages = page_table.shape[1]
    page = cache_ckv.shape[1]
    npg = math.gcd(n_pages, 16)
    nq = ql.shape[1]
    assert db % DEC_SEQS == 0
    grid_spec = pltpu.PrefetchScalarGridSpec(
        num_scalar_prefetch=1,
        grid=(db // DEC_SEQS,),
        in_specs=[pl.BlockSpec((DEC_SEQS, nq, KV_LORA), lambda b, pt: (b, 0, 0)),
                  pl.BlockSpec((DEC_SEQS, nq, QK_ROPE), lambda b, pt: (b, 0, 0)),
                  pl.BlockSpec((DEC_SEQS, NEW_PAD, KV_LORA), lambda b, pt: (b, 0, 0)),
                  pl.BlockSpec((DEC_SEQS, NEW_PAD, QK_ROPE), lambda b, pt: (b, 0, 0)),
                  pl.BlockSpec(memory_space=pl.ANY),
                  pl.BlockSpec(memory_space=pl.ANY)],
        out_specs=pl.BlockSpec((DEC_SEQS, nq, KV_LORA), lambda b, pt: (b, 0, 0)),
        scratch_shapes=[pltpu.VMEM((2, DEC_SEQS, npg * page, KV_LORA), F32),
                        pltpu.VMEM((2, DEC_SEQS, QK_ROPE, npg * page), F32),
                        pltpu.SemaphoreType.DMA((2, 2))],
    )
    return pl.pallas_call(
        functools.partial(_decode_kernel, n_pages=n_pages, npg=npg, page=page, ds=ds),
        grid_spec=grid_spec,
        out_shape=jax.ShapeDtypeStruct((db, nq, KV_LORA), BF16),
        compiler_params=_cparams(("arbitrary",)),
        name="mla_decode",
    )(page_table, ql, qr, ckn, krn, cache_ckv, cache_krope_t)


def _router_kernel(x_ref, g_ref, shp, shs, scp, scs, wr_ref, br_ref, xn_ref, idx_ref, gate_ref, cnt_ref,
                   *, npt):
    i = pl.program_id(0)
    xn = _norm_mod(i, npt, x_ref, g_ref, shp, shs, scp, scs)
    for c in range(ROW_TILE):
        xn_ref[pl.ds(c, TM, stride=ROW_TILE), :] = xn[:, c * LANES:(c + 1) * LANES]
    logits = jnp.dot(xn, wr_ref[...], preferred_element_type=F32,
                     precision=lax.Precision.HIGHEST) + br_ref[...]
    lane_e = lax.broadcasted_iota(jnp.int32, logits.shape, 1).astype(F32)
    lane_o = lax.broadcasted_iota(jnp.int32, (logits.shape[0], LANES), 1)
    idx_out = jnp.zeros((logits.shape[0], LANES), F32)
    val_out = jnp.zeros((logits.shape[0], LANES), F32)
    vals = []
    member = jnp.zeros(logits.shape, F32)
    for kk in range(TOP_K):
        m = jnp.max(logits, axis=-1, keepdims=True)
        sel = jnp.min(jnp.where(logits == m, lane_e, float(N_EXPERTS)), axis=-1, keepdims=True)
        chosen = lane_e == sel
        member = jnp.where(chosen, 1.0, member)
        logits = jnp.where(chosen, -jnp.inf, logits)
        idx_out = jnp.where(lane_o == kk, sel, idx_out)
        vals.append(m)

    @pl.when(i == 0)
    def _():
        cnt_ref[...] = jnp.zeros_like(cnt_ref)

    cnt_ref[...] += jnp.sum(member, axis=0, keepdims=True)
    es = [jnp.exp(v - vals[0]) for v in vals]
    den = es[0] + es[1] + es[2] + es[3]
    for kk in range(TOP_K):
        val_out = jnp.where(lane_o == kk, es[kk] / den, val_out)
    idx_ref[...] = idx_out.astype(jnp.int32)
    gate_ref[...] = val_out


def _router(cfg, h, g, mods, w_r, b_r):
    t, npt, seq_tiles = cfg["T"], cfg["NPT"], cfg["SEQ_TILES"]
    (shp, shs), (scp, scs) = mods
    sh_specs, sh_args = _mod_specs(shp, shs, 3, npt, seq_tiles)
    sc_specs, sc_args = _mod_specs(scp, scs, 4, npt, seq_tiles)
    return pl.pallas_call(
        functools.partial(_router_kernel, npt=npt),
        grid=(t // TM,),
        in_specs=[pl.BlockSpec((TM, D_MODEL), lambda i: (i, 0)),
                  pl.BlockSpec((1, D_MODEL), lambda i: (0, 0)),
                  *sh_specs, *sc_specs,
                  pl.BlockSpec((D_MODEL, N_EXPERTS), lambda i: (0, 0)),
                  pl.BlockSpec((1, N_EXPERTS), lambda i: (0, 0))],
        out_specs=[pl.BlockSpec((TM * ROW_TILE, LANES), lambda i: (i, 0)),
                   pl.BlockSpec((TM, LANES), lambda i: (i, 0)),
                   pl.BlockSpec((TM, LANES), lambda i: (i, 0)),
                   pl.BlockSpec((1, N_EXPERTS), lambda i: (0, 0))],
        out_shape=[jax.ShapeDtypeStruct((t * ROW_TILE, LANES), F32),
                   jax.ShapeDtypeStruct((t, LANES), jnp.int32),
                   jax.ShapeDtypeStruct((t, LANES), F32),
                   jax.ShapeDtypeStruct((1, N_EXPERTS), F32)],
        compiler_params=_cparams(("arbitrary",)),
        name="moe_router",
    )(h, g.reshape(1, D_MODEL), *sh_args, *sc_args, w_r, b_r.reshape(1, N_EXPERTS))


IDX_ALIGN = 128
IDX_WIN = MOE_ROWS + IDX_ALIGN
IDX_SLOT = 512
IDX_SLOTS = 4


def _expert_kernel(be_ref, bo_ref, bv_ref, tok_hbm, dst_hbm, x_hbm, wgu_ref, bgu_ref, wdn_ref, bdn_ref,
                   y_hbm, tok_smem, dst_smem, xbuf, ybuf, isem, gsem, ssem, *, n_blocks, n_slots):
    i = pl.program_id(0)
    slot = i & 1
    last = n_blocks - 1
    blk_rows = MOE_ROWS * ROW_TILE
    spill0 = n_slots * ROW_TILE

    def idx_copies(blk):
        s4 = blk & (IDX_SLOTS - 1)
        off = bo_ref[jnp.minimum(blk, last)]
        src = pl.ds(pl.multiple_of(off & -IDX_ALIGN, IDX_ALIGN), IDX_WIN)
        dst = pl.ds(pl.multiple_of(s4 * IDX_SLOT, IDX_ALIGN), IDX_WIN)
        return (pltpu.make_async_copy(tok_hbm.at[src], tok_smem.at[dst], isem.at[0, s4]),
                pltpu.make_async_copy(dst_hbm.at[src], dst_smem.at[dst], isem.at[1, s4]))

    def idx_base(blk):
        return (blk & (IDX_SLOTS - 1)) * IDX_SLOT + (bo_ref[jnp.minimum(blk, last)] & (IDX_ALIGN - 1))

    def issue_gather(blk, buf):
        base = idx_base(blk)
        for r in range(MOE_ROWS):
            row = pl.multiple_of(tok_smem[base + r], ROW_TILE)
            pltpu.make_async_copy(x_hbm.at[pl.ds(row, ROW_TILE)], xbuf.at[buf, pl.ds(r * ROW_TILE, ROW_TILE)],
                                  gsem.at[buf]).start()

    def wait_gather(buf):
        pltpu.make_async_copy(x_hbm.at[pl.ds(0, blk_rows)], xbuf.at[buf], gsem.at[buf]).wait()

    def wait_scatter(buf):
        pltpu.make_async_copy(ybuf.at[buf], y_hbm.at[pl.ds(0, blk_rows)], ssem.at[buf]).wait()

    @pl.when(i == 0)
    def _():
        for c in idx_copies(0):
            c.start()
        for c in idx_copies(0):
            c.wait()
        issue_gather(0, 0)
        for c in idx_copies(1):
            c.start()
        ybuf[...] = jnp.zeros_like(ybuf)
        for b in range(2):
            pltpu.make_async_copy(ybuf.at[b], y_hbm.at[pl.ds(spill0 + b * blk_rows, blk_rows)],
                                  ssem.at[b]).start()

    for c in idx_copies(i + 1):
        c.wait()
    wait_gather(slot)
    wait_scatter(slot)

    issue_gather(i + 1, 1 - slot)
    for c in idx_copies(i + 2):
        c.start()

    xb = jnp.concatenate([xbuf[slot, pl.ds(c, MOE_ROWS, stride=ROW_TILE), :] for c in range(ROW_TILE)],
                         axis=1).astype(BF16)
    hgu = _dot(xb, wgu_ref[0]) + bgu_ref[0]
    gate = jnp.minimum(hgu[:, :D_EXPERT], SWIGLU_LIMIT)
    up = jnp.clip(hgu[:, D_EXPERT:], -SWIGLU_LIMIT, SWIGLU_LIMIT)
    act = (up + 1.0) * gate * jax.nn.sigmoid(SWIGLU_ALPHA * gate)
    y = _dot(act.astype(BF16), wdn_ref[0]) + bdn_ref[0]
    for c in range(ROW_TILE):
        ybuf[slot, pl.ds(c, MOE_ROWS, stride=ROW_TILE), :] = y[:, c * LANES:(c + 1) * LANES]

    n_valid = bv_ref[i]
    base = idx_base(i)
    spill = spill0 + slot * blk_rows
    for r in range(MOE_ROWS):
        row = pl.multiple_of(jnp.where(r < n_valid, dst_smem[base + r], spill + r * ROW_TILE), ROW_TILE)
        pltpu.make_async_copy(ybuf.at[slot, pl.ds(r * ROW_TILE, ROW_TILE)], y_hbm.at[pl.ds(row, ROW_TILE)],
                              ssem.at[slot]).start()

    @pl.when(i == last)
    def _():
        wait_gather(1 - slot)
        for c in idx_copies(i + 2):
            c.wait()
        wait_scatter(1 - slot)
        wait_scatter(slot)


def _experts(cfg, xn_rows, tables, w_gu, b_gu, w_dn, b_dn):
    t = cfg["T"]
    blk_exp, blk_off, blk_valid, tok_rows, dst_rows = tables
    n_blocks = blk_exp.shape[0]
    n_slots = t * TOP_K
    any_spec = pl.BlockSpec(memory_space=pl.ANY)
    grid_spec = pltpu.PrefetchScalarGridSpec(
        num_scalar_prefetch=3,
        grid=(n_blocks,),
        in_specs=[any_spec, any_spec, any_spec,
                  pl.BlockSpec((1, D_MODEL, 2 * D_EXPERT), lambda i, be, bo, bv: (be[i], 0, 0)),
                  pl.BlockSpec((1, 1, 2 * D_EXPERT), lambda i, be, bo, bv: (be[i], 0, 0)),
                  pl.BlockSpec((1, D_EXPERT, D_MODEL), lambda i, be, bo, bv: (be[i], 0, 0)),
                  pl.BlockSpec((1, 1, D_MODEL), lambda i, be, bo, bv: (be[i], 0, 0))],
        out_specs=any_spec,
        scratch_shapes=[pltpu.SMEM((IDX_SLOTS * IDX_SLOT,), jnp.int32),
                        pltpu.SMEM((IDX_SLOTS * IDX_SLOT,), jnp.int32),
                        pltpu.VMEM((2, MOE_ROWS * ROW_TILE, LANES), F32),
                        pltpu.VMEM((2, MOE_ROWS * ROW_TILE, LANES), F32),
                        pltpu.SemaphoreType.DMA((2, IDX_SLOTS)),
                        pltpu.SemaphoreType.DMA((2,)),
                        pltpu.SemaphoreType.DMA((2,))],
    )
    return pl.pallas_call(
        functools.partial(_expert_kernel, n_blocks=n_blocks, n_slots=n_slots),
        grid_spec=grid_spec,
        out_shape=jax.ShapeDtypeStruct(((n_slots + 2 * MOE_ROWS) * ROW_TILE, LANES), F32),
        compiler_params=_cparams(("arbitrary",)),
        name="moe_experts",
    )(blk_exp, blk_off, blk_valid, tok_rows, dst_rows, xn_rows, w_gu,
      b_gu.reshape(N_EXPERTS, 1, 2 * D_EXPERT), w_dn, b_dn.reshape(N_EXPERTS, 1, D_MODEL))


def _combine_kernel(h_ref, y0, y1, y2, y3, gates_ref, gp, gs, o_ref, *, npt):
    i = pl.program_id(0)
    gates = gates_ref[...]
    gmod = _pick(i < npt, gp, gs)
    for c in range(ROW_TILE):
        sl = slice(c * LANES, (c + 1) * LANES)
        rows = pl.ds(c, TM, stride=ROW_TILE)
        y = gates[:, 0:1] * y0[rows, :]
        for kk, y_ref in ((1, y1), (2, y2), (3, y3)):
            y = y + gates[:, kk:kk + 1] * y_ref[rows, :]
        o_ref[:, sl] = h_ref[:, sl] + gmod[:, sl] * y


def _combine(cfg, h, y_rows, gates, gate_mods):
    t, npt, seq_tiles = cfg["T"], cfg["NPT"], cfg["SEQ_TILES"]
    g_specs, g_args = _mod_specs(*gate_mods, 5, npt, seq_tiles)
    nt = t // TM
    y_specs = [pl.BlockSpec((TM * ROW_TILE, LANES), functools.partial(lambda i, kk: (kk * nt + i, 0), kk=kk))
               for kk in range(TOP_K)]
    return pl.pallas_call(
        functools.partial(_combine_kernel, npt=npt),
        grid=(nt,),
        in_specs=[pl.BlockSpec((TM, D_MODEL), lambda i: (i, 0)),
                  *y_specs,
                  pl.BlockSpec((TM, LANES), lambda i: (i, 0)),
                  *g_specs],
        out_specs=pl.BlockSpec((TM, D_MODEL), lambda i: (i, 0)),
        out_shape=jax.ShapeDtypeStruct((t, D_MODEL), F32),
        compiler_params=_cparams(("arbitrary",)),
        name="moe_combine",
    )(h, y_rows, y_rows, y_rows, y_rows, gates, *g_args)


def _dispatch_tables(cfg, top_i, counts):
    t = cfg["T"]
    n_asg = t * TOP_K
    bits = (n_asg - 1).bit_length()
    flat = jnp.arange(n_asg, dtype=jnp.int32).reshape(t, TOP_K)
    keys = jnp.sort(((top_i << bits) | flat).reshape(-1))
    asg = keys & ((1 << bits) - 1)
    tok = asg // TOP_K
    pad = jnp.zeros((2 * MOE_ROWS,), jnp.int32)
    tok_rows = jnp.concatenate([tok * ROW_TILE, pad])
    dst_rows = jnp.concatenate([((asg % TOP_K) * t + tok) * ROW_TILE, pad])
    n_blocks = -(-n_asg // MOE_ROWS) + N_EXPERTS
    nblk = (counts + MOE_ROWS - 1) // MOE_ROWS
    blk_end = jnp.cumsum(nblk)
    grp_start = jnp.cumsum(counts) - counts
    b = jnp.arange(n_blocks, dtype=jnp.int32)
    blk_exp = jnp.minimum(jnp.searchsorted(blk_end, b, side="right"), N_EXPERTS - 1).astype(jnp.int32)
    within = (b - (blk_end - nblk)[blk_exp]) * MOE_ROWS
    blk_off = jnp.minimum(grp_start[blk_exp] + within, n_asg).astype(jnp.int32)
    blk_valid = jnp.clip(counts[blk_exp] - within, 0, MOE_ROWS).astype(jnp.int32)
    return blk_exp, blk_off, blk_valid, tok_rows, dst_rows


def _moe(cfg, h, g, mods, gate_mods, w_r, b_r, w_gu, b_gu, w_dn, b_dn):
    xn, idx, gates, counts = _router(cfg, h, g, mods, w_r, b_r)
    tables = _dispatch_tables(cfg, idx[:, :TOP_K], counts[0].astype(jnp.int32))
    yk = _experts(cfg, xn, tables, w_gu, b_gu, w_dn, b_dn)
    return _combine(cfg, h, yk, gates, gate_mods)


def _final_kernel(x_ref, g_ref, shp, shs, scp, scs, o_ref, *, npt):
    i = pl.program_id(0)
    o_ref[...] = _norm_mod(i, npt, x_ref, g_ref, shp, shs, scp, scs)


def _final(cfg, h, g, mods):
    t, npt, seq_tiles = cfg["T"], cfg["NPT"], cfg["SEQ_TILES"]
    (shp, shs), (scp, scs) = mods
    sh_specs, sh_args = _mod_specs(shp, shs, 0, npt, seq_tiles)
    sc_specs, sc_args = _mod_specs(scp, scs, 1, npt, seq_tiles)
    return pl.pallas_call(
        functools.partial(_final_kernel, npt=npt),
        grid=(t // TM,),
        in_specs=[pl.BlockSpec((TM, D_MODEL), lambda i: (i, 0)),
                  pl.BlockSpec((1, D_MODEL), lambda i: (0, 0)),
                  *sh_specs, *sc_specs],
        out_specs=pl.BlockSpec((TM, D_MODEL), lambda i: (i, 0)),
        out_shape=jax.ShapeDtypeStruct((t, D_MODEL), F32),
        compiler_params=_cparams(("arbitrary",)),
        name="final_norm",
    )(h, g.reshape(1, D_MODEL), *sh_args, *sc_args)


def _rope_tables(pos, group):
    half = group // 2
    inv = ROPE_BASE ** (-jnp.arange(half, dtype=F32) / half)
    ang = pos.astype(F32)[:, None] * inv[None, :]
    cos = jnp.cos(ang)
    sin = jnp.sin(ang)
    reps = LANES // group
    cos_t = jnp.tile(jnp.concatenate([cos, cos], axis=-1), (1, reps))
    sin_t = jnp.tile(jnp.concatenate([-sin, sin], axis=-1), (1, reps))
    return cos_t, sin_t


def _split_mods(cfg, ada):
    b, ds = cfg["B"], cfg["DS"]
    return ada[:b].reshape(b, 1, ada.shape[1]), jnp.repeat(ada[b:], ds, axis=0)


def kernel(x_prompt, x_sample, c_prompt, c_sample, state_ret, cache_ckv, cache_krope, page_table,
           w_ada, b_ada, g_mix, g_ffn, ret_w_in, ret_gn, ret_w_o,
           w_ada_kv, b_ada_kv, g_kv_in, mla_w_kv_a, g_ckv, mla_w_uk, mla_w_uv,
           mla_w_dq, g_cq, mla_w_uq, mla_w_o,
           w_router, b_router, w_gu, b_gu, w_dn, b_dn,
           w_ada_f, b_ada_f, g_final):
    b, s, d = x_prompt.shape
    db, ds, _ = x_sample.shape
    assert d == D_MODEL and db * ds == TM and s % TM == 0 and s % ATT_T == 0
    npr = b * s
    cfg = dict(B=b, S=s, DB=db, DS=ds, NP=npr, T=npr + TM, NPT=npr // TM, SEQ_TILES=s // TM)
    past_len = page_table.shape[1] * cache_ckv.shape[1]

    h = jnp.concatenate([x_prompt.reshape(npr, d), x_sample.reshape(TM, d)], axis=0)
    c_all = jnp.concatenate([c_prompt, c_sample], axis=0)

    pos = jnp.concatenate([jnp.arange(s, dtype=jnp.int32),
                           jnp.tile(past_len + jnp.arange(ds, dtype=jnp.int32), db)])
    cos_r, sin_r = _rope_tables(pos, RET_DK)
    k_scale = RET_DK ** -0.5
    ret_cos = jnp.stack([cos_r, cos_r * k_scale])
    ret_sin = jnp.stack([sin_r, sin_r * k_scale])
    mla_cos, mla_sin = _rope_tables(pos, QK_ROPE)

    mods_kv = _split_mods(cfg, _adaln(c_all, w_ada_kv, b_ada_kv))
    mods_f = _split_mods(cfg, _adaln(c_all, w_ada_f, b_ada_f))

    n_a = state_ret.shape[0]
    depth = w_ada.shape[0]
    ret_p, ret_s = [], []
    outs_kv = None
    for l in range(depth):
        mods = _split_mods(cfg, _adaln(c_all, w_ada, b_ada, layer=l))
        mm = (mods, mods)
        if l < n_a:
            z = _ret_inproj(cfg, h, g_mix[l], mm, ret_w_in[l].astype(BF16), ret_cos, ret_sin)
            a_p, st_p = _ret_prompt(cfg, z, ret_gn[l])
            a_s, st_s = _ret_sample(cfg, z, ret_gn[l], state_ret[l])
            ret_p.append(st_p)
            ret_s.append(st_s)
            h = _mm_res(cfg, a_p, a_s, ret_w_o[l].astype(BF16), h, mods, 2)
        else:
            bl = l - n_a
            if bl == 0:
                wa_pad = jnp.pad(mla_w_kv_a, ((0, 0), (0, KV_A_PAD - mla_w_kv_a.shape[1]))).astype(BF16)
                w_ukv = jnp.concatenate([mla_w_uk.reshape(KV_LORA, -1), mla_w_uv.reshape(KV_LORA, -1)],
                                        axis=1).astype(BF16)
                ckv, kr, k_cat, v_all = _mla_kv(cfg, h, g_kv_in, (mods_kv, mods_kv), wa_pad, g_ckv,
                                                mla_cos, mla_sin, w_ukv)
                outs_kv = (ckv, kr)
                ckn = jnp.pad(ckv[npr:].reshape(db, ds, KV_LORA), ((0, 0), (0, NEW_PAD - ds), (0, 0)))
                krn = jnp.pad(kr[npr:].reshape(db, ds, QK_ROPE), ((0, 0), (0, NEW_PAD - ds), (0, 0)))
                w_uk_t = jnp.transpose(mla_w_uk, (1, 2, 0)).astype(BF16)
                w_uv_h = jnp.transpose(mla_w_uv, (1, 0, 2)).astype(BF16)
            cq = _mla_dq(cfg, h, g_mix[l], mm, mla_w_dq[bl].astype(BF16), g_cq[bl])
            w_uq = mla_w_uq[bl].reshape(Q_LORA, MLA_HEADS, QK_HEAD)
            w_uq_perm = jnp.concatenate([w_uq[:, :, :QK_NOPE].reshape(Q_LORA, -1),
                                         w_uq[:, :, QK_NOPE:].reshape(Q_LORA, -1)], axis=1).astype(BF16)
            q_cat = _mla_uq(cfg, cq, w_uq_perm, mla_cos, mla_sin)
            o_p = _flash(cfg, q_cat, k_cat, v_all)
            q_s = q_cat[:, npr:, :]
            ql = _head_bmm(q_s[:, :, :QK_NOPE], w_uk_t)
            to_seq = lambda x: x.reshape(MLA_HEADS, db, ds, x.shape[-1]).transpose(1, 0, 2, 3).reshape(
                db, MLA_HEADS * ds, x.shape[-1])
            ctx = _decode(cfg, page_table, to_seq(ql), to_seq(q_s[:, :, QK_NOPE:]), ckn, krn,
                          cache_ckv, jnp.swapaxes(cache_krope, 1, 2))
            ctx_h = ctx.reshape(db, MLA_HEADS, ds, KV_LORA).transpose(1, 0, 2, 3).reshape(MLA_HEADS, TM, KV_LORA)
            o_s = _head_bmm(ctx_h, w_uv_h).transpose(1, 0, 2).reshape(TM, MLA_HEADS * V_HEAD)
            h = _mm_res(cfg, o_p, o_s, mla_w_o[bl].astype(BF16), h, mods, 2)
        h = _moe(cfg, h, g_ffn[l], mm, mods, w_router[l], b_router[l],
                 w_gu[l].astype(BF16), b_gu[l], w_dn[l].astype(BF16), b_dn[l])

    y = _final(cfg, h, g_final, (mods_f, mods_f))
    ckv, kr = outs_kv
    return (y[:npr].reshape(b, s, d), y[npr:].reshape(db, ds, d),
            jnp.stack(ret_p, axis=0), jnp.stack(ret_s, axis=0),
            ckv[:npr].reshape(b, s, KV_LORA), kr[:npr].reshape(b, s, QK_ROPE),
            ckv[npr:].reshape(db, ds, KV_LORA), kr[npr:].reshape(db, ds, QK_ROPE))
```

```python
import functools
import math

import jax
import jax.numpy as jnp
from jax import lax
from jax.experimental import pallas as pl
from jax.experimental.pallas import tpu as pltpu

F32 = jnp.float32
BF16 = jnp.bfloat16

D_MODEL = 1024
RET_HEADS = 8
RET_DK = 128
RET_DV = 256
RET_CHUNK = 128
MLA_HEADS = 8
QK_NOPE = 128
QK_ROPE = 64
QK_HEAD = QK_NOPE + QK_ROPE
V_HEAD = 128
KV_LORA = 256
Q_LORA = 768
SM_SCALE = QK_HEAD ** -0.5
N_EXPERTS = 32
TOP_K = 4
D_EXPERT = 1024
SWIGLU_LIMIT = 7.0
SWIGLU_ALPHA = 1.702
ROPE_BASE = 10000.0
EPS = 1e-6
NEG_INF = -1e30

LANES = 128
ROW_TILE = D_MODEL // LANES
TM = 512
MOE_ROWS = 256
ATT_T = 512
ATT_TK = 512
LOG2_E = math.log2(math.e)
VMEM_LIMIT = 56 * 1024 * 1024

LOG_DECAY = tuple(math.log1p(-2.0 ** (-5.0 - h)) for h in range(RET_HEADS))


def _cparams(sem):
    return pltpu.CompilerParams(dimension_semantics=sem, vmem_limit_bytes=VMEM_LIMIT)


def _rms(x, g):
    return x * lax.rsqrt(jnp.mean(x * x, axis=-1, keepdims=True) + EPS) * g


def _dot(a, b):
    return jnp.dot(a, b, preferred_element_type=F32)


def _dot_nt(a, b):
    return lax.dot_general(a, b, (((1,), (1,)), ((), ())), preferred_element_type=F32)


def _dot_tn(a, b):
    return lax.dot_general(a, b, (((0,), (0,)), ((), ())), preferred_element_type=F32)


def _rope_group(x, cos, sin_signed, group):
    half = group // 2
    lane = lax.broadcasted_iota(jnp.int32, x.shape, 1)
    x_up = pltpu.roll(x, LANES - half, 1)
    x_dn = pltpu.roll(x, half, 1)
    rot = jnp.where((lane & (group - 1)) < half, x_up, x_dn)
    return x * cos + rot * sin_signed


def _pick(is_prompt, p_ref, s_ref):
    return jnp.where(is_prompt, p_ref[0], s_ref[...])


def _mod_specs(mod_p, mod_s, chunk, npt, seq_tiles):
    dm = D_MODEL
    sp = pl.BlockSpec((1, 1, dm), lambda i, *_: (jnp.minimum(i, npt - 1) // seq_tiles, 0, chunk))
    ss = pl.BlockSpec((TM, dm), lambda i, *_: (0, chunk))
    return [sp, ss], [mod_p, mod_s]


def _norm_mod(i, npt, x_ref, g_ref, shp, shs, scp, scs):
    is_p = i < npt
    xn = _rms(x_ref[...], g_ref[...])
    return xn * (1.0 + _pick(is_p, scp, scs)) + _pick(is_p, shp, shs)


def _adaln_kernel(c_ref, w_ref, b_ref, o_ref):
    c = c_ref[...]
    a = (c * jax.nn.sigmoid(c)).astype(BF16)
    o_ref[...] = _dot(a, w_ref[...].astype(BF16)) + b_ref[...]


def _adaln(c_all, w, b, layer=None):
    n = w.shape[-1]
    tn = 512
    rows = c_all.shape[0]
    if layer is None:
        w_spec = pl.BlockSpec((D_MODEL, tn), lambda j: (0, j))
        b_spec = pl.BlockSpec((1, tn), lambda j: (0, j))
        b = b.reshape(1, n)
    else:
        w_spec = pl.BlockSpec((None, D_MODEL, tn), lambda j: (layer, 0, j))
        b_spec = pl.BlockSpec((None, 1, tn), lambda j: (layer, 0, j))
        b = b.reshape(b.shape[0], 1, n)
    return pl.pallas_call(
        _adaln_kernel,
        grid=(n // tn,),
        in_specs=[pl.BlockSpec((rows, D_MODEL), lambda j: (0, 0)), w_spec, b_spec],
        out_specs=pl.BlockSpec((rows, tn), lambda j: (0, j)),
        out_shape=jax.ShapeDtypeStruct((rows, n), F32),
        compiler_params=_cparams(("arbitrary",)),
        name="adaln",
    )(c_all, w, b)


def _inproj_kernel(x_ref, g_ref, shp, shs, scp, scs, w_ref, cos_ref, sin_ref, o_ref, xn_scr,
                   *, npt, tn, n_rope):
    i = pl.program_id(0)
    j = pl.program_id(1)

    @pl.when(j == 0)
    def _():
        xn_scr[...] = _norm_mod(i, npt, x_ref, g_ref, shp, shs, scp, scs).astype(BF16)

    acc = _dot(xn_scr[...], w_ref[...])

    @pl.when(j < n_rope)
    def _():
        cos = cos_ref[0]
        sin = sin_ref[0]
        for c in range(tn // LANES):
            sl = slice(c * LANES, (c + 1) * LANES)
            o_ref[:, sl] = _rope_group(acc[:, sl], cos, sin, RET_DK).astype(BF16)

    @pl.when(j >= n_rope)
    def _():
        o_ref[...] = acc.astype(BF16)


def _ret_inproj(cfg, h, g, mods, w, cos_tab, sin_tab):
    t, npt, seq_tiles = cfg["T"], cfg["NPT"], cfg["SEQ_TILES"]
    n = w.shape[1]
    tn = 1024
    n_rope = (2 * RET_HEADS * RET_DK) // tn
    (shp, shs), (scp, scs) = mods
    sh_specs, sh_args = _mod_specs(shp, shs, 0, npt, seq_tiles)
    sc_specs, sc_args = _mod_specs(scp, scs, 1, npt, seq_tiles)
    tab_spec = pl.BlockSpec(
        (1, TM, LANES),
        lambda i, j: (jnp.minimum(j, n_rope - 1), jnp.where(i < npt, i % seq_tiles, seq_tiles), 0))
    return pl.pallas_call(
        functools.partial(_inproj_kernel, npt=npt, tn=tn, n_rope=n_rope),
        grid=(t // TM, n // tn),
        in_specs=[pl.BlockSpec((TM, D_MODEL), lambda i, j: (i, 0)),
                  pl.BlockSpec((1, D_MODEL), lambda i, j: (0, 0)),
                  *sh_specs, *sc_specs,
                  pl.BlockSpec((D_MODEL, tn), lambda i, j: (0, j)),
                  tab_spec, tab_spec],
        out_specs=pl.BlockSpec((TM, tn), lambda i, j: (i, j)),
        out_shape=jax.ShapeDtypeStruct((t, n), BF16),
        scratch_shapes=[pltpu.VMEM((TM, D_MODEL), BF16)],
        compiler_params=_cparams(("arbitrary", "arbitrary")),
        name="ret_inproj",
    )(h, g.reshape(1, D_MODEL), *sh_args, *sc_args, w, cos_tab, sin_tab)


def _group_norm_gate(o, gn_row, g_bf16):
    mu = jnp.mean(o, axis=-1, keepdims=True)
    var = jnp.mean(jnp.square(o - mu), axis=-1, keepdims=True)
    on = (o - mu) * lax.rsqrt(var + EPS) * gn_row
    gg = g_bf16.astype(F32)
    return (gg * jax.nn.sigmoid(gg) * on).astype(BF16)


def _ret_prompt_kernel(q_ref, k_ref, v_ref, g_ref, gn_ref, a_ref, st_ref):
    c = pl.program_id(1)
    ch = RET_CHUNK

    @pl.when(c == 0)
    def _():
        st_ref[...] = jnp.zeros_like(st_ref)

    row = lax.broadcasted_iota(jnp.int32, (ch, ch), 0)
    col = lax.broadcasted_iota(jnp.int32, (ch, ch), 1)
    causal = row >= col
    diff = jnp.where(causal, (row - col).astype(F32), 0.0)
    ridx = lax.broadcasted_iota(jnp.int32, (ch, 1), 0).astype(F32)
    for h in range(RET_HEADS):
        lg = LOG_DECAY[h]
        ks = slice(h * RET_DK, (h + 1) * RET_DK)
        vs = slice(h * RET_DV, (h + 1) * RET_DV)
        q = q_ref[:, ks]
        k = k_ref[:, ks]
        v = v_ref[:, vs]
        st = st_ref[0, h]
        decay = jnp.where(causal, jnp.exp(diff * lg), 0.0)
        s = _dot_nt(q, k) * decay
        o = _dot(s.astype(BF16), v) + jnp.exp((ridx + 1.0) * lg) * _dot(q, st.astype(BF16))
        kd = (k.astype(F32) * jnp.exp((ch - 1.0 - ridx) * lg)).astype(BF16)
        st_ref[0, h] = math.exp(ch * lg) * st + _dot_tn(kd, v)
        a_ref[:, vs] = _group_norm_gate(o, gn_ref[h:h + 1, :], g_ref[:, vs])


def _ret_prompt(cfg, z, gn):
    b, s = cfg["B"], cfg["S"]
    nc = s // RET_CHUNK
    hk = RET_HEADS * RET_DK
    hv = RET_HEADS * RET_DV
    return pl.pallas_call(
        _ret_prompt_kernel,
        grid=(b, nc),
        in_specs=[pl.BlockSpec((RET_CHUNK, hk), lambda bi, c: (bi * nc + c, 0)),
                  pl.BlockSpec((RET_CHUNK, hk), lambda bi, c: (bi * nc + c, 1)),
                  pl.BlockSpec((RET_CHUNK, hv), lambda bi, c: (bi * nc + c, 1)),
                  pl.BlockSpec((RET_CHUNK, hv), lambda bi, c: (bi * nc + c, 2)),
                  pl.BlockSpec((RET_HEADS, RET_DV), lambda bi, c: (0, 0))],
        out_specs=[pl.BlockSpec((RET_CHUNK, hv), lambda bi, c: (bi * nc + c, 0)),
                   pl.BlockSpec((1, RET_HEADS, RET_DK, RET_DV), lambda bi, c: (bi, 0, 0, 0))],
        out_shape=[jax.ShapeDtypeStruct((b * s, hv), BF16),
                   jax.ShapeDtypeStruct((b, RET_HEADS, RET_DK, RET_DV), F32)],
        compiler_params=_cparams(("arbitrary", "arbitrary")),
        name="ret_prompt",
    )(z, z, z, z, gn)


RS_SEQ = 4


def _ret_sample_kernel(q_ref, k_ref, v_ref, g_ref, gn_ref, st_in_ref, a_ref, st_out_ref, *, ds):
    rows = RS_SEQ * ds
    shift = ds.bit_length() - 1
    row = lax.broadcasted_iota(jnp.int32, (rows, rows), 0)
    col = lax.broadcasted_iota(jnp.int32, (rows, rows), 1)
    causal = jnp.where(row >= col, (row >> shift) - (col >> shift), -1) == 0
    diff = jnp.where(causal, (row - col).astype(F32), 0.0)
    r1 = lax.broadcasted_iota(jnp.int32, (rows, 1), 0)
    t_idx = (r1 & (ds - 1)).astype(F32)
    seq_of_row = r1 >> shift
    for h in range(RET_HEADS):
        lg = LOG_DECAY[h]
        ks = slice(h * RET_DK, (h + 1) * RET_DK)
        vs = slice(h * RET_DV, (h + 1) * RET_DV)
        q = q_ref[:, ks]
        k = k_ref[:, ks]
        v = v_ref[:, vs]
        decay = jnp.where(causal, jnp.exp(diff * lg), 0.0)
        s = _dot_nt(q, k) * decay
        o = _dot(s.astype(BF16), v)
        qd = jnp.exp((t_idx + 1.0) * lg)
        kd = k.astype(F32) * jnp.exp((ds - 1.0 - t_idx) * lg)
        for sq in range(RS_SEQ):
            st = st_in_ref[sq, h]
            mine = seq_of_row == sq
            o = o + jnp.where(mine, qd * _dot(q, st.astype(BF16)), 0.0)
            kds = jnp.where(mine, kd, 0.0).astype(BF16)
            st_out_ref[sq, h] = math.exp(ds * lg) * st + _dot_tn(kds, v)
        a_ref[:, vs] = _group_norm_gate(o, gn_ref[h:h + 1, :], g_ref[:, vs])


def _ret_sample(cfg, z, gn, state):
    db, ds, npr = cfg["DB"], cfg["DS"], cfg["NP"]
    rows = RS_SEQ * ds
    assert ds & (ds - 1) == 0 and db % RS_SEQ == 0 and npr % rows == 0
    base = npr // rows
    hk = RET_HEADS * RET_DK
    hv = RET_HEADS * RET_DV
    st_spec = pl.BlockSpec((RS_SEQ, RET_HEADS, RET_DK, RET_DV), lambda i: (i, 0, 0, 0))
    return pl.pallas_call(
        functools.partial(_ret_sample_kernel, ds=ds),
        grid=(db // RS_SEQ,),
        in_specs=[pl.BlockSpec((rows, hk), lambda i: (base + i, 0)),
                  pl.BlockSpec((rows, hk), lambda i: (base + i, 1)),
                  pl.BlockSpec((rows, hv), lambda i: (base + i, 1)),
                  pl.BlockSpec((rows, hv), lambda i: (base + i, 2)),
                  pl.BlockSpec((RET_HEADS, RET_DV), lambda i: (0, 0)),
                  st_spec],
        out_specs=[pl.BlockSpec((rows, hv), lambda i: (i, 0)), st_spec],
        out_shape=[jax.ShapeDtypeStruct((db * ds, hv), BF16),
                   jax.ShapeDtypeStruct((db, RET_HEADS, RET_DK, RET_DV), F32)],
        compiler_params=_cparams(("arbitrary",)),
        name="ret_sample",
    )(z, z, z, z, gn, state)


def _mm_res_kernel(ap_ref, as_ref, w_ref, res_ref, gp, gs, o_ref, *, npt):
    i = pl.program_id(0)
    gate = _pick(i < npt, gp, gs)

    def run(a_ref):
        o_ref[...] = res_ref[...] + gate * _dot(a_ref[...], w_ref[...])

    pl.when(i < npt)(lambda: run(ap_ref))
    pl.when(i >= npt)(lambda: run(as_ref))


def _mm_res(cfg, a_p, a_s, w, res, gate_mods, chunk):
    t, npt, seq_tiles = cfg["T"], cfg["NPT"], cfg["SEQ_TILES"]
    k = w.shape[0]
    g_specs, g_args = _mod_specs(*gate_mods, chunk, npt, seq_tiles)
    return pl.pallas_call(
        functools.partial(_mm_res_kernel, npt=npt),
        grid=(t // TM,),
        in_specs=[pl.BlockSpec((TM, k), lambda i: (jnp.minimum(i, npt - 1), 0)),
                  pl.BlockSpec((TM, k), lambda i: (0, 0)),
                  pl.BlockSpec((k, D_MODEL), lambda i: (0, 0)),
                  pl.BlockSpec((TM, D_MODEL), lambda i: (i, 0)),
                  *g_specs],
        out_specs=pl.BlockSpec((TM, D_MODEL), lambda i: (i, 0)),
        out_shape=jax.ShapeDtypeStruct((t, D_MODEL), F32),
        compiler_params=_cparams(("arbitrary",)),
        name="mm_res",
    )(a_p, a_s, w, res, *g_args)


KV_A_PAD = KV_LORA + LANES


def _kv_kernel(x_ref, g_ref, shp, shs, scp, scs, wa_ref, gck_ref, cos_ref, sin_ref, wukv_ref,
               ckvp_ref, ckvs_ref, krp_ref, krs_ref, kcat_ref, v_ref, *, npt):
    i = pl.program_id(0)
    xn = _norm_mod(i, npt, x_ref, g_ref, shp, shs, scp, scs).astype(BF16)
    z = _dot(xn, wa_ref[...])
    ckv = _rms(z[:, :KV_LORA], gck_ref[...])
    kr_lanes = _rope_group(z[:, KV_LORA:], cos_ref[...], sin_ref[...], QK_ROPE)
    kr = kr_lanes[:, :QK_ROPE]

    @pl.when(i < npt)
    def _():
        ckvp_ref[...] = ckv
        krp_ref[0] = kr_lanes.T[:QK_ROPE, :]

    @pl.when(i >= npt)
    def _():
        ckvs_ref[...] = ckv
        krs_ref[...] = kr

    up = _dot(ckv.astype(BF16), wukv_ref[...])
    kr16 = kr.astype(BF16)
    for h in range(MLA_HEADS):
        kcat_ref[h, :, :QK_NOPE] = up[:, h * QK_NOPE:(h + 1) * QK_NOPE].astype(BF16)
        kcat_ref[h, :, QK_NOPE:] = kr16
    v_ref[...] = up[:, MLA_HEADS * QK_NOPE:].astype(BF16)


def _mla_kv(cfg, h, g, mods, wa_pad, g_ckv, cos_tab, sin_tab, w_ukv):
    t, npt, seq_tiles = cfg["T"], cfg["NPT"], cfg["SEQ_TILES"]
    (shp, shs), (scp, scs) = mods
    sh_specs, sh_args = _mod_specs(shp, shs, 0, npt, seq_tiles)
    sc_specs, sc_args = _mod_specs(scp, scs, 1, npt, seq_tiles)
    tab_spec = pl.BlockSpec((TM, LANES), lambda i: (jnp.where(i < npt, i % seq_tiles, seq_tiles), 0))
    n_up = w_ukv.shape[1]
    return pl.pallas_call(
        functools.partial(_kv_kernel, npt=npt),
        grid=(t // TM,),
        in_specs=[pl.BlockSpec((TM, D_MODEL), lambda i: (i, 0)),
                  pl.BlockSpec((1, D_MODEL), lambda i: (0, 0)),
                  *sh_specs, *sc_specs,
                  pl.BlockSpec((D_MODEL, KV_A_PAD), lambda i: (0, 0)),
                  pl.BlockSpec((1, KV_LORA), lambda i: (0, 0)),
                  tab_spec, tab_spec,
                  pl.BlockSpec((KV_LORA, n_up), lambda i: (0, 0))],
        out_specs=[pl.BlockSpec((TM, KV_LORA), lambda i: (jnp.minimum(i, npt - 1), 0)),
                   pl.BlockSpec((TM, KV_LORA), lambda i: (0, 0)),
                   pl.BlockSpec((1, QK_ROPE, TM),
                                lambda i: (jnp.minimum(i, npt - 1) // seq_tiles, 0, jnp.minimum(i, npt - 1) % seq_tiles)),
                   pl.BlockSpec((TM, QK_ROPE), lambda i: (0, 0)),
                   pl.BlockSpec((MLA_HEADS, TM, QK_HEAD), lambda i: (0, i, 0)),
                   pl.BlockSpec((TM, MLA_HEADS * V_HEAD), lambda i: (i, 0))],
        out_shape=[jax.ShapeDtypeStruct((npt * TM, KV_LORA), F32),
                   jax.ShapeDtypeStruct((TM, KV_LORA), F32),
                   jax.ShapeDtypeStruct((npt // seq_tiles, QK_ROPE, seq_tiles * TM), F32),
                   jax.ShapeDtypeStruct((TM, QK_ROPE), F32),
                   jax.ShapeDtypeStruct((MLA_HEADS, t, QK_HEAD), BF16),
                   jax.ShapeDtypeStruct((t, MLA_HEADS * V_HEAD), BF16)],
        compiler_params=_cparams(("arbitrary",)),
        name="mla_kv",
    )(h, g.reshape(1, D_MODEL), *sh_args, *sc_args, wa_pad, g_ckv.reshape(1, KV_LORA),
      cos_tab, sin_tab, w_ukv)


def _dq_kernel(x_ref, g_ref, shp, shs, scp, scs, w_ref, gcq_ref, o_ref, *, npt):
    i = pl.program_id(0)
    xn = _norm_mod(i, npt, x_ref, g_ref, shp, shs, scp, scs).astype(BF16)
    o_ref[...] = _rms(_dot(xn, w_ref[...]), gcq_ref[...]).astype(BF16)


def _mla_dq(cfg, h, g, mods, w_dq, g_cq):
    t, npt, seq_tiles = cfg["T"], cfg["NPT"], cfg["SEQ_TILES"]
    (shp, shs), (scp, scs) = mods
    sh_specs, sh_args = _mod_specs(shp, shs, 0, npt, seq_tiles)
    sc_specs, sc_args = _mod_specs(scp, scs, 1, npt, seq_tiles)
    return pl.pallas_call(
        functools.partial(_dq_kernel, npt=npt),
        grid=(t // TM,),
        in_specs=[pl.BlockSpec((TM, D_MODEL), lambda i: (i, 0)),
                  pl.BlockSpec((1, D_MODEL), lambda i: (0, 0)),
                  *sh_specs, *sc_specs,
                  pl.BlockSpec((D_MODEL, Q_LORA), lambda i: (0, 0)),
                  pl.BlockSpec((1, Q_LORA), lambda i: (0, 0))],
        out_specs=pl.BlockSpec((TM, Q_LORA), lambda i: (i, 0)),
        out_shape=jax.ShapeDtypeStruct((t, Q_LORA), BF16),
        compiler_params=_cparams(("arbitrary",)),
        name="mla_dq",
    )(h, g.reshape(1, D_MODEL), *sh_args, *sc_args, w_dq, g_cq.reshape(1, Q_LORA))


def _uq_kernel(cq_ref, w_ref, cos_ref, sin_ref, o_ref):
    z = _dot(cq_ref[...], w_ref[...])
    n_nope = MLA_HEADS * QK_NOPE
    for h in range(MLA_HEADS):
        o_ref[h, :, :QK_NOPE] = z[:, h * QK_NOPE:(h + 1) * QK_NOPE].astype(BF16)
    cos = cos_ref[...]
    sin = sin_ref[...]
    heads_per_group = LANES // QK_ROPE
    for c in range(MLA_HEADS // heads_per_group):
        r = _rope_group(z[:, n_nope + c * LANES:n_nope + (c + 1) * LANES], cos, sin, QK_ROPE).astype(BF16)
        for u in range(heads_per_group):
            o_ref[c * heads_per_group + u, :, QK_NOPE:] = r[:, u * QK_ROPE:(u + 1) * QK_ROPE]


def _mla_uq(cfg, cq, w_uq_perm, cos_tab, sin_tab):
    t, npt, seq_tiles = cfg["T"], cfg["NPT"], cfg["SEQ_TILES"]
    n = w_uq_perm.shape[1]
    tab_spec = pl.BlockSpec((TM, LANES), lambda i: (jnp.where(i < npt, i % seq_tiles, seq_tiles), 0))
    return pl.pallas_call(
        _uq_kernel,
        grid=(t // TM,),
        in_specs=[pl.BlockSpec((TM, Q_LORA), lambda i: (i, 0)),
                  pl.BlockSpec((Q_LORA, n), lambda i: (0, 0)),
                  tab_spec, tab_spec],
        out_specs=pl.BlockSpec((MLA_HEADS, TM, QK_HEAD), lambda i: (0, i, 0)),
        out_shape=jax.ShapeDtypeStruct((MLA_HEADS, t, QK_HEAD), BF16),
        compiler_params=_cparams(("arbitrary",)),
        name="mla_uq",
    )(cq, w_uq_perm, cos_tab, sin_tab)


def _flash_kernel(q_ref, k_ref, v_ref, o_ref):
    qi = pl.program_id(2)
    q = q_ref[0]
    tq = q.shape[0]

    per_q = ATT_T // ATT_TK

    def step(ki, carry, diag):
        m, l, acc = carry
        start = pl.multiple_of(ki * ATT_TK, ATT_TK)
        k = k_ref[0, pl.ds(start, ATT_TK), :]
        v = v_ref[pl.ds(start, ATT_TK), :]
        s = _dot_nt(q, k) * (SM_SCALE * LOG2_E)
        if diag is not None:
            row = lax.broadcasted_iota(jnp.int32, s.shape, 0)
            col = lax.broadcasted_iota(jnp.int32, s.shape, 1) + diag * ATT_TK
            s = jnp.where(col <= row, s, NEG_INF)
        m_new = jnp.maximum(m, jnp.max(s, axis=-1, keepdims=True))
        alpha = jnp.exp2(m - m_new)
        p = jnp.exp2(s - m_new)
        l = alpha * l + jnp.sum(p, axis=-1, keepdims=True)
        acc = alpha * acc + _dot(p.astype(BF16), v)
        return m_new, l, acc

    carry = (jnp.full((tq, 1), NEG_INF, F32), jnp.zeros((tq, 1), F32), jnp.zeros((tq, V_HEAD), F32))
    def body(j, c):
        for d in range(per_q):
            c = step(j * per_q + d, c, None)
        return c

    carry = lax.fori_loop(0, qi, body, carry)
    for d in range(per_q):
        carry = step(qi * per_q + d, carry, d)
    _, l, acc = carry
    o_ref[...] = (acc / l).astype(BF16)


def _flash(cfg, q_cat, k_cat, v):
    b, s, npr = cfg["B"], cfg["S"], cfg["NP"]
    nq = s // ATT_T
    return pl.pallas_call(
        _flash_kernel,
        grid=(b, MLA_HEADS, nq),
        in_specs=[pl.BlockSpec((1, ATT_T, QK_HEAD), lambda bi, h, qi: (h, bi * nq + qi, 0)),
                  pl.BlockSpec((1, s, QK_HEAD), lambda bi, h, qi: (h, bi, 0)),
                  pl.BlockSpec((s, V_HEAD), lambda bi, h, qi: (bi, h))],
        out_specs=pl.BlockSpec((ATT_T, V_HEAD), lambda bi, h, qi: (bi * nq + qi, h)),
        out_shape=jax.ShapeDtypeStruct((npr, MLA_HEADS * V_HEAD), BF16),
        compiler_params=_cparams(("arbitrary", "arbitrary", "arbitrary")),
        name="mla_flash",
    )(q_cat, k_cat, v)


def _bmm_kernel(x_ref, w_ref, o_ref):
    o_ref[0] = _dot(x_ref[0], w_ref[0]).astype(o_ref.dtype)


def _head_bmm(x, w):
    hh, m, k = x.shape
    n = w.shape[2]
    return pl.pallas_call(
        _bmm_kernel,
        grid=(hh,),
        in_specs=[pl.BlockSpec((1, m, k), lambda h: (h, 0, 0)),
                  pl.BlockSpec((1, k, n), lambda h: (h, 0, 0))],
        out_specs=pl.BlockSpec((1, m, n), lambda h: (h, 0, 0)),
        out_shape=jax.ShapeDtypeStruct((hh, m, n), BF16),
        compiler_params=_cparams(("arbitrary",)),
        name="head_bmm",
    )(x, w)


NEW_PAD = 16


DEC_SEQS = 2


def _decode_kernel(pt_ref, ql_ref, qr_ref, ckn_ref, krn_ref, ck_hbm, kr_hbm, o_ref,
                   ckbuf, krbuf, sem, *, n_pages, npg, page, ds):
    g = pl.program_id(0)
    nch = n_pages // npg
    nk = npg * page

    def start(c, slot):
        for u in range(DEC_SEQS):
            for p in range(npg):
                pg = pt_ref[g * DEC_SEQS + u, c * npg + p]
                pltpu.make_async_copy(ck_hbm.at[pg], ckbuf.at[slot, u, pl.ds(p * page, page)],
                                      sem.at[0, slot]).start()
                pltpu.make_async_copy(kr_hbm.at[pg], krbuf.at[slot, u, :, pl.ds(p * page, page)],
                                      sem.at[1, slot]).start()

    def wait(slot):
        pltpu.make_async_copy(ckbuf.at[slot], ckbuf.at[slot], sem.at[0, slot]).wait()
        pltpu.make_async_copy(krbuf.at[slot], krbuf.at[slot], sem.at[1, slot]).wait()

    qls = [ql_ref[u] for u in range(DEC_SEQS)]
    qrs = [qr_ref[u] for u in range(DEC_SEQS)]
    nq = qls[0].shape[0]

    def update(carry, s, vals):
        m, l, acc = carry
        m_new = jnp.maximum(m, jnp.max(s, axis=-1, keepdims=True))
        alpha = jnp.exp(m - m_new)
        p = jnp.exp(s - m_new)
        l = alpha * l + jnp.sum(p, axis=-1, keepdims=True)
        acc = alpha * acc + _dot(p.astype(BF16), vals)
        return m_new, l, acc

    start(0, 0)

    def body(c, carry):
        slot = c & 1

        @pl.when(c + 1 < nch)
        def _():
            start(c + 1, 1 - slot)

        wait(slot)
        out = []
        for u in range(DEC_SEQS):
            ck = ckbuf[slot, u].astype(BF16)
            kr_t = krbuf[slot, u].astype(BF16)
            s = (_dot_nt(qls[u], ck) + _dot(qrs[u], kr_t)) * SM_SCALE
            out.append(update(carry[u], s, ck))
        return tuple(out)

    init = tuple((jnp.full((nq, 1), NEG_INF, F32), jnp.zeros((nq, 1), F32), jnp.zeros((nq, KV_LORA), F32))
                 for _ in range(DEC_SEQS))
    carry = lax.fori_loop(0, nch, body, init)
    for u in range(DEC_SEQS):
        ckn = ckn_ref[u].astype(BF16)
        krn = krn_ref[u].astype(BF16)
        s = (_dot_nt(qls[u], ckn) + _dot_nt(qrs[u], krn)) * SM_SCALE
        row_t = lax.broadcasted_iota(jnp.int32, s.shape, 0) & (ds - 1)
        col = lax.broadcasted_iota(jnp.int32, s.shape, 1)
        s = jnp.where(col <= row_t, s, NEG_INF)
        _, l, acc = update(carry[u], s, ckn)
        o_ref[u] = (acc / l).astype(BF16)


def _decode(cfg, page_table, ql, qr, ckn, krn, cache_ckv, cache_krope_t):
    db, ds = cfg["DB"], cfg["DS"]
    n_pages = page_table.shape[1]
    page = cache_ckv.shape[1]
    npg = math.gcd(n_pages, 16)
    nq = ql.shape[1]
    assert db % DEC_SEQS == 0
    grid_spec = pltpu.PrefetchScalarGridSpec(
        num_scalar_prefetch=1,
        grid=(db // DEC_SEQS,),
        in_specs=[pl.BlockSpec((DEC_SEQS, nq, KV_LORA), lambda b, pt: (b, 0, 0)),
                  pl.BlockSpec((DEC_SEQS, nq, QK_ROPE), lambda b, pt: (b, 0, 0)),
                  pl.BlockSpec((DEC_SEQS, NEW_PAD, KV_LORA), lambda b, pt: (b, 0, 0)),
                  pl.BlockSpec((DEC_SEQS, NEW_PAD, QK_ROPE), lambda b, pt: (b, 0, 0)),
                  pl.BlockSpec(memory_space=pl.ANY),
                  pl.BlockSpec(memory_space=pl.ANY)],
        out_specs=pl.BlockSpec((DEC_SEQS, nq, KV_LORA), lambda b, pt: (b, 0, 0)),
        scratch_shapes=[pltpu.VMEM((2, DEC_SEQS, npg * page, KV_LORA), F32),
                        pltpu.VMEM((2, DEC_SEQS, QK_ROPE, npg * page), F32),
                        pltpu.SemaphoreType.DMA((2, 2))],
    )
    return pl.pallas_call(
        functools.partial(_decode_kernel, n_pages=n_pages, npg=npg, page=page, ds=ds),
        grid_spec=grid_spec,
        out_shape=jax.ShapeDtypeStruct((db, nq, KV_LORA), BF16),
        compiler_params=_cparams(("arbitrary",)),
        name="mla_decode",
    )(page_table, ql, qr, ckn, krn, cache_ckv, cache_krope_t)


def _router_kernel(x_ref, g_ref, shp, shs, scp, scs, wr_ref, br_ref, xn_ref, idx_ref, gate_ref, cnt_ref,
                   *, npt):
    i = pl.program_id(0)
    xn = _norm_mod(i, npt, x_ref, g_ref, shp, shs, scp, scs)
    for c in range(ROW_TILE):
        xn_ref[pl.ds(c, TM, stride=ROW_TILE), :] = xn[:, c * LANES:(c + 1) * LANES]
    logits = jnp.dot(xn, wr_ref[...], preferred_element_type=F32,
                     precision=lax.Precision.HIGHEST) + br_ref[...]
    lane_e = lax.broadcasted_iota(jnp.int32, logits.shape, 1).astype(F32)
    lane_o = lax.broadcasted_iota(jnp.int32, (logits.shape[0], LANES), 1)
    idx_out = jnp.zeros((logits.shape[0], LANES), F32)
    val_out = jnp.zeros((logits.shape[0], LANES), F32)
    vals = []
    member = jnp.zeros(logits.shape, F32)
    for kk in range(TOP_K):
        m = jnp.max(logits, axis=-1, keepdims=True)
        sel = jnp.min(jnp.where(logits == m, lane_e, float(N_EXPERTS)), axis=-1, keepdims=True)
        chosen = lane_e == sel
        member = jnp.where(chosen, 1.0, member)
        logits = jnp.where(chosen, -jnp.inf, logits)
        idx_out = jnp.where(lane_o == kk, sel, idx_out)
        vals.append(m)

    @pl.when(i == 0)
    def _():
        cnt_ref[...] = jnp.zeros_like(cnt_ref)

    cnt_ref[...] += jnp.sum(member, axis=0, keepdims=True)
    es = [jnp.exp(v - vals[0]) for v in vals]
    den = es[0] + es[1] + es[2] + es[3]
    for kk in range(TOP_K):
        val_out = jnp.where(lane_o == kk, es[kk] / den, val_out)
    idx_ref[...] = idx_out.astype(jnp.int32)
    gate_ref[...] = val_out


def _router(cfg, h, g, mods, w_r, b_r):
    t, npt, seq_tiles = cfg["T"], cfg["NPT"], cfg["SEQ_TILES"]
    (shp, shs), (scp, scs) = mods
    sh_specs, sh_args = _mod_specs(shp, shs, 3, npt, seq_tiles)
    sc_specs, sc_args = _mod_specs(scp, scs, 4, npt, seq_tiles)
    return pl.pallas_call(
        functools.partial(_router_kernel, npt=npt),
        grid=(t // TM,),
        in_specs=[pl.BlockSpec((TM, D_MODEL), lambda i: (i, 0)),
                  pl.BlockSpec((1, D_MODEL), lambda i: (0, 0)),
                  *sh_specs, *sc_specs,
                  pl.BlockSpec((D_MODEL, N_EXPERTS), lambda i: (0, 0)),
                  pl.BlockSpec((1, N_EXPERTS), lambda i: (0, 0))],
        out_specs=[pl.BlockSpec((TM * ROW_TILE, LANES), lambda i: (i, 0)),
                   pl.BlockSpec((TM, LANES), lambda i: (i, 0)),
                   pl.BlockSpec((TM, LANES), lambda i: (i, 0)),
                   pl.BlockSpec((1, N_EXPERTS), lambda i: (0, 0))],
        out_shape=[jax.ShapeDtypeStruct((t * ROW_TILE, LANES), F32),
                   jax.ShapeDtypeStruct((t, LANES), jnp.int32),
                   jax.ShapeDtypeStruct((t, LANES), F32),
                   jax.ShapeDtypeStruct((1, N_EXPERTS), F32)],
        compiler_params=_cparams(("arbitrary",)),
        name="moe_router",
    )(h, g.reshape(1, D_MODEL), *sh_args, *sc_args, w_r, b_r.reshape(1, N_EXPERTS))


IDX_ALIGN = 128
IDX_WIN = MOE_ROWS + IDX_ALIGN
IDX_SLOT = 512
IDX_SLOTS = 4


def _expert_kernel(be_ref, bo_ref, bv_ref, tok_hbm, dst_hbm, x_hbm, wgu_ref, bgu_ref, wdn_ref, bdn_ref,
                   y_hbm, tok_smem, dst_smem, xbuf, ybuf, wgu_bf, wdn_bf, isem, gsem, ssem,
                   *, n_blocks, n_slots):
    i = pl.program_id(0)
    slot = i & 1
    last = n_blocks - 1
    blk_rows = MOE_ROWS * ROW_TILE
    spill0 = n_slots * ROW_TILE

    def idx_copies(blk):
        s4 = blk & (IDX_SLOTS - 1)
        off = bo_ref[jnp.minimum(blk, last)]
        src = pl.ds(pl.multiple_of(off & -IDX_ALIGN, IDX_ALIGN), IDX_WIN)
        dst = pl.ds(pl.multiple_of(s4 * IDX_SLOT, IDX_ALIGN), IDX_WIN)
        return (pltpu.make_async_copy(tok_hbm.at[src], tok_smem.at[dst], isem.at[0, s4]),
                pltpu.make_async_copy(dst_hbm.at[src], dst_smem.at[dst], isem.at[1, s4]))

    def idx_base(blk):
        return (blk & (IDX_SLOTS - 1)) * IDX_SLOT + (bo_ref[jnp.minimum(blk, last)] & (IDX_ALIGN - 1))

    def issue_gather(blk, buf):
        base = idx_base(blk)
        for r in range(MOE_ROWS):
            row = pl.multiple_of(tok_smem[base + r], ROW_TILE)
            pltpu.make_async_copy(x_hbm.at[pl.ds(row, ROW_TILE)], xbuf.at[buf, pl.ds(r * ROW_TILE, ROW_TILE)],
                                  gsem.at[buf]).start()

    def wait_gather(buf):
        pltpu.make_async_copy(x_hbm.at[pl.ds(0, blk_rows)], xbuf.at[buf], gsem.at[buf]).wait()

    def wait_scatter(buf):
        pltpu.make_async_copy(ybuf.at[buf], y_hbm.at[pl.ds(0, blk_rows)], ssem.at[buf]).wait()

    @pl.when(i == 0)
    def _():
        for c in idx_copies(0):
            c.start()
        for c in idx_copies(0):
            c.wait()
        issue_gather(0, 0)
        for c in idx_copies(1):
            c.start()
        ybuf[...] = jnp.zeros_like(ybuf)
        for b in range(2):
            pltpu.make_async_copy(ybuf.at[b], y_hbm.at[pl.ds(spill0 + b * blk_rows, blk_rows)],
                                  ssem.at[b]).start()

    for c in idx_copies(i + 1):
        c.wait()
    wait_gather(slot)
    wait_scatter(slot)

    issue_gather(i + 1, 1 - slot)
    for c in idx_copies(i + 2):
        c.start()

    @pl.when((i == 0) | (be_ref[i] != be_ref[jnp.maximum(i - 1, 0)]))
    def _():
        wgu_bf[...] = wgu_ref[0].astype(BF16)
        wdn_bf[...] = wdn_ref[0].astype(BF16)

    xb = jnp.concatenate([xbuf[slot, pl.ds(c, MOE_ROWS, stride=ROW_TILE), :] for c in range(ROW_TILE)],
                         axis=1).astype(BF16)
    hgu = _dot(xb, wgu_bf[...]) + bgu_ref[0]
    gate = jnp.minimum(hgu[:, :D_EXPERT], SWIGLU_LIMIT)
    up = jnp.clip(hgu[:, D_EXPERT:], -SWIGLU_LIMIT, SWIGLU_LIMIT)
    act = (up + 1.0) * gate * jax.nn.sigmoid(SWIGLU_ALPHA * gate)
    y = _dot(act.astype(BF16), wdn_bf[...]) + bdn_ref[0]
    for c in range(ROW_TILE):
        ybuf[slot, pl.ds(c, MOE_ROWS, stride=ROW_TILE), :] = y[:, c * LANES:(c + 1) * LANES]

    n_valid = bv_ref[i]
    base = idx_base(i)
    spill = spill0 + slot * blk_rows
    for r in range(MOE_ROWS):
        row = pl.multiple_of(jnp.where(r < n_valid, dst_smem[base + r], spill + r * ROW_TILE), ROW_TILE)
        pltpu.make_async_copy(ybuf.at[slot, pl.ds(r * ROW_TILE, ROW_TILE)], y_hbm.at[pl.ds(row, ROW_TILE)],
                              ssem.at[slot]).start()

    @pl.when(i == last)
    def _():
        wait_gather(1 - slot)
        for c in idx_copies(i + 2):
            c.wait()
        wait_scatter(1 - slot)
        wait_scatter(slot)


def _experts(cfg, xn_rows, tables, layer, w_gu, b_gu, w_dn, b_dn):
    t = cfg["T"]
    blk_exp, blk_off, blk_valid, tok_rows, dst_rows = tables
    n_blocks = blk_exp.shape[0]
    n_slots = t * TOP_K
    any_spec = pl.BlockSpec(memory_space=pl.ANY)
    e0 = layer * N_EXPERTS
    w_gu = w_gu.reshape(-1, D_MODEL, 2 * D_EXPERT)
    w_dn = w_dn.reshape(-1, D_EXPERT, D_MODEL)
    b_gu = b_gu.reshape(-1, 1, 2 * D_EXPERT)
    b_dn = b_dn.reshape(-1, 1, D_MODEL)
    grid_spec = pltpu.PrefetchScalarGridSpec(
        num_scalar_prefetch=3,
        grid=(n_blocks,),
        in_specs=[any_spec, any_spec, any_spec,
                  pl.BlockSpec((1, D_MODEL, 2 * D_EXPERT), lambda i, be, bo, bv: (e0 + be[i], 0, 0)),
                  pl.BlockSpec((1, 1, 2 * D_EXPERT), lambda i, be, bo, bv: (e0 + be[i], 0, 0)),
                  pl.BlockSpec((1, D_EXPERT, D_MODEL), lambda i, be, bo, bv: (e0 + be[i], 0, 0)),
                  pl.BlockSpec((1, 1, D_MODEL), lambda i, be, bo, bv: (e0 + be[i], 0, 0))],
        out_specs=any_spec,
        scratch_shapes=[pltpu.SMEM((IDX_SLOTS * IDX_SLOT,), jnp.int32),
                        pltpu.SMEM((IDX_SLOTS * IDX_SLOT,), jnp.int32),
                        pltpu.VMEM((2, MOE_ROWS * ROW_TILE, LANES), F32),
                        pltpu.VMEM((2, MOE_ROWS * ROW_TILE, LANES), F32),
                        pltpu.VMEM((D_MODEL, 2 * D_EXPERT), BF16),
                        pltpu.VMEM((D_EXPERT, D_MODEL), BF16),
                        pltpu.SemaphoreType.DMA((2, IDX_SLOTS)),
                        pltpu.SemaphoreType.DMA((2,)),
                        pltpu.SemaphoreType.DMA((2,))],
    )
    return pl.pallas_call(
        functools.partial(_expert_kernel, n_blocks=n_blocks, n_slots=n_slots),
        grid_spec=grid_spec,
        out_shape=jax.ShapeDtypeStruct(((n_slots + 2 * MOE_ROWS) * ROW_TILE, LANES), F32),
        compiler_params=_cparams(("arbitrary",)),
        name="moe_experts",
    )(blk_exp, blk_off, blk_valid, tok_rows, dst_rows, xn_rows, w_gu, b_gu, w_dn, b_dn)


def _combine_kernel(h_ref, y0, y1, y2, y3, gates_ref, gp, gs, o_ref, *, npt):
    i = pl.program_id(0)
    gates = gates_ref[...]
    gmod = _pick(i < npt, gp, gs)
    for c in range(ROW_TILE):
        sl = slice(c * LANES, (c + 1) * LANES)
        rows = pl.ds(c, TM, stride=ROW_TILE)
        y = gates[:, 0:1] * y0[rows, :]
        for kk, y_ref in ((1, y1), (2, y2), (3, y3)):
            y = y + gates[:, kk:kk + 1] * y_ref[rows, :]
        o_ref[:, sl] = h_ref[:, sl] + gmod[:, sl] * y


def _combine(cfg, h, y_rows, gates, gate_mods):
    t, npt, seq_tiles = cfg["T"], cfg["NPT"], cfg["SEQ_TILES"]
    g_specs, g_args = _mod_specs(*gate_mods, 5, npt, seq_tiles)
    nt = t // TM
    y_specs = [pl.BlockSpec((TM * ROW_TILE, LANES), functools.partial(lambda i, kk: (kk * nt + i, 0), kk=kk))
               for kk in range(TOP_K)]
    return pl.pallas_call(
        functools.partial(_combine_kernel, npt=npt),
        grid=(nt,),
        in_specs=[pl.BlockSpec((TM, D_MODEL), lambda i: (i, 0)),
                  *y_specs,
                  pl.BlockSpec((TM, LANES), lambda i: (i, 0)),
                  *g_specs],
        out_specs=pl.BlockSpec((TM, D_MODEL), lambda i: (i, 0)),
        out_shape=jax.ShapeDtypeStruct((t, D_MODEL), F32),
        compiler_params=_cparams(("arbitrary",)),
        name="moe_combine",
    )(h, y_rows, y_rows, y_rows, y_rows, gates, *g_args)


def _dispatch_tables(cfg, top_i, counts):
    t = cfg["T"]
    n_asg = t * TOP_K
    bits = (n_asg - 1).bit_length()
    flat = jnp.arange(n_asg, dtype=jnp.int32).reshape(t, TOP_K)
    keys = jnp.sort(((top_i << bits) | flat).reshape(-1))
    asg = keys & ((1 << bits) - 1)
    tok = asg // TOP_K
    pad = jnp.zeros((2 * MOE_ROWS,), jnp.int32)
    tok_rows = jnp.concatenate([tok * ROW_TILE, pad])
    dst_rows = jnp.concatenate([((asg % TOP_K) * t + tok) * ROW_TILE, pad])
    n_blocks = -(-n_asg // MOE_ROWS) + N_EXPERTS
    nblk = (counts + MOE_ROWS - 1) // MOE_ROWS
    blk_end = jnp.cumsum(nblk)
    grp_start = jnp.cumsum(counts) - counts
    b = jnp.arange(n_blocks, dtype=jnp.int32)
    blk_exp = jnp.minimum(jnp.sum((blk_end[None, :] <= b[:, None]).astype(jnp.int32), axis=1), N_EXPERTS - 1)
    within = (b - (blk_end - nblk)[blk_exp]) * MOE_ROWS
    blk_off = jnp.minimum(grp_start[blk_exp] + within, n_asg).astype(jnp.int32)
    blk_valid = jnp.clip(counts[blk_exp] - within, 0, MOE_ROWS).astype(jnp.int32)
    return blk_exp, blk_off, blk_valid, tok_rows, dst_rows


def _moe(cfg, h, g, mods, gate_mods, w_r, b_r, layer, w_gu, b_gu, w_dn, b_dn):
    xn, idx, gates, counts = _router(cfg, h, g, mods, w_r, b_r)
    tables = _dispatch_tables(cfg, idx[:, :TOP_K], counts[0].astype(jnp.int32))
    yk = _experts(cfg, xn, tables, layer, w_gu, b_gu, w_dn, b_dn)
    return _combine(cfg, h, yk, gates, gate_mods)


def _final_kernel(x_ref, g_ref, shp, shs, scp, scs, op_ref, os_ref, *, npt):
    i = pl.program_id(0)
    y = _norm_mod(i, npt, x_ref, g_ref, shp, shs, scp, scs)

    @pl.when(i < npt)
    def _():
        op_ref[...] = y

    @pl.when(i >= npt)
    def _():
        os_ref[...] = y


def _final(cfg, h, g, mods):
    t, npt, seq_tiles = cfg["T"], cfg["NPT"], cfg["SEQ_TILES"]
    (shp, shs), (scp, scs) = mods
    sh_specs, sh_args = _mod_specs(shp, shs, 0, npt, seq_tiles)
    sc_specs, sc_args = _mod_specs(scp, scs, 1, npt, seq_tiles)
    return pl.pallas_call(
        functools.partial(_final_kernel, npt=npt),
        grid=(t // TM,),
        in_specs=[pl.BlockSpec((TM, D_MODEL), lambda i: (i, 0)),
                  pl.BlockSpec((1, D_MODEL), lambda i: (0, 0)),
                  *sh_specs, *sc_specs],
        out_specs=[pl.BlockSpec((TM, D_MODEL), lambda i: (jnp.minimum(i, npt - 1), 0)),
                   pl.BlockSpec((TM, D_MODEL), lambda i: (0, 0))],
        out_shape=[jax.ShapeDtypeStruct((npt * TM, D_MODEL), F32),
                   jax.ShapeDtypeStruct((TM, D_MODEL), F32)],
        compiler_params=_cparams(("arbitrary",)),
        name="final_norm",
    )(h, g.reshape(1, D_MODEL), *sh_args, *sc_args)


def _rope_tables(pos, group):
    half = group // 2
    inv = ROPE_BASE ** (-jnp.arange(half, dtype=F32) / half)
    ang = pos.astype(F32)[:, None] * inv[None, :]
    cos = jnp.cos(ang)
    sin = jnp.sin(ang)
    reps = LANES // group
    cos_t = jnp.tile(jnp.concatenate([cos, cos], axis=-1), (1, reps))
    sin_t = jnp.tile(jnp.concatenate([-sin, sin], axis=-1), (1, reps))
    return cos_t, sin_t


def _split_mods(cfg, ada):
    b, ds = cfg["B"], cfg["DS"]
    return ada[:b].reshape(b, 1, ada.shape[1]), jnp.repeat(ada[b:], ds, axis=0)


def kernel(x_prompt, x_sample, c_prompt, c_sample, state_ret, cache_ckv, cache_krope, page_table,
           w_ada, b_ada, g_mix, g_ffn, ret_w_in, ret_gn, ret_w_o,
           w_ada_kv, b_ada_kv, g_kv_in, mla_w_kv_a, g_ckv, mla_w_uk, mla_w_uv,
           mla_w_dq, g_cq, mla_w_uq, mla_w_o,
           w_router, b_router, w_gu, b_gu, w_dn, b_dn,
           w_ada_f, b_ada_f, g_final):
    b, s, d = x_prompt.shape
    db, ds, _ = x_sample.shape
    assert d == D_MODEL and db * ds == TM and s % TM == 0 and s % ATT_T == 0
    npr = b * s
    cfg = dict(B=b, S=s, DB=db, DS=ds, NP=npr, T=npr + TM, NPT=npr // TM, SEQ_TILES=s // TM)
    past_len = page_table.shape[1] * cache_ckv.shape[1]

    h = jnp.concatenate([x_prompt.reshape(npr, d), x_sample.reshape(TM, d)], axis=0)
    c_all = jnp.concatenate([c_prompt, c_sample], axis=0)

    pos = jnp.concatenate([jnp.arange(s, dtype=jnp.int32),
                           jnp.tile(past_len + jnp.arange(ds, dtype=jnp.int32), db)])
    cos_r, sin_r = _rope_tables(pos, RET_DK)
    k_scale = RET_DK ** -0.5
    ret_cos = jnp.stack([cos_r, cos_r * k_scale])
    ret_sin = jnp.stack([sin_r, sin_r * k_scale])
    mla_cos, mla_sin = _rope_tables(pos, QK_ROPE)

    mods_kv = _split_mods(cfg, _adaln(c_all, w_ada_kv, b_ada_kv))
    mods_f = _split_mods(cfg, _adaln(c_all, w_ada_f, b_ada_f))

    n_a = state_ret.shape[0]
    depth = w_ada.shape[0]
    ret_p, ret_s = [], []
    outs_kv = None
    for l in range(depth):
        mods = _split_mods(cfg, _adaln(c_all, w_ada, b_ada, layer=l))
        mm = (mods, mods)
        if l < n_a:
            z = _ret_inproj(cfg, h, g_mix[l], mm, ret_w_in[l].astype(BF16), ret_cos, ret_sin)
            a_p, st_p = _ret_prompt(cfg, z, ret_gn[l])
            a_s, st_s = _ret_sample(cfg, z, ret_gn[l], state_ret[l])
            ret_p.append(st_p)
            ret_s.append(st_s)
            h = _mm_res(cfg, a_p, a_s, ret_w_o[l].astype(BF16), h, mods, 2)
        else:
            bl = l - n_a
            if bl == 0:
                wa_pad = jnp.pad(mla_w_kv_a, ((0, 0), (0, KV_A_PAD - mla_w_kv_a.shape[1]))).astype(BF16)
                w_ukv = jnp.concatenate([mla_w_uk.reshape(KV_LORA, -1), mla_w_uv.reshape(KV_LORA, -1)],
                                        axis=1).astype(BF16)
                ckv_p, ckv_s, kr_p_t, kr_s, k_cat, v_all = _mla_kv(
                    cfg, h, g_kv_in, (mods_kv, mods_kv), wa_pad, g_ckv, mla_cos, mla_sin, w_ukv)
                outs_kv = (ckv_p, ckv_s, kr_p_t, kr_s)
                ckn = jnp.pad(ckv_s.reshape(db, ds, KV_LORA), ((0, 0), (0, NEW_PAD - ds), (0, 0)))
                krn = jnp.pad(kr_s.reshape(db, ds, QK_ROPE), ((0, 0), (0, NEW_PAD - ds), (0, 0)))
                w_uk_t = jnp.transpose(mla_w_uk, (1, 2, 0)).astype(BF16)
                w_uv_h = jnp.transpose(mla_w_uv, (1, 0, 2)).astype(BF16)
            cq = _mla_dq(cfg, h, g_mix[l], mm, mla_w_dq[bl].astype(BF16), g_cq[bl])
            w_uq = mla_w_uq[bl].reshape(Q_LORA, MLA_HEADS, QK_HEAD)
            w_uq_perm = jnp.concatenate([w_uq[:, :, :QK_NOPE].reshape(Q_LORA, -1),
                                         w_uq[:, :, QK_NOPE:].reshape(Q_LORA, -1)], axis=1).astype(BF16)
            q_cat = _mla_uq(cfg, cq, w_uq_perm, mla_cos, mla_sin)
            o_p = _flash(cfg, q_cat, k_cat, v_all)
            q_s = q_cat[:, npr:, :]
            ql = _head_bmm(q_s[:, :, :QK_NOPE], w_uk_t)
            to_seq = lambda x: x.reshape(MLA_HEADS, db, ds, x.shape[-1]).transpose(1, 0, 2, 3).reshape(
                db, MLA_HEADS * ds, x.shape[-1])
            ctx = _decode(cfg, page_table, to_seq(ql), to_seq(q_s[:, :, QK_NOPE:]), ckn, krn,
                          cache_ckv, jnp.swapaxes(cache_krope, 1, 2))
            ctx_h = ctx.reshape(db, MLA_HEADS, ds, KV_LORA).transpose(1, 0, 2, 3).reshape(MLA_HEADS, TM, KV_LORA)
            o_s = _head_bmm(ctx_h, w_uv_h).transpose(1, 0, 2).reshape(TM, MLA_HEADS * V_HEAD)
            h = _mm_res(cfg, o_p, o_s, mla_w_o[bl].astype(BF16), h, mods, 2)
        h = _moe(cfg, h, g_ffn[l], mm, mods, w_router[l], b_router[l],
                 l, w_gu, b_gu, w_dn, b_dn)

    y_p, y_s = _final(cfg, h, g_final, (mods_f, mods_f))
    ckv_p, ckv_s, kr_p_t, kr_s = outs_kv
    return (y_p.reshape(b, s, d), y_s.reshape(db, ds, d),
            jnp.stack(ret_p, axis=0), jnp.stack(ret_s, axis=0),
            ckv_p.reshape(b, s, KV_LORA), jnp.swapaxes(kr_p_t, 1, 2),
            ckv_s.reshape(db, ds, KV_LORA), kr_s.reshape(db, ds, QK_ROPE))
```

```python
import functools
import math

import jax
import jax.numpy as jnp
from jax import lax
from jax.experimental import pallas as pl
from jax.experimental.pallas import tpu as pltpu

F32 = jnp.float32
BF16 = jnp.bfloat16

D_MODEL = 1024
RET_HEADS = 8
RET_DK = 128
RET_DV = 256
RET_CHUNK = 128
MLA_HEADS = 8
QK_NOPE = 128
QK_ROPE = 64
QK_HEAD = QK_NOPE + QK_ROPE
V_HEAD = 128
KV_LORA = 256
Q_LORA = 768
SM_SCALE = QK_HEAD ** -0.5
N_EXPERTS = 32
TOP_K = 4
D_EXPERT = 1024
SWIGLU_LIMIT = 7.0
SWIGLU_ALPHA = 1.702
ROPE_BASE = 10000.0
EPS = 1e-6
NEG_INF = -1e30

LANES = 128
ROW_TILE = D_MODEL // LANES
TM = 512
MOE_ROWS = 256
ATT_T = 512
LOG2_E = math.log2(math.e)
VMEM_LIMIT = 56 * 1024 * 1024

LOG_DECAY = tuple(math.log1p(-2.0 ** (-5.0 - h)) for h in range(RET_HEADS))


def _cparams(sem):
    return pltpu.CompilerParams(dimension_semantics=sem, vmem_limit_bytes=VMEM_LIMIT)


def _rms(x, g):
    return x * lax.rsqrt(jnp.mean(x * x, axis=-1, keepdims=True) + EPS) * g


def _dot(a, b):
    return jnp.dot(a, b, preferred_element_type=F32)


def _dot_nt(a, b):
    return lax.dot_general(a, b, (((1,), (1,)), ((), ())), preferred_element_type=F32)


def _dot_tn(a, b):
    return lax.dot_general(a, b, (((0,), (0,)), ((), ())), preferred_element_type=F32)


def _rope_group(x, cos, sin_signed, group):
    half = group // 2
    lane = lax.broadcasted_iota(jnp.int32, x.shape, 1)
    x_up = pltpu.roll(x, LANES - half, 1)
    x_dn = pltpu.roll(x, half, 1)
    rot = jnp.where((lane & (group - 1)) < half, x_up, x_dn)
    return x * cos + rot * sin_signed


def _pick(is_prompt, p_ref, s_ref):
    return jnp.where(is_prompt, p_ref[0], s_ref[...])


def _mod_specs(mod_p, mod_s, chunk, npt, seq_tiles):
    dm = D_MODEL
    sp = pl.BlockSpec((1, 1, dm), lambda i, *_: (jnp.minimum(i, npt - 1) // seq_tiles, 0, chunk))
    ss = pl.BlockSpec((TM, dm), lambda i, *_: (0, chunk))
    return [sp, ss], [mod_p, mod_s]


def _norm_mod(i, npt, x_ref, g_ref, shp, shs, scp, scs):
    is_p = i < npt
    xn = _rms(x_ref[...], g_ref[...])
    return xn * (1.0 + _pick(is_p, scp, scs)) + _pick(is_p, shp, shs)


def _adaln_kernel(c_ref, w_ref, b_ref, o_ref):
    c = c_ref[...]
    a = (c * jax.nn.sigmoid(c)).astype(BF16)
    o_ref[...] = _dot(a, w_ref[...].astype(BF16)) + b_ref[...]


def _adaln(c_all, w, b, layer=None):
    n = w.shape[-1]
    tn = 512
    rows = c_all.shape[0]
    if layer is None:
        w_spec = pl.BlockSpec((D_MODEL, tn), lambda j: (0, j))
        b_spec = pl.BlockSpec((1, tn), lambda j: (0, j))
        b = b.reshape(1, n)
    else:
        w_spec = pl.BlockSpec((None, D_MODEL, tn), lambda j: (layer, 0, j))
        b_spec = pl.BlockSpec((None, 1, tn), lambda j: (layer, 0, j))
        b = b.reshape(b.shape[0], 1, n)
    return pl.pallas_call(
        _adaln_kernel,
        grid=(n // tn,),
        in_specs=[pl.BlockSpec((rows, D_MODEL), lambda j: (0, 0)), w_spec, b_spec],
        out_specs=pl.BlockSpec((rows, tn), lambda j: (0, j)),
        out_shape=jax.ShapeDtypeStruct((rows, n), F32),
        compiler_params=_cparams(("arbitrary",)),
        name="adaln",
    )(c_all, w, b)


def _inproj_kernel(x_ref, g_ref, shp, shs, scp, scs, w_ref, cos_ref, sin_ref, o_ref, xn_scr,
                   *, npt, tn, n_rope):
    i = pl.program_id(0)
    j = pl.program_id(1)

    @pl.when(j == 0)
    def _():
        xn_scr[...] = _norm_mod(i, npt, x_ref, g_ref, shp, shs, scp, scs).astype(BF16)

    acc = _dot(xn_scr[...], w_ref[...])

    @pl.when(j < n_rope)
    def _():
        cos = cos_ref[0]
        sin = sin_ref[0]
        for c in range(tn // LANES):
            sl = slice(c * LANES, (c + 1) * LANES)
            o_ref[:, sl] = _rope_group(acc[:, sl], cos, sin, RET_DK).astype(BF16)

    @pl.when(j >= n_rope)
    def _():
        o_ref[...] = acc.astype(BF16)


def _ret_inproj(cfg, h, g, mods, w, cos_tab, sin_tab):
    t, npt, seq_tiles = cfg["T"], cfg["NPT"], cfg["SEQ_TILES"]
    n = w.shape[1]
    tn = 1024
    n_rope = (2 * RET_HEADS * RET_DK) // tn
    (shp, shs), (scp, scs) = mods
    sh_specs, sh_args = _mod_specs(shp, shs, 0, npt, seq_tiles)
    sc_specs, sc_args = _mod_specs(scp, scs, 1, npt, seq_tiles)
    tab_spec = pl.BlockSpec(
        (1, TM, LANES),
        lambda i, j: (jnp.minimum(j, n_rope - 1), jnp.where(i < npt, i % seq_tiles, seq_tiles), 0))
    return pl.pallas_call(
        functools.partial(_inproj_kernel, npt=npt, tn=tn, n_rope=n_rope),
        grid=(t // TM, n // tn),
        in_specs=[pl.BlockSpec((TM, D_MODEL), lambda i, j: (i, 0)),
                  pl.BlockSpec((1, D_MODEL), lambda i, j: (0, 0)),
                  *sh_specs, *sc_specs,
                  pl.BlockSpec((D_MODEL, tn), lambda i, j: (0, j)),
                  tab_spec, tab_spec],
        out_specs=pl.BlockSpec((TM, tn), lambda i, j: (i, j)),
        out_shape=jax.ShapeDtypeStruct((t, n), BF16),
        scratch_shapes=[pltpu.VMEM((TM, D_MODEL), BF16)],
        compiler_params=_cparams(("arbitrary", "arbitrary")),
        name="ret_inproj",
    )(h, g.reshape(1, D_MODEL), *sh_args, *sc_args, w, cos_tab, sin_tab)


def _group_norm_gate(o, gn_row, g_bf16):
    mu = jnp.mean(o, axis=-1, keepdims=True)
    var = jnp.mean(jnp.square(o - mu), axis=-1, keepdims=True)
    on = (o - mu) * lax.rsqrt(var + EPS) * gn_row
    gg = g_bf16.astype(F32)
    return (gg * jax.nn.sigmoid(gg) * on).astype(BF16)


def _ret_prompt_kernel(q_ref, k_ref, v_ref, g_ref, gn_ref, a_ref, st_ref):
    c = pl.program_id(1)
    ch = RET_CHUNK

    @pl.when(c == 0)
    def _():
        st_ref[...] = jnp.zeros_like(st_ref)

    row = lax.broadcasted_iota(jnp.int32, (ch, ch), 0)
    col = lax.broadcasted_iota(jnp.int32, (ch, ch), 1)
    causal = row >= col
    diff = jnp.where(causal, (row - col).astype(F32), 0.0)
    ridx = lax.broadcasted_iota(jnp.int32, (ch, 1), 0).astype(F32)
    for h in range(RET_HEADS):
        lg = LOG_DECAY[h]
        ks = slice(h * RET_DK, (h + 1) * RET_DK)
        vs = slice(h * RET_DV, (h + 1) * RET_DV)
        q = q_ref[:, ks]
        k = k_ref[:, ks]
        v = v_ref[:, vs]
        st = st_ref[0, h]
        decay = jnp.where(causal, jnp.exp(diff * lg), 0.0)
        s = _dot_nt(q, k) * decay
        o = _dot(s.astype(BF16), v) + jnp.exp((ridx + 1.0) * lg) * _dot(q, st.astype(BF16))
        kd = (k.astype(F32) * jnp.exp((ch - 1.0 - ridx) * lg)).astype(BF16)
        st_ref[0, h] = math.exp(ch * lg) * st + _dot_tn(kd, v)
        a_ref[:, vs] = _group_norm_gate(o, gn_ref[h:h + 1, :], g_ref[:, vs])


def _ret_prompt(cfg, z, gn):
    b, s = cfg["B"], cfg["S"]
    nc = s // RET_CHUNK
    hk = RET_HEADS * RET_DK
    hv = RET_HEADS * RET_DV
    return pl.pallas_call(
        _ret_prompt_kernel,
        grid=(b, nc),
        in_specs=[pl.BlockSpec((RET_CHUNK, hk), lambda bi, c: (bi * nc + c, 0)),
                  pl.BlockSpec((RET_CHUNK, hk), lambda bi, c: (bi * nc + c, 1)),
                  pl.BlockSpec((RET_CHUNK, hv), lambda bi, c: (bi * nc + c, 1)),
                  pl.BlockSpec((RET_CHUNK, hv), lambda bi, c: (bi * nc + c, 2)),
                  pl.BlockSpec((RET_HEADS, RET_DV), lambda bi, c: (0, 0))],
        out_specs=[pl.BlockSpec((RET_CHUNK, hv), lambda bi, c: (bi * nc + c, 0)),
                   pl.BlockSpec((1, RET_HEADS, RET_DK, RET_DV), lambda bi, c: (bi, 0, 0, 0))],
        out_shape=[jax.ShapeDtypeStruct((b * s, hv), BF16),
                   jax.ShapeDtypeStruct((b, RET_HEADS, RET_DK, RET_DV), F32)],
        compiler_params=_cparams(("arbitrary", "arbitrary")),
        name="ret_prompt",
    )(z, z, z, z, gn)


RS_SEQ = 4


def _ret_sample_kernel(q_ref, k_ref, v_ref, g_ref, gn_ref, st_in_ref, a_ref, st_out_ref, *, ds):
    rows = RS_SEQ * ds
    shift = ds.bit_length() - 1
    row = lax.broadcasted_iota(jnp.int32, (rows, rows), 0)
    col = lax.broadcasted_iota(jnp.int32, (rows, rows), 1)
    causal = jnp.where(row >= col, (row >> shift) - (col >> shift), -1) == 0
    diff = jnp.where(causal, (row - col).astype(F32), 0.0)
    r1 = lax.broadcasted_iota(jnp.int32, (rows, 1), 0)
    t_idx = (r1 & (ds - 1)).astype(F32)
    seq_of_row = r1 >> shift
    for h in range(RET_HEADS):
        lg = LOG_DECAY[h]
        ks = slice(h * RET_DK, (h + 1) * RET_DK)
        vs = slice(h * RET_DV, (h + 1) * RET_DV)
        q = q_ref[:, ks]
        k = k_ref[:, ks]
        v = v_ref[:, vs]
        decay = jnp.where(causal, jnp.exp(diff * lg), 0.0)
        s = _dot_nt(q, k) * decay
        o = _dot(s.astype(BF16), v)
        qd = jnp.exp((t_idx + 1.0) * lg)
        kd = k.astype(F32) * jnp.exp((ds - 1.0 - t_idx) * lg)
        for sq in range(RS_SEQ):
            st = st_in_ref[sq, h]
            mine = seq_of_row == sq
            o = o + jnp.where(mine, qd * _dot(q, st.astype(BF16)), 0.0)
            kds = jnp.where(mine, kd, 0.0).astype(BF16)
            st_out_ref[sq, h] = math.exp(ds * lg) * st + _dot_tn(kds, v)
        a_ref[:, vs] = _group_norm_gate(o, gn_ref[h:h + 1, :], g_ref[:, vs])


def _ret_sample(cfg, z, gn, state):
    db, ds, npr = cfg["DB"], cfg["DS"], cfg["NP"]
    rows = RS_SEQ * ds
    assert ds & (ds - 1) == 0 and db % RS_SEQ == 0 and npr % rows == 0
    base = npr // rows
    hk = RET_HEADS * RET_DK
    hv = RET_HEADS * RET_DV
    st_spec = pl.BlockSpec((RS_SEQ, RET_HEADS, RET_DK, RET_DV), lambda i: (i, 0, 0, 0))
    return pl.pallas_call(
        functools.partial(_ret_sample_kernel, ds=ds),
        grid=(db // RS_SEQ,),
        in_specs=[pl.BlockSpec((rows, hk), lambda i: (base + i, 0)),
                  pl.BlockSpec((rows, hk), lambda i: (base + i, 1)),
                  pl.BlockSpec((rows, hv), lambda i: (base + i, 1)),
                  pl.BlockSpec((rows, hv), lambda i: (base + i, 2)),
                  pl.BlockSpec((RET_HEADS, RET_DV), lambda i: (0, 0)),
                  st_spec],
        out_specs=[pl.BlockSpec((rows, hv), lambda i: (i, 0)), st_spec],
        out_shape=[jax.ShapeDtypeStruct((db * ds, hv), BF16),
                   jax.ShapeDtypeStruct((db, RET_HEADS, RET_DK, RET_DV), F32)],
        compiler_params=_cparams(("arbitrary",)),
        name="ret_sample",
    )(z, z, z, z, gn, state)


def _mm_res_kernel(ap_ref, as_ref, w_ref, res_ref, gp, gs, o_ref, *, npt):
    i = pl.program_id(0)
    gate = _pick(i < npt, gp, gs)

    def run(a_ref):
        o_ref[...] = res_ref[...] + gate * _dot(a_ref[...], w_ref[...])

    pl.when(i < npt)(lambda: run(ap_ref))
    pl.when(i >= npt)(lambda: run(as_ref))


def _mm_res(cfg, a_p, a_s, w, res, gate_mods, chunk):
    t, npt, seq_tiles = cfg["T"], cfg["NPT"], cfg["SEQ_TILES"]
    k = w.shape[0]
    g_specs, g_args = _mod_specs(*gate_mods, chunk, npt, seq_tiles)
    return pl.pallas_call(
        functools.partial(_mm_res_kernel, npt=npt),
        grid=(t // TM,),
        in_specs=[pl.BlockSpec((TM, k), lambda i: (jnp.minimum(i, npt - 1), 0)),
                  pl.BlockSpec((TM, k), lambda i: (0, 0)),
                  pl.BlockSpec((k, D_MODEL), lambda i: (0, 0)),
                  pl.BlockSpec((TM, D_MODEL), lambda i: (i, 0)),
                  *g_specs],
        out_specs=pl.BlockSpec((TM, D_MODEL), lambda i: (i, 0)),
        out_shape=jax.ShapeDtypeStruct((t, D_MODEL), F32),
        compiler_params=_cparams(("arbitrary",)),
        name="mm_res",
    )(a_p, a_s, w, res, *g_args)


KV_A_PAD = KV_LORA + LANES


def _kv_kernel(x_ref, g_ref, shp, shs, scp, scs, wa_ref, gck_ref, cos_ref, sin_ref, wukv_ref,
               ckvp_ref, ckvs_ref, krp_ref, krs_ref, kcat_ref, v_ref, *, npt):
    i = pl.program_id(0)
    xn = _norm_mod(i, npt, x_ref, g_ref, shp, shs, scp, scs).astype(BF16)
    z = _dot(xn, wa_ref[...])
    ckv = _rms(z[:, :KV_LORA], gck_ref[...])
    kr_lanes = _rope_group(z[:, KV_LORA:], cos_ref[...], sin_ref[...], QK_ROPE)
    kr = kr_lanes[:, :QK_ROPE]

    @pl.when(i < npt)
    def _():
        ckvp_ref[...] = ckv
        krp_ref[0] = kr_lanes.T[:QK_ROPE, :]

    @pl.when(i >= npt)
    def _():
        ckvs_ref[...] = ckv
        krs_ref[...] = kr

    up = _dot(ckv.astype(BF16), wukv_ref[...])
    kr16 = kr.astype(BF16)
    for h in range(MLA_HEADS):
        kcat_ref[h, :, :QK_NOPE] = up[:, h * QK_NOPE:(h + 1) * QK_NOPE].astype(BF16)
        kcat_ref[h, :, QK_NOPE:] = kr16
    v_ref[...] = up[:, MLA_HEADS * QK_NOPE:].astype(BF16)


def _mla_kv(cfg, h, g, mods, wa_pad, g_ckv, cos_tab, sin_tab, w_ukv):
    t, npt, seq_tiles = cfg["T"], cfg["NPT"], cfg["SEQ_TILES"]
    (shp, shs), (scp, scs) = mods
    sh_specs, sh_args = _mod_specs(shp, shs, 0, npt, seq_tiles)
    sc_specs, sc_args = _mod_specs(scp, scs, 1, npt, seq_tiles)
    tab_spec = pl.BlockSpec((TM, LANES), lambda i: (jnp.where(i < npt, i % seq_tiles, seq_tiles), 0))
    n_up = w_ukv.shape[1]
    return pl.pallas_call(
        functools.partial(_kv_kernel, npt=npt),
        grid=(t // TM,),
        in_specs=[pl.BlockSpec((TM, D_MODEL), lambda i: (i, 0)),
                  pl.BlockSpec((1, D_MODEL), lambda i: (0, 0)),
                  *sh_specs, *sc_specs,
                  pl.BlockSpec((D_MODEL, KV_A_PAD), lambda i: (0, 0)),
                  pl.BlockSpec((1, KV_LORA), lambda i: (0, 0)),
                  tab_spec, tab_spec,
                  pl.BlockSpec((KV_LORA, n_up), lambda i: (0, 0))],
        out_specs=[pl.BlockSpec((TM, KV_LORA), lambda i: (jnp.minimum(i, npt - 1), 0)),
                   pl.BlockSpec((TM, KV_LORA), lambda i: (0, 0)),
                   pl.BlockSpec((1, QK_ROPE, TM),
                                lambda i: (jnp.minimum(i, npt - 1) // seq_tiles, 0, jnp.minimum(i, npt - 1) % seq_tiles)),
                   pl.BlockSpec((TM, QK_ROPE), lambda i: (0, 0)),
                   pl.BlockSpec((MLA_HEADS, TM, QK_HEAD), lambda i: (0, i, 0)),
                   pl.BlockSpec((TM, MLA_HEADS * V_HEAD), lambda i: (i, 0))],
        out_shape=[jax.ShapeDtypeStruct((npt * TM, KV_LORA), F32),
                   jax.ShapeDtypeStruct((TM, KV_LORA), F32),
                   jax.ShapeDtypeStruct((npt // seq_tiles, QK_ROPE, seq_tiles * TM), F32),
                   jax.ShapeDtypeStruct((TM, QK_ROPE), F32),
                   jax.ShapeDtypeStruct((MLA_HEADS, t, QK_HEAD), BF16),
                   jax.ShapeDtypeStruct((t, MLA_HEADS * V_HEAD), BF16)],
        compiler_params=_cparams(("arbitrary",)),
        name="mla_kv",
    )(h, g.reshape(1, D_MODEL), *sh_args, *sc_args, wa_pad, g_ckv.reshape(1, KV_LORA),
      cos_tab, sin_tab, w_ukv)


def _dq_kernel(x_ref, g_ref, shp, shs, scp, scs, w_ref, gcq_ref, o_ref, *, npt):
    i = pl.program_id(0)
    xn = _norm_mod(i, npt, x_ref, g_ref, shp, shs, scp, scs).astype(BF16)
    o_ref[...] = _rms(_dot(xn, w_ref[...]), gcq_ref[...]).astype(BF16)


def _mla_dq(cfg, h, g, mods, w_dq, g_cq):
    t, npt, seq_tiles = cfg["T"], cfg["NPT"], cfg["SEQ_TILES"]
    (shp, shs), (scp, scs) = mods
    sh_specs, sh_args = _mod_specs(shp, shs, 0, npt, seq_tiles)
    sc_specs, sc_args = _mod_specs(scp, scs, 1, npt, seq_tiles)
    return pl.pallas_call(
        functools.partial(_dq_kernel, npt=npt),
        grid=(t // TM,),
        in_specs=[pl.BlockSpec((TM, D_MODEL), lambda i: (i, 0)),
                  pl.BlockSpec((1, D_MODEL), lambda i: (0, 0)),
                  *sh_specs, *sc_specs,
                  pl.BlockSpec((D_MODEL, Q_LORA), lambda i: (0, 0)),
                  pl.BlockSpec((1, Q_LORA), lambda i: (0, 0))],
        out_specs=pl.BlockSpec((TM, Q_LORA), lambda i: (i, 0)),
        out_shape=jax.ShapeDtypeStruct((t, Q_LORA), BF16),
        compiler_params=_cparams(("arbitrary",)),
        name="mla_dq",
    )(h, g.reshape(1, D_MODEL), *sh_args, *sc_args, w_dq, g_cq.reshape(1, Q_LORA))


def _uq_kernel(cq_ref, w_ref, cos_ref, sin_ref, o_ref):
    z = _dot(cq_ref[...], w_ref[...])
    n_nope = MLA_HEADS * QK_NOPE
    for h in range(MLA_HEADS):
        o_ref[h, :, :QK_NOPE] = z[:, h * QK_NOPE:(h + 1) * QK_NOPE].astype(BF16)
    cos = cos_ref[...]
    sin = sin_ref[...]
    heads_per_group = LANES // QK_ROPE
    for c in range(MLA_HEADS // heads_per_group):
        r = _rope_group(z[:, n_nope + c * LANES:n_nope + (c + 1) * LANES], cos, sin, QK_ROPE).astype(BF16)
        for u in range(heads_per_group):
            o_ref[c * heads_per_group + u, :, QK_NOPE:] = r[:, u * QK_ROPE:(u + 1) * QK_ROPE]


def _mla_uq(cfg, cq, w_uq_perm, cos_tab, sin_tab):
    t, npt, seq_tiles = cfg["T"], cfg["NPT"], cfg["SEQ_TILES"]
    n = w_uq_perm.shape[1]
    tab_spec = pl.BlockSpec((TM, LANES), lambda i: (jnp.where(i < npt, i % seq_tiles, seq_tiles), 0))
    return pl.pallas_call(
        _uq_kernel,
        grid=(t // TM,),
        in_specs=[pl.BlockSpec((TM, Q_LORA), lambda i: (i, 0)),
                  pl.BlockSpec((Q_LORA, n), lambda i: (0, 0)),
                  tab_spec, tab_spec],
        out_specs=pl.BlockSpec((MLA_HEADS, TM, QK_HEAD), lambda i: (0, i, 0)),
        out_shape=jax.ShapeDtypeStruct((MLA_HEADS, t, QK_HEAD), BF16),
        compiler_params=_cparams(("arbitrary",)),
        name="mla_uq",
    )(cq, w_uq_perm, cos_tab, sin_tab)


def _flash_kernel(q_ref, k_ref, v_ref, o_ref):
    qi = pl.program_id(2)
    q = q_ref[0]
    tq = q.shape[0]

    def step(ki, carry, masked):
        m, l, acc = carry
        start = pl.multiple_of(ki * ATT_T, ATT_T)
        k = k_ref[0, pl.ds(start, ATT_T), :]
        v = v_ref[pl.ds(start, ATT_T), :]
        s = _dot_nt(q, k) * (SM_SCALE * LOG2_E)
        if masked:
            row = lax.broadcasted_iota(jnp.int32, s.shape, 0)
            col = lax.broadcasted_iota(jnp.int32, s.shape, 1)
            s = jnp.where(col <= row, s, NEG_INF)
        m_new = jnp.maximum(m, jnp.max(s, axis=-1, keepdims=True))
        alpha = jnp.exp2(m - m_new)
        p = jnp.exp2(s - m_new)
        l = alpha * l + jnp.sum(p, axis=-1, keepdims=True)
        acc = alpha * acc + _dot(p.astype(BF16), v)
        return m_new, l, acc

    carry = (jnp.full((tq, 1), NEG_INF, F32), jnp.zeros((tq, 1), F32), jnp.zeros((tq, V_HEAD), F32))
    carry = lax.fori_loop(0, qi // 2, lambda j, c: step(2 * j + 1, step(2 * j, c, False), False), carry)
    carry = lax.cond(qi % 2 == 1, lambda c: step(qi - 1, c, False), lambda c: c, carry)
    _, l, acc = step(qi, carry, True)
    o_ref[...] = (acc / l).astype(BF16)


def _flash(cfg, q_cat, k_cat, v):
    b, s, npr = cfg["B"], cfg["S"], cfg["NP"]
    nq = s // ATT_T
    return pl.pallas_call(
        _flash_kernel,
        grid=(b, MLA_HEADS, nq),
        in_specs=[pl.BlockSpec((1, ATT_T, QK_HEAD), lambda bi, h, qi: (h, bi * nq + qi, 0)),
                  pl.BlockSpec((1, s, QK_HEAD), lambda bi, h, qi: (h, bi, 0)),
                  pl.BlockSpec((s, V_HEAD), lambda bi, h, qi: (bi, h))],
        out_specs=pl.BlockSpec((ATT_T, V_HEAD), lambda bi, h, qi: (bi * nq + qi, h)),
        out_shape=jax.ShapeDtypeStruct((npr, MLA_HEADS * V_HEAD), BF16),
        compiler_params=_cparams(("arbitrary", "arbitrary", "arbitrary")),
        name="mla_flash",
    )(q_cat, k_cat, v)


def _bmm_kernel(x_ref, w_ref, o_ref):
    o_ref[0] = _dot(x_ref[0], w_ref[0]).astype(o_ref.dtype)


def _head_bmm(x, w):
    hh, m, k = x.shape
    n = w.shape[2]
    return pl.pallas_call(
        _bmm_kernel,
        grid=(hh,),
        in_specs=[pl.BlockSpec((1, m, k), lambda h: (h, 0, 0)),
                  pl.BlockSpec((1, k, n), lambda h: (h, 0, 0))],
        out_specs=pl.BlockSpec((1, m, n), lambda h: (h, 0, 0)),
        out_shape=jax.ShapeDtypeStruct((hh, m, n), BF16),
        compiler_params=_cparams(("arbitrary",)),
        name="head_bmm",
    )(x, w)


NEW_PAD = 16


DEC_SEQS = 2


def _decode_kernel(pt_ref, ql_ref, qr_ref, ckn_ref, krn_ref, ck_hbm, kr_hbm, o_ref,
                   ckbuf, krbuf, sem, *, n_pages, npg, page, ds, n_steps):
    g = pl.program_id(0)
    nch = n_pages // npg
    nk = npg * page

    def start(step, c, slot):
        for u in range(DEC_SEQS):
            for p in range(npg):
                pg = pt_ref[step * DEC_SEQS + u, c * npg + p]
                pltpu.make_async_copy(ck_hbm.at[pg], ckbuf.at[slot, u, pl.ds(p * page, page)],
                                      sem.at[0, slot]).start()
                pltpu.make_async_copy(kr_hbm.at[pg], krbuf.at[slot, u, :, pl.ds(p * page, page)],
                                      sem.at[1, slot]).start()

    def wait(slot):
        pltpu.make_async_copy(ckbuf.at[slot], ckbuf.at[slot], sem.at[0, slot]).wait()
        pltpu.make_async_copy(krbuf.at[slot], krbuf.at[slot], sem.at[1, slot]).wait()

    qls = [ql_ref[u] for u in range(DEC_SEQS)]
    qrs = [qr_ref[u] for u in range(DEC_SEQS)]
    nq = qls[0].shape[0]

    def update(carry, s, vals):
        m, l, acc = carry
        m_new = jnp.maximum(m, jnp.max(s, axis=-1, keepdims=True))
        alpha = jnp.exp(m - m_new)
        p = jnp.exp(s - m_new)
        l = alpha * l + jnp.sum(p, axis=-1, keepdims=True)
        acc = alpha * acc + _dot(p.astype(BF16), vals)
        return m_new, l, acc

    @pl.when(g == 0)
    def _():
        start(0, 0, 0)

    def body(c, carry):
        slot = (g * nch + c) & 1

        @pl.when(c + 1 < nch)
        def _():
            start(g, c + 1, 1 - slot)

        @pl.when((c + 1 == nch) & (g + 1 < n_steps))
        def _():
            start(g + 1, 0, 1 - slot)

        wait(slot)
        out = []
        for u in range(DEC_SEQS):
            ck = ckbuf[slot, u].astype(BF16)
            kr_t = krbuf[slot, u].astype(BF16)
            s = (_dot_nt(qls[u], ck) + _dot(qrs[u], kr_t)) * SM_SCALE
            out.append(update(carry[u], s, ck))
        return tuple(out)

    init = tuple((jnp.full((nq, 1), NEG_INF, F32), jnp.zeros((nq, 1), F32), jnp.zeros((nq, KV_LORA), F32))
                 for _ in range(DEC_SEQS))
    carry = lax.fori_loop(0, nch, body, init)
    for u in range(DEC_SEQS):
        ckn = ckn_ref[u].astype(BF16)
        krn = krn_ref[u].astype(BF16)
        s = (_dot_nt(qls[u], ckn) + _dot_nt(qrs[u], krn)) * SM_SCALE
        row_t = lax.broadcasted_iota(jnp.int32, s.shape, 0) & (ds - 1)
        col = lax.broadcasted_iota(jnp.int32, s.shape, 1)
        s = jnp.where(col <= row_t, s, NEG_INF)
        _, l, acc = update(carry[u], s, ckn)
        o_ref[u] = (acc / l).astype(BF16)


def _decode(cfg, page_table, ql, qr, ckn, krn, cache_ckv, cache_krope_t):
    db, ds = cfg["DB"], cfg["DS"]
    n_pages = page_table.shape[1]
    page = cache_ckv.shape[1]
    npg = math.gcd(n_pages, 16)
    nq = ql.shape[1]
    assert db % DEC_SEQS == 0
    grid_spec = pltpu.PrefetchScalarGridSpec(
        num_scalar_prefetch=1,
        grid=(db // DEC_SEQS,),
        in_specs=[pl.BlockSpec((DEC_SEQS, nq, KV_LORA), lambda b, pt: (b, 0, 0)),
                  pl.BlockSpec((DEC_SEQS, nq, QK_ROPE), lambda b, pt: (b, 0, 0)),
                  pl.BlockSpec((DEC_SEQS, NEW_PAD, KV_LORA), lambda b, pt: (b, 0, 0)),
                  pl.BlockSpec((DEC_SEQS, NEW_PAD, QK_ROPE), lambda b, pt: (b, 0, 0)),
                  pl.BlockSpec(memory_space=pl.ANY),
                  pl.BlockSpec(memory_space=pl.ANY)],
        out_specs=pl.BlockSpec((DEC_SEQS, nq, KV_LORA), lambda b, pt: (b, 0, 0)),
        scratch_shapes=[pltpu.VMEM((2, DEC_SEQS, npg * page, KV_LORA), F32),
                        pltpu.VMEM((2, DEC_SEQS, QK_ROPE, npg * page), F32),
                        pltpu.SemaphoreType.DMA((2, 2))],
    )
    return pl.pallas_call(
        functools.partial(_decode_kernel, n_pages=n_pages, npg=npg, page=page, ds=ds,
                          n_steps=db // DEC_SEQS),
        grid_spec=grid_spec,
        out_shape=jax.ShapeDtypeStruct((db, nq, KV_LORA), BF16),
        compiler_params=_cparams(("arbitrary",)),
        name="mla_decode",
    )(page_table, ql, qr, ckn, krn, cache_ckv, cache_krope_t)


def _router_kernel(x_ref, g_ref, shp, shs, scp, scs, wr_ref, br_ref, xn_ref, idx_ref, gate_ref, cnt_ref,
                   *, npt):
    i = pl.program_id(0)
    xn = _norm_mod(i, npt, x_ref, g_ref, shp, shs, scp, scs)
    for c in range(ROW_TILE):
        xn_ref[pl.ds(c, TM, stride=ROW_TILE), :] = xn[:, c * LANES:(c + 1) * LANES]
    logits = jnp.dot(xn, wr_ref[...], preferred_element_type=F32,
                     precision=lax.Precision.HIGHEST) + br_ref[...]
    lane_e = lax.broadcasted_iota(jnp.int32, logits.shape, 1).astype(F32)
    lane_o = lax.broadcasted_iota(jnp.int32, (logits.shape[0], LANES), 1)
    idx_out = jnp.zeros((logits.shape[0], LANES), F32)
    val_out = jnp.zeros((logits.shape[0], LANES), F32)
    vals = []
    member = jnp.zeros(logits.shape, F32)
    for kk in range(TOP_K):
        m = jnp.max(logits, axis=-1, keepdims=True)
        sel = jnp.min(jnp.where(logits == m, lane_e, float(N_EXPERTS)), axis=-1, keepdims=True)
        chosen = lane_e == sel
        member = jnp.where(chosen, 1.0, member)
        logits = jnp.where(chosen, -jnp.inf, logits)
        idx_out = jnp.where(lane_o == kk, sel, idx_out)
        vals.append(m)

    @pl.when(i == 0)
    def _():
        cnt_ref[...] = jnp.zeros_like(cnt_ref)

    cnt_ref[...] += jnp.sum(member, axis=0, keepdims=True)
    es = [jnp.exp(v - vals[0]) for v in vals]
    den = es[0] + es[1] + es[2] + es[3]
    for kk in range(TOP_K):
        val_out = jnp.where(lane_o == kk, es[kk] / den, val_out)
    idx_ref[...] = idx_out.astype(jnp.int32)
    gate_ref[...] = val_out


def _router(cfg, h, g, mods, w_r, b_r):
    t, npt, seq_tiles = cfg["T"], cfg["NPT"], cfg["SEQ_TILES"]
    (shp, shs), (scp, scs) = mods
    sh_specs, sh_args = _mod_specs(shp, shs, 3, npt, seq_tiles)
    sc_specs, sc_args = _mod_specs(scp, scs, 4, npt, seq_tiles)
    return pl.pallas_call(
        functools.partial(_router_kernel, npt=npt),
        grid=(t // TM,),
        in_specs=[pl.BlockSpec((TM, D_MODEL), lambda i: (i, 0)),
                  pl.BlockSpec((1, D_MODEL), lambda i: (0, 0)),
                  *sh_specs, *sc_specs,
                  pl.BlockSpec((D_MODEL, N_EXPERTS), lambda i: (0, 0)),
                  pl.BlockSpec((1, N_EXPERTS), lambda i: (0, 0))],
        out_specs=[pl.BlockSpec((TM * ROW_TILE, LANES), lambda i: (i, 0)),
                   pl.BlockSpec((TM, LANES), lambda i: (i, 0)),
                   pl.BlockSpec((TM, LANES), lambda i: (i, 0)),
                   pl.BlockSpec((1, N_EXPERTS), lambda i: (0, 0))],
        out_shape=[jax.ShapeDtypeStruct((t * ROW_TILE, LANES), F32),
                   jax.ShapeDtypeStruct((t, LANES), jnp.int32),
                   jax.ShapeDtypeStruct((t, LANES), F32),
                   jax.ShapeDtypeStruct((1, N_EXPERTS), F32)],
        compiler_params=_cparams(("arbitrary",)),
        name="moe_router",
    )(h, g.reshape(1, D_MODEL), *sh_args, *sc_args, w_r, b_r.reshape(1, N_EXPERTS))


IDX_ALIGN = 128
IDX_WIN = MOE_ROWS + IDX_ALIGN
IDX_SLOT = 512
IDX_SLOTS = 4


def _expert_kernel(be_ref, bo_ref, bv_ref, tok_hbm, dst_hbm, x_hbm, wgu_ref, bgu_ref, wdn_ref, bdn_ref,
                   y_hbm, tok_smem, dst_smem, xbuf, ybuf, wgu_bf, wdn_bf, isem, gsem, ssem,
                   *, n_blocks, n_slots):
    i = pl.program_id(0)
    slot = i & 1
    last = n_blocks - 1
    blk_rows = MOE_ROWS * ROW_TILE
    spill0 = n_slots * ROW_TILE

    def idx_copies(blk):
        s4 = blk & (IDX_SLOTS - 1)
        off = bo_ref[jnp.minimum(blk, last)]
        src = pl.ds(pl.multiple_of(off & -IDX_ALIGN, IDX_ALIGN), IDX_WIN)
        dst = pl.ds(pl.multiple_of(s4 * IDX_SLOT, IDX_ALIGN), IDX_WIN)
        return (pltpu.make_async_copy(tok_hbm.at[src], tok_smem.at[dst], isem.at[0, s4]),
                pltpu.make_async_copy(dst_hbm.at[src], dst_smem.at[dst], isem.at[1, s4]))

    def idx_base(blk):
        return (blk & (IDX_SLOTS - 1)) * IDX_SLOT + (bo_ref[jnp.minimum(blk, last)] & (IDX_ALIGN - 1))

    def issue_gather(blk, buf):
        base = idx_base(blk)
        for r in range(MOE_ROWS):
            row = pl.multiple_of(tok_smem[base + r], ROW_TILE)
            pltpu.make_async_copy(x_hbm.at[pl.ds(row, ROW_TILE)], xbuf.at[buf, pl.ds(r * ROW_TILE, ROW_TILE)],
                                  gsem.at[buf]).start(priority=r % 2)

    def wait_gather(buf):
        pltpu.make_async_copy(x_hbm.at[pl.ds(0, blk_rows)], xbuf.at[buf], gsem.at[buf]).wait()

    def wait_scatter(buf):
        pltpu.make_async_copy(ybuf.at[buf], y_hbm.at[pl.ds(0, blk_rows)], ssem.at[buf]).wait()

    @pl.when(i == 0)
    def _():
        for c in idx_copies(0):
            c.start()
        for c in idx_copies(0):
            c.wait()
        issue_gather(0, 0)
        for c in idx_copies(1):
            c.start()
        ybuf[...] = jnp.zeros_like(ybuf)
        for b in range(2):
            pltpu.make_async_copy(ybuf.at[b], y_hbm.at[pl.ds(spill0 + b * blk_rows, blk_rows)],
                                  ssem.at[b]).start()

    for c in idx_copies(i + 1):
        c.wait()
    wait_gather(slot)
    wait_scatter(slot)

    issue_gather(i + 1, 1 - slot)
    for c in idx_copies(i + 2):
        c.start()

    @pl.when((i == 0) | (be_ref[i] != be_ref[jnp.maximum(i - 1, 0)]))
    def _():
        wgu_bf[...] = wgu_ref[0].astype(BF16)
        wdn_bf[...] = wdn_ref[0].astype(BF16)

    xb = jnp.concatenate([xbuf[slot, pl.ds(c, MOE_ROWS, stride=ROW_TILE), :] for c in range(ROW_TILE)],
                         axis=1).astype(BF16)
    hgu = _dot(xb, wgu_bf[...]) + bgu_ref[0]
    gate = jnp.minimum(hgu[:, :D_EXPERT], SWIGLU_LIMIT)
    up = jnp.clip(hgu[:, D_EXPERT:], -SWIGLU_LIMIT, SWIGLU_LIMIT)
    act = (up + 1.0) * gate * jax.nn.sigmoid(SWIGLU_ALPHA * gate)
    y = _dot(act.astype(BF16), wdn_bf[...]) + bdn_ref[0]
    for c in range(ROW_TILE):
        ybuf[slot, pl.ds(c, MOE_ROWS, stride=ROW_TILE), :] = y[:, c * LANES:(c + 1) * LANES]

    n_valid = bv_ref[i]
    base = idx_base(i)
    spill = spill0 + slot * blk_rows
    for r in range(MOE_ROWS):
        row = pl.multiple_of(jnp.where(r < n_valid, dst_smem[base + r], spill + r * ROW_TILE), ROW_TILE)
        pltpu.make_async_copy(ybuf.at[slot, pl.ds(r * ROW_TILE, ROW_TILE)], y_hbm.at[pl.ds(row, ROW_TILE)],
                              ssem.at[slot]).start(priority=r % 2)

    @pl.when(i == last)
    def _():
        wait_gather(1 - slot)
        for c in idx_copies(i + 2):
            c.wait()
        wait_scatter(1 - slot)
        wait_scatter(slot)


def _experts(cfg, xn_rows, tables, layer, w_gu, b_gu, w_dn, b_dn):
    t = cfg["T"]
    blk_exp, blk_off, blk_valid, tok_rows, dst_rows = tables
    n_blocks = blk_exp.shape[0]
    n_slots = t * TOP_K
    any_spec = pl.BlockSpec(memory_space=pl.ANY)
    e0 = layer * N_EXPERTS
    w_gu = w_gu.reshape(-1, D_MODEL, 2 * D_EXPERT)
    w_dn = w_dn.reshape(-1, D_EXPERT, D_MODEL)
    b_gu = b_gu.reshape(-1, 1, 2 * D_EXPERT)
    b_dn = b_dn.reshape(-1, 1, D_MODEL)
    grid_spec = pltpu.PrefetchScalarGridSpec(
        num_scalar_prefetch=3,
        grid=(n_blocks,),
        in_specs=[any_spec, any_spec, any_spec,
                  pl.BlockSpec((1, D_MODEL, 2 * D_EXPERT), lambda i, be, bo, bv: (e0 + be[i], 0, 0)),
                  pl.BlockSpec((1, 1, 2 * D_EXPERT), lambda i, be, bo, bv: (e0 + be[i], 0, 0)),
                  pl.BlockSpec((1, D_EXPERT, D_MODEL), lambda i, be, bo, bv: (e0 + be[i], 0, 0)),
                  pl.BlockSpec((1, 1, D_MODEL), lambda i, be, bo, bv: (e0 + be[i], 0, 0))],
        out_specs=any_spec,
        scratch_shapes=[pltpu.SMEM((IDX_SLOTS * IDX_SLOT,), jnp.int32),
                        pltpu.SMEM((IDX_SLOTS * IDX_SLOT,), jnp.int32),
                        pltpu.VMEM((2, MOE_ROWS * ROW_TILE, LANES), F32),
                        pltpu.VMEM((2, MOE_ROWS * ROW_TILE, LANES), F32),
                        pltpu.VMEM((D_MODEL, 2 * D_EXPERT), BF16),
                        pltpu.VMEM((D_EXPERT, D_MODEL), BF16),
                        pltpu.SemaphoreType.DMA((2, IDX_SLOTS)),
                        pltpu.SemaphoreType.DMA((2,)),
                        pltpu.SemaphoreType.DMA((2,))],
    )
    return pl.pallas_call(
        functools.partial(_expert_kernel, n_blocks=n_blocks, n_slots=n_slots),
        grid_spec=grid_spec,
        out_shape=jax.ShapeDtypeStruct(((n_slots + 2 * MOE_ROWS) * ROW_TILE, LANES), F32),
        compiler_params=_cparams(("arbitrary",)),
        name="moe_experts",
    )(blk_exp, blk_off, blk_valid, tok_rows, dst_rows, xn_rows, w_gu, b_gu, w_dn, b_dn)


def _combine_kernel(h_ref, y0, y1, y2, y3, gates_ref, gp, gs, o_ref, *, npt):
    i = pl.program_id(0)
    gates = gates_ref[...]
    gmod = _pick(i < npt, gp, gs)
    for c in range(ROW_TILE):
        sl = slice(c * LANES, (c + 1) * LANES)
        rows = pl.ds(c, TM, stride=ROW_TILE)
        y = gates[:, 0:1] * y0[rows, :]
        for kk, y_ref in ((1, y1), (2, y2), (3, y3)):
            y = y + gates[:, kk:kk + 1] * y_ref[rows, :]
        o_ref[:, sl] = h_ref[:, sl] + gmod[:, sl] * y


def _combine(cfg, h, y_rows, gates, gate_mods):
    t, npt, seq_tiles = cfg["T"], cfg["NPT"], cfg["SEQ_TILES"]
    g_specs, g_args = _mod_specs(*gate_mods, 5, npt, seq_tiles)
    nt = t // TM
    y_specs = [pl.BlockSpec((TM * ROW_TILE, LANES), functools.partial(lambda i, kk: (kk * nt + i, 0), kk=kk))
               for kk in range(TOP_K)]
    return pl.pallas_call(
        functools.partial(_combine_kernel, npt=npt),
        grid=(nt,),
        in_specs=[pl.BlockSpec((TM, D_MODEL), lambda i: (i, 0)),
                  *y_specs,
                  pl.BlockSpec((TM, LANES), lambda i: (i, 0)),
                  *g_specs],
        out_specs=pl.BlockSpec((TM, D_MODEL), lambda i: (i, 0)),
        out_shape=jax.ShapeDtypeStruct((t, D_MODEL), F32),
        compiler_params=_cparams(("arbitrary",)),
        name="moe_combine",
    )(h, y_rows, y_rows, y_rows, y_rows, gates, *g_args)


def _dispatch_tables(cfg, top_i, counts):
    t = cfg["T"]
    n_asg = t * TOP_K
    bits = (n_asg - 1).bit_length()
    flat = jnp.arange(n_asg, dtype=jnp.int32).reshape(t, TOP_K)
    keys = jnp.sort(((top_i << bits) | flat).reshape(-1))
    asg = keys & ((1 << bits) - 1)
    tok = asg // TOP_K
    pad = jnp.zeros((2 * MOE_ROWS,), jnp.int32)
    tok_rows = jnp.concatenate([tok * ROW_TILE, pad])
    dst_rows = jnp.concatenate([((asg % TOP_K) * t + tok) * ROW_TILE, pad])
    n_blocks = -(-n_asg // MOE_ROWS) + N_EXPERTS
    nblk = (counts + MOE_ROWS - 1) // MOE_ROWS
    blk_end = jnp.cumsum(nblk)
    grp_start = jnp.cumsum(counts) - counts
    b = jnp.arange(n_blocks, dtype=jnp.int32)
    blk_exp = jnp.minimum(jnp.sum((blk_end[None, :] <= b[:, None]).astype(jnp.int32), axis=1), N_EXPERTS - 1)
    within = (b - (blk_end - nblk)[blk_exp]) * MOE_ROWS
    blk_off = jnp.minimum(grp_start[blk_exp] + within, n_asg).astype(jnp.int32)
    blk_valid = jnp.clip(counts[blk_exp] - within, 0, MOE_ROWS).astype(jnp.int32)
    return blk_exp, blk_off, blk_valid, tok_rows, dst_rows


def _moe(cfg, h, g, mods, gate_mods, w_r, b_r, layer, w_gu, b_gu, w_dn, b_dn):
    xn, idx, gates, counts = _router(cfg, h, g, mods, w_r, b_r)
    tables = _dispatch_tables(cfg, idx[:, :TOP_K], counts[0].astype(jnp.int32))
    yk = _experts(cfg, xn, tables, layer, w_gu, b_gu, w_dn, b_dn)
    return _combine(cfg, h, yk, gates, gate_mods)


def _final_kernel(x_ref, g_ref, shp, shs, scp, scs, op_ref, os_ref, *, npt):
    i = pl.program_id(0)
    y = _norm_mod(i, npt, x_ref, g_ref, shp, shs, scp, scs)

    @pl.when(i < npt)
    def _():
        op_ref[...] = y

    @pl.when(i >= npt)
    def _():
        os_ref[...] = y


def _final(cfg, h, g, mods):
    t, npt, seq_tiles = cfg["T"], cfg["NPT"], cfg["SEQ_TILES"]
    (shp, shs), (scp, scs) = mods
    sh_specs, sh_args = _mod_specs(shp, shs, 0, npt, seq_tiles)
    sc_specs, sc_args = _mod_specs(scp, scs, 1, npt, seq_tiles)
    return pl.pallas_call(
        functools.partial(_final_kernel, npt=npt),
        grid=(t // TM,),
        in_specs=[pl.BlockSpec((TM, D_MODEL), lambda i: (i, 0)),
                  pl.BlockSpec((1, D_MODEL), lambda i: (0, 0)),
                  *sh_specs, *sc_specs],
        out_specs=[pl.BlockSpec((TM, D_MODEL), lambda i: (jnp.minimum(i, npt - 1), 0)),
                   pl.BlockSpec((TM, D_MODEL), lambda i: (0, 0))],
        out_shape=[jax.ShapeDtypeStruct((npt * TM, D_MODEL), F32),
                   jax.ShapeDtypeStruct((TM, D_MODEL), F32)],
        compiler_params=_cparams(("arbitrary",)),
        name="final_norm",
    )(h, g.reshape(1, D_MODEL), *sh_args, *sc_args)


def _rope_tables(pos, group):
    half = group // 2
    inv = ROPE_BASE ** (-jnp.arange(half, dtype=F32) / half)
    ang = pos.astype(F32)[:, None] * inv[None, :]
    cos = jnp.cos(ang)
    sin = jnp.sin(ang)
    reps = LANES // group
    cos_t = jnp.tile(jnp.concatenate([cos, cos], axis=-1), (1, reps))
    sin_t = jnp.tile(jnp.concatenate([-sin, sin], axis=-1), (1, reps))
    return cos_t, sin_t


def _split_mods(cfg, ada):
    b, ds = cfg["B"], cfg["DS"]
    return ada[:b].reshape(b, 1, ada.shape[1]), jnp.repeat(ada[b:], ds, axis=0)


def kernel(x_prompt, x_sample, c_prompt, c_sample, state_ret, cache_ckv, cache_krope, page_table,
           w_ada, b_ada, g_mix, g_ffn, ret_w_in, ret_gn, ret_w_o,
           w_ada_kv, b_ada_kv, g_kv_in, mla_w_kv_a, g_ckv, mla_w_uk, mla_w_uv,
           mla_w_dq, g_cq, mla_w_uq, mla_w_o,
           w_router, b_router, w_gu, b_gu, w_dn, b_dn,
           w_ada_f, b_ada_f, g_final):
    b, s, d = x_prompt.shape
    db, ds, _ = x_sample.shape
    assert d == D_MODEL and db * ds == TM and s % TM == 0 and s % ATT_T == 0
    npr = b * s
    cfg = dict(B=b, S=s, DB=db, DS=ds, NP=npr, T=npr + TM, NPT=npr // TM, SEQ_TILES=s // TM)
    past_len = page_table.shape[1] * cache_ckv.shape[1]

    h = jnp.concatenate([x_prompt.reshape(npr, d), x_sample.reshape(TM, d)], axis=0)
    c_all = jnp.concatenate([c_prompt, c_sample], axis=0)

    pos = jnp.concatenate([jnp.arange(s, dtype=jnp.int32),
                           jnp.tile(past_len + jnp.arange(ds, dtype=jnp.int32), db)])
    cos_r, sin_r = _rope_tables(pos, RET_DK)
    k_scale = RET_DK ** -0.5
    ret_cos = jnp.stack([cos_r, cos_r * k_scale])
    ret_sin = jnp.stack([sin_r, sin_r * k_scale])
    mla_cos, mla_sin = _rope_tables(pos, QK_ROPE)

    mods_kv = _split_mods(cfg, _adaln(c_all, w_ada_kv, b_ada_kv))
    mods_f = _split_mods(cfg, _adaln(c_all, w_ada_f, b_ada_f))

    n_a = state_ret.shape[0]
    depth = w_ada.shape[0]
    ret_p, ret_s = [], []
    outs_kv = None
    for l in range(depth):
        mods = _split_mods(cfg, _adaln(c_all, w_ada, b_ada, layer=l))
        mm = (mods, mods)
        if l < n_a:
            z = _ret_inproj(cfg, h, g_mix[l], mm, ret_w_in[l].astype(BF16), ret_cos, ret_sin)
            a_p, st_p = _ret_prompt(cfg, z, ret_gn[l])
            a_s, st_s = _ret_sample(cfg, z, ret_gn[l], state_ret[l])
            ret_p.append(st_p)
            ret_s.append(st_s)
            h = _mm_res(cfg, a_p, a_s, ret_w_o[l].astype(BF16), h, mods, 2)
        else:
            bl = l - n_a
            if bl == 0:
                wa_pad = jnp.pad(mla_w_kv_a, ((0, 0), (0, KV_A_PAD - mla_w_kv_a.shape[1]))).astype(BF16)
                w_ukv = jnp.concatenate([mla_w_uk.reshape(KV_LORA, -1), mla_w_uv.reshape(KV_LORA, -1)],
                                        axis=1).astype(BF16)
                ckv_p, ckv_s, kr_p_t, kr_s, k_cat, v_all = _mla_kv(
                    cfg, h, g_kv_in, (mods_kv, mods_kv), wa_pad, g_ckv, mla_cos, mla_sin, w_ukv)
                outs_kv = (ckv_p, ckv_s, kr_p_t, kr_s)
                ckn = jnp.pad(ckv_s.reshape(db, ds, KV_LORA), ((0, 0), (0, NEW_PAD - ds), (0, 0)))
                krn = jnp.pad(kr_s.reshape(db, ds, QK_ROPE), ((0, 0), (0, NEW_PAD - ds), (0, 0)))
                w_uk_t = jnp.transpose(mla_w_uk, (1, 2, 0)).astype(BF16)
                w_uv_h = jnp.transpose(mla_w_uv, (1, 0, 2)).astype(BF16)
            cq = _mla_dq(cfg, h, g_mix[l], mm, mla_w_dq[bl].astype(BF16), g_cq[bl])
            w_uq = mla_w_uq[bl].reshape(Q_LORA, MLA_HEADS, QK_HEAD)
            w_uq_perm = jnp.concatenate([w_uq[:, :, :QK_NOPE].reshape(Q_LORA, -1),
                                         w_uq[:, :, QK_NOPE:].reshape(Q_LORA, -1)], axis=1).astype(BF16)
            q_cat = _mla_uq(cfg, cq, w_uq_perm, mla_cos, mla_sin)
            o_p = _flash(cfg, q_cat, k_cat, v_all)
            q_s = q_cat[:, npr:, :]
            ql = _head_bmm(q_s[:, :, :QK_NOPE], w_uk_t)
            to_seq = lambda x: x.reshape(MLA_HEADS, db, ds, x.shape[-1]).transpose(1, 0, 2, 3).reshape(
                db, MLA_HEADS * ds, x.shape[-1])
            ctx = _decode(cfg, page_table, to_seq(ql), to_seq(q_s[:, :, QK_NOPE:]), ckn, krn,
                          cache_ckv, jnp.swapaxes(cache_krope, 1, 2))
            ctx_h = ctx.reshape(db, MLA_HEADS, ds, KV_LORA).transpose(1, 0, 2, 3).reshape(MLA_HEADS, TM, KV_LORA)
            o_s = _head_bmm(ctx_h, w_uv_h).transpose(1, 0, 2).reshape(TM, MLA_HEADS * V_HEAD)
            h = _mm_res(cfg, o_p, o_s, mla_w_o[bl].astype(BF16), h, mods, 2)
        h = _moe(cfg, h, g_ffn[l], mm, mods, w_router[l], b_router[l],
                 l, w_gu, b_gu, w_dn, b_dn)

    y_p, y_s = _final(cfg, h, g_final, (mods_f, mods_f))
    ckv_p, ckv_s, kr_p_t, kr_s = outs_kv
    return (y_p.reshape(b, s, d), y_s.reshape(db, ds, d),
            jnp.stack(ret_p, axis=0), jnp.stack(ret_s, axis=0),
            ckv_p.reshape(b, s, KV_LORA), jnp.swapaxes(kr_p_t, 1, 2),
            ckv_s.reshape(db, ds, KV_LORA), kr_s.reshape(db, ds, QK_ROPE))
```

```python
import functools
import math

import jax
import jax.numpy as jnp
from jax import lax
from jax.experimental import pallas as pl
from jax.experimental.pallas import tpu as pltpu

F32 = jnp.float32
BF16 = jnp.bfloat16

D_MODEL = 1024
RET_HEADS = 8
RET_DK = 128
RET_DV = 256
RET_CHUNK = 128
MLA_HEADS = 8
QK_NOPE = 128
QK_ROPE = 64
QK_HEAD = QK_NOPE + QK_ROPE
V_HEAD = 128
KV_LORA = 256
Q_LORA = 768
SM_SCALE = QK_HEAD ** -0.5
N_EXPERTS = 32
TOP_K = 4
D_EXPERT = 1024
SWIGLU_LIMIT = 7.0
SWIGLU_ALPHA = 1.702
ROPE_BASE = 10000.0
EPS = 1e-6
NEG_INF = -1e30

LANES = 128
ROW_TILE = D_MODEL // LANES
TM = 512
MOE_ROWS = 256
ATT_T = 512
LOG2_E = math.log2(math.e)
VMEM_LIMIT = 56 * 1024 * 1024

LOG_DECAY = tuple(math.log1p(-2.0 ** (-5.0 - h)) for h in range(RET_HEADS))


def _cparams(sem):
    return pltpu.CompilerParams(dimension_semantics=sem, vmem_limit_bytes=VMEM_LIMIT)


def _rms(x, g):
    return x * lax.rsqrt(jnp.mean(x * x, axis=-1, keepdims=True) + EPS) * g


def _dot(a, b):
    return jnp.dot(a, b, preferred_element_type=F32)


def _dot_nt(a, b):
    return lax.dot_general(a, b, (((1,), (1,)), ((), ())), preferred_element_type=F32)


def _dot_tn(a, b):
    return lax.dot_general(a, b, (((0,), (0,)), ((), ())), preferred_element_type=F32)


def _rope_group(x, cos, sin_signed, group):
    half = group // 2
    lane = lax.broadcasted_iota(jnp.int32, x.shape, 1)
    x_up = pltpu.roll(x, LANES - half, 1)
    x_dn = pltpu.roll(x, half, 1)
    rot = jnp.where((lane & (group - 1)) < half, x_up, x_dn)
    return x * cos + rot * sin_signed


def _pick(is_prompt, p_ref, s_ref):
    return jnp.where(is_prompt, p_ref[0], s_ref[...])


def _mod_specs(mod_p, mod_s, chunk, npt, seq_tiles):
    dm = D_MODEL
    sp = pl.BlockSpec((1, 1, dm), lambda i, *_: (jnp.minimum(i, npt - 1) // seq_tiles, 0, chunk))
    ss = pl.BlockSpec((TM, dm), lambda i, *_: (0, chunk))
    return [sp, ss], [mod_p, mod_s]


def _norm_mod(i, npt, x_ref, g_ref, shp, shs, scp, scs):
    is_p = i < npt
    xn = _rms(x_ref[...], g_ref[...])
    return xn * (1.0 + _pick(is_p, scp, scs)) + _pick(is_p, shp, shs)


def _adaln_kernel(c_ref, w_ref, b_ref, o_ref):
    c = c_ref[...]
    a = (c * jax.nn.sigmoid(c)).astype(BF16)
    o_ref[...] = _dot(a, w_ref[...].astype(BF16)) + b_ref[...]


def _adaln(c_all, w, b, layer=None):
    n = w.shape[-1]
    tn = 512
    rows = c_all.shape[0]
    if layer is None:
        w_spec = pl.BlockSpec((D_MODEL, tn), lambda j: (0, j))
        b_spec = pl.BlockSpec((1, tn), lambda j: (0, j))
        b = b.reshape(1, n)
    else:
        w_spec = pl.BlockSpec((None, D_MODEL, tn), lambda j: (layer, 0, j))
        b_spec = pl.BlockSpec((None, 1, tn), lambda j: (layer, 0, j))
        b = b.reshape(b.shape[0], 1, n)
    return pl.pallas_call(
        _adaln_kernel,
        grid=(n // tn,),
        in_specs=[pl.BlockSpec((rows, D_MODEL), lambda j: (0, 0)), w_spec, b_spec],
        out_specs=pl.BlockSpec((rows, tn), lambda j: (0, j)),
        out_shape=jax.ShapeDtypeStruct((rows, n), F32),
        compiler_params=_cparams(("arbitrary",)),
        name="adaln",
    )(c_all, w, b)


def _inproj_kernel(x_ref, g_ref, shp, shs, scp, scs, w_ref, cos_ref, sin_ref, o_ref, xn_scr,
                   *, npt, tn, n_rope):
    i = pl.program_id(0)
    j = pl.program_id(1)

    @pl.when(j == 0)
    def _():
        xn_scr[...] = _norm_mod(i, npt, x_ref, g_ref, shp, shs, scp, scs).astype(BF16)

    acc = _dot(xn_scr[...], w_ref[...])

    @pl.when(j < n_rope)
    def _():
        cos = cos_ref[0]
        sin = sin_ref[0]
        for c in range(tn // LANES):
            sl = slice(c * LANES, (c + 1) * LANES)
            o_ref[:, sl] = _rope_group(acc[:, sl], cos, sin, RET_DK).astype(BF16)

    @pl.when(j >= n_rope)
    def _():
        o_ref[...] = acc.astype(BF16)


def _ret_inproj(cfg, h, g, mods, w, cos_tab, sin_tab):
    t, npt, seq_tiles = cfg["T"], cfg["NPT"], cfg["SEQ_TILES"]
    n = w.shape[1]
    tn = 1024
    n_rope = (2 * RET_HEADS * RET_DK) // tn
    (shp, shs), (scp, scs) = mods
    sh_specs, sh_args = _mod_specs(shp, shs, 0, npt, seq_tiles)
    sc_specs, sc_args = _mod_specs(scp, scs, 1, npt, seq_tiles)
    tab_spec = pl.BlockSpec(
        (1, TM, LANES),
        lambda i, j: (jnp.minimum(j, n_rope - 1), jnp.where(i < npt, i % seq_tiles, seq_tiles), 0))
    return pl.pallas_call(
        functools.partial(_inproj_kernel, npt=npt, tn=tn, n_rope=n_rope),
        grid=(t // TM, n // tn),
        in_specs=[pl.BlockSpec((TM, D_MODEL), lambda i, j: (i, 0)),
                  pl.BlockSpec((1, D_MODEL), lambda i, j: (0, 0)),
                  *sh_specs, *sc_specs,
                  pl.BlockSpec((D_MODEL, tn), lambda i, j: (0, j)),
                  tab_spec, tab_spec],
        out_specs=pl.BlockSpec((TM, tn), lambda i, j: (i, j)),
        out_shape=jax.ShapeDtypeStruct((t, n), BF16),
        scratch_shapes=[pltpu.VMEM((TM, D_MODEL), BF16)],
        compiler_params=_cparams(("arbitrary", "arbitrary")),
        name="ret_inproj",
    )(h, g.reshape(1, D_MODEL), *sh_args, *sc_args, w, cos_tab, sin_tab)


def _group_norm_gate(o, gn_row, g_bf16):
    mu = jnp.mean(o, axis=-1, keepdims=True)
    var = jnp.mean(jnp.square(o - mu), axis=-1, keepdims=True)
    on = (o - mu) * lax.rsqrt(var + EPS) * gn_row
    gg = g_bf16.astype(F32)
    return (gg * jax.nn.sigmoid(gg) * on).astype(BF16)


def _ret_prompt_kernel(q_ref, k_ref, v_ref, g_ref, gn_ref, a_ref, st_ref):
    c = pl.program_id(1)
    ch = RET_CHUNK

    @pl.when(c == 0)
    def _():
        st_ref[...] = jnp.zeros_like(st_ref)

    row = lax.broadcasted_iota(jnp.int32, (ch, ch), 0)
    col = lax.broadcasted_iota(jnp.int32, (ch, ch), 1)
    causal = row >= col
    diff = jnp.where(causal, (row - col).astype(F32), 0.0)
    ridx = lax.broadcasted_iota(jnp.int32, (ch, 1), 0).astype(F32)
    for h in range(RET_HEADS):
        lg = LOG_DECAY[h]
        ks = slice(h * RET_DK, (h + 1) * RET_DK)
        vs = slice(h * RET_DV, (h + 1) * RET_DV)
        q = q_ref[:, ks]
        k = k_ref[:, ks]
        v = v_ref[:, vs]
        st = st_ref[0, h]
        decay = jnp.where(causal, jnp.exp(diff * lg), 0.0)
        s = _dot_nt(q, k) * decay
        o = _dot(s.astype(BF16), v) + jnp.exp((ridx + 1.0) * lg) * _dot(q, st.astype(BF16))
        kd = (k.astype(F32) * jnp.exp((ch - 1.0 - ridx) * lg)).astype(BF16)
        st_ref[0, h] = math.exp(ch * lg) * st + _dot_tn(kd, v)
        a_ref[:, vs] = _group_norm_gate(o, gn_ref[h:h + 1, :], g_ref[:, vs])


def _ret_prompt(cfg, z, gn):
    b, s = cfg["B"], cfg["S"]
    nc = s // RET_CHUNK
    hk = RET_HEADS * RET_DK
    hv = RET_HEADS * RET_DV
    return pl.pallas_call(
        _ret_prompt_kernel,
        grid=(b, nc),
        in_specs=[pl.BlockSpec((RET_CHUNK, hk), lambda bi, c: (bi * nc + c, 0)),
                  pl.BlockSpec((RET_CHUNK, hk), lambda bi, c: (bi * nc + c, 1)),
                  pl.BlockSpec((RET_CHUNK, hv), lambda bi, c: (bi * nc + c, 1)),
                  pl.BlockSpec((RET_CHUNK, hv), lambda bi, c: (bi * nc + c, 2)),
                  pl.BlockSpec((RET_HEADS, RET_DV), lambda bi, c: (0, 0))],
        out_specs=[pl.BlockSpec((RET_CHUNK, hv), lambda bi, c: (bi * nc + c, 0)),
                   pl.BlockSpec((1, RET_HEADS, RET_DK, RET_DV), lambda bi, c: (bi, 0, 0, 0))],
        out_shape=[jax.ShapeDtypeStruct((b * s, hv), BF16),
                   jax.ShapeDtypeStruct((b, RET_HEADS, RET_DK, RET_DV), F32)],
        compiler_params=_cparams(("arbitrary", "arbitrary")),
        name="ret_prompt",
    )(z, z, z, z, gn)


RS_SEQ = 4


def _ret_sample_kernel(q_ref, k_ref, v_ref, g_ref, gn_ref, st_in_ref, a_ref, st_out_ref, *, ds):
    rows = RS_SEQ * ds
    shift = ds.bit_length() - 1
    row = lax.broadcasted_iota(jnp.int32, (rows, rows), 0)
    col = lax.broadcasted_iota(jnp.int32, (rows, rows), 1)
    causal = jnp.where(row >= col, (row >> shift) - (col >> shift), -1) == 0
    diff = jnp.where(causal, (row - col).astype(F32), 0.0)
    r1 = lax.broadcasted_iota(jnp.int32, (rows, 1), 0)
    t_idx = (r1 & (ds - 1)).astype(F32)
    seq_of_row = r1 >> shift
    for h in range(RET_HEADS):
        lg = LOG_DECAY[h]
        ks = slice(h * RET_DK, (h + 1) * RET_DK)
        vs = slice(h * RET_DV, (h + 1) * RET_DV)
        q = q_ref[:, ks]
        k = k_ref[:, ks]
        v = v_ref[:, vs]
        decay = jnp.where(causal, jnp.exp(diff * lg), 0.0)
        s = _dot_nt(q, k) * decay
        o = _dot(s.astype(BF16), v)
        qd = jnp.exp((t_idx + 1.0) * lg)
        kd = k.astype(F32) * jnp.exp((ds - 1.0 - t_idx) * lg)
        for sq in range(RS_SEQ):
            st = st_in_ref[sq, h]
            mine = seq_of_row == sq
            o = o + jnp.where(mine, qd * _dot(q, st.astype(BF16)), 0.0)
            kds = jnp.where(mine, kd, 0.0).astype(BF16)
            st_out_ref[sq, h] = math.exp(ds * lg) * st + _dot_tn(kds, v)
        a_ref[:, vs] = _group_norm_gate(o, gn_ref[h:h + 1, :], g_ref[:, vs])


def _ret_sample(cfg, z, gn, state):
    db, ds, npr = cfg["DB"], cfg["DS"], cfg["NP"]
    rows = RS_SEQ * ds
    assert ds & (ds - 1) == 0 and db % RS_SEQ == 0 and npr % rows == 0
    base = npr // rows
    hk = RET_HEADS * RET_DK
    hv = RET_HEADS * RET_DV
    st_spec = pl.BlockSpec((RS_SEQ, RET_HEADS, RET_DK, RET_DV), lambda i: (i, 0, 0, 0))
    return pl.pallas_call(
        functools.partial(_ret_sample_kernel, ds=ds),
        grid=(db // RS_SEQ,),
        in_specs=[pl.BlockSpec((rows, hk), lambda i: (base + i, 0)),
                  pl.BlockSpec((rows, hk), lambda i: (base + i, 1)),
                  pl.BlockSpec((rows, hv), lambda i: (base + i, 1)),
                  pl.BlockSpec((rows, hv), lambda i: (base + i, 2)),
                  pl.BlockSpec((RET_HEADS, RET_DV), lambda i: (0, 0)),
                  st_spec],
        out_specs=[pl.BlockSpec((rows, hv), lambda i: (i, 0)), st_spec],
        out_shape=[jax.ShapeDtypeStruct((db * ds, hv), BF16),
                   jax.ShapeDtypeStruct((db, RET_HEADS, RET_DK, RET_DV), F32)],
        compiler_params=_cparams(("arbitrary",)),
        name="ret_sample",
    )(z, z, z, z, gn, state)


KV_A_PAD = KV_LORA + LANES


def _kv_kernel(x_ref, g_ref, shp, shs, scp, scs, wa_ref, gck_ref, cos_ref, sin_ref, wukv_ref,
               ckvp_ref, ckvs_ref, krp_ref, krs_ref, kcat_ref, v_ref, *, npt):
    i = pl.program_id(0)
    xn = _norm_mod(i, npt, x_ref, g_ref, shp, shs, scp, scs).astype(BF16)
    z = _dot(xn, wa_ref[...])
    ckv = _rms(z[:, :KV_LORA], gck_ref[...])
    kr_lanes = _rope_group(z[:, KV_LORA:], cos_ref[...], sin_ref[...], QK_ROPE)
    kr = kr_lanes[:, :QK_ROPE]

    @pl.when(i < npt)
    def _():
        ckvp_ref[...] = ckv
        krp_ref[0] = kr_lanes.T[:QK_ROPE, :]

    @pl.when(i >= npt)
    def _():
        ckvs_ref[...] = ckv
        krs_ref[...] = kr

    up = _dot(ckv.astype(BF16), wukv_ref[...])
    kr16 = kr.astype(BF16)
    for h in range(MLA_HEADS):
        kcat_ref[h, :, :QK_NOPE] = up[:, h * QK_NOPE:(h + 1) * QK_NOPE].astype(BF16)
        kcat_ref[h, :, QK_NOPE:] = kr16
    v_ref[...] = up[:, MLA_HEADS * QK_NOPE:].astype(BF16)


def _mla_kv(cfg, h, g, mods, wa_pad, g_ckv, cos_tab, sin_tab, w_ukv):
    t, npt, seq_tiles = cfg["T"], cfg["NPT"], cfg["SEQ_TILES"]
    (shp, shs), (scp, scs) = mods
    sh_specs, sh_args = _mod_specs(shp, shs, 0, npt, seq_tiles)
    sc_specs, sc_args = _mod_specs(scp, scs, 1, npt, seq_tiles)
    tab_spec = pl.BlockSpec((TM, LANES), lambda i: (jnp.where(i < npt, i % seq_tiles, seq_tiles), 0))
    n_up = w_ukv.shape[1]
    return pl.pallas_call(
        functools.partial(_kv_kernel, npt=npt),
        grid=(t // TM,),
        in_specs=[pl.BlockSpec((TM, D_MODEL), lambda i: (i, 0)),
                  pl.BlockSpec((1, D_MODEL), lambda i: (0, 0)),
                  *sh_specs, *sc_specs,
                  pl.BlockSpec((D_MODEL, KV_A_PAD), lambda i: (0, 0)),
                  pl.BlockSpec((1, KV_LORA), lambda i: (0, 0)),
                  tab_spec, tab_spec,
                  pl.BlockSpec((KV_LORA, n_up), lambda i: (0, 0))],
        out_specs=[pl.BlockSpec((TM, KV_LORA), lambda i: (jnp.minimum(i, npt - 1), 0)),
                   pl.BlockSpec((TM, KV_LORA), lambda i: (0, 0)),
                   pl.BlockSpec((1, QK_ROPE, TM),
                                lambda i: (jnp.minimum(i, npt - 1) // seq_tiles, 0, jnp.minimum(i, npt - 1) % seq_tiles)),
                   pl.BlockSpec((TM, QK_ROPE), lambda i: (0, 0)),
                   pl.BlockSpec((MLA_HEADS, TM, QK_HEAD), lambda i: (0, i, 0)),
                   pl.BlockSpec((TM, MLA_HEADS * V_HEAD), lambda i: (i, 0))],
        out_shape=[jax.ShapeDtypeStruct((npt * TM, KV_LORA), F32),
                   jax.ShapeDtypeStruct((TM, KV_LORA), F32),
                   jax.ShapeDtypeStruct((npt // seq_tiles, QK_ROPE, seq_tiles * TM), F32),
                   jax.ShapeDtypeStruct((TM, QK_ROPE), F32),
                   jax.ShapeDtypeStruct((MLA_HEADS, t, QK_HEAD), BF16),
                   jax.ShapeDtypeStruct((t, MLA_HEADS * V_HEAD), BF16)],
        compiler_params=_cparams(("arbitrary",)),
        name="mla_kv",
    )(h, g.reshape(1, D_MODEL), *sh_args, *sc_args, wa_pad, g_ckv.reshape(1, KV_LORA),
      cos_tab, sin_tab, w_ukv)


def _dq_kernel(x_ref, g_ref, shp, shs, scp, scs, w_ref, gcq_ref, o_ref, *, npt):
    i = pl.program_id(0)
    xn = _norm_mod(i, npt, x_ref, g_ref, shp, shs, scp, scs).astype(BF16)
    o_ref[...] = _rms(_dot(xn, w_ref[...]), gcq_ref[...]).astype(BF16)


def _mla_dq(cfg, h, g, mods, w_dq, g_cq):
    t, npt, seq_tiles = cfg["T"], cfg["NPT"], cfg["SEQ_TILES"]
    (shp, shs), (scp, scs) = mods
    sh_specs, sh_args = _mod_specs(shp, shs, 0, npt, seq_tiles)
    sc_specs, sc_args = _mod_specs(scp, scs, 1, npt, seq_tiles)
    return pl.pallas_call(
        functools.partial(_dq_kernel, npt=npt),
        grid=(t // TM,),
        in_specs=[pl.BlockSpec((TM, D_MODEL), lambda i: (i, 0)),
                  pl.BlockSpec((1, D_MODEL), lambda i: (0, 0)),
                  *sh_specs, *sc_specs,
                  pl.BlockSpec((D_MODEL, Q_LORA), lambda i: (0, 0)),
                  pl.BlockSpec((1, Q_LORA), lambda i: (0, 0))],
        out_specs=pl.BlockSpec((TM, Q_LORA), lambda i: (i, 0)),
        out_shape=jax.ShapeDtypeStruct((t, Q_LORA), BF16),
        compiler_params=_cparams(("arbitrary",)),
        name="mla_dq",
    )(h, g.reshape(1, D_MODEL), *sh_args, *sc_args, w_dq, g_cq.reshape(1, Q_LORA))


def _uq_kernel(cq_ref, w_ref, cos_ref, sin_ref, o_ref):
    z = _dot(cq_ref[...], w_ref[...])
    n_nope = MLA_HEADS * QK_NOPE
    for h in range(MLA_HEADS):
        o_ref[h, :, :QK_NOPE] = z[:, h * QK_NOPE:(h + 1) * QK_NOPE].astype(BF16)
    cos = cos_ref[...]
    sin = sin_ref[...]
    heads_per_group = LANES // QK_ROPE
    for c in range(MLA_HEADS // heads_per_group):
        r = _rope_group(z[:, n_nope + c * LANES:n_nope + (c + 1) * LANES], cos, sin, QK_ROPE).astype(BF16)
        for u in range(heads_per_group):
            o_ref[c * heads_per_group + u, :, QK_NOPE:] = r[:, u * QK_ROPE:(u + 1) * QK_ROPE]


def _mla_uq(cfg, cq, w_uq_perm, cos_tab, sin_tab):
    t, npt, seq_tiles = cfg["T"], cfg["NPT"], cfg["SEQ_TILES"]
    n = w_uq_perm.shape[1]
    tab_spec = pl.BlockSpec((TM, LANES), lambda i: (jnp.where(i < npt, i % seq_tiles, seq_tiles), 0))
    return pl.pallas_call(
        _uq_kernel,
        grid=(t // TM,),
        in_specs=[pl.BlockSpec((TM, Q_LORA), lambda i: (i, 0)),
                  pl.BlockSpec((Q_LORA, n), lambda i: (0, 0)),
                  tab_spec, tab_spec],
        out_specs=pl.BlockSpec((MLA_HEADS, TM, QK_HEAD), lambda i: (0, i, 0)),
        out_shape=jax.ShapeDtypeStruct((MLA_HEADS, t, QK_HEAD), BF16),
        compiler_params=_cparams(("arbitrary",)),
        name="mla_uq",
    )(cq, w_uq_perm, cos_tab, sin_tab)


def _flash_kernel(q_ref, k_ref, v_ref, o_ref):
    qi = pl.program_id(2)
    q = q_ref[0]
    tq = q.shape[0]

    def step(ki, carry, masked):
        m, l, acc = carry
        start = pl.multiple_of(ki * ATT_T, ATT_T)
        k = k_ref[0, pl.ds(start, ATT_T), :]
        v = v_ref[pl.ds(start, ATT_T), :]
        s = _dot_nt(q, k) * (SM_SCALE * LOG2_E)
        if masked:
            row = lax.broadcasted_iota(jnp.int32, s.shape, 0)
            col = lax.broadcasted_iota(jnp.int32, s.shape, 1)
            s = jnp.where(col <= row, s, NEG_INF)
        m_new = jnp.maximum(m, jnp.max(s, axis=-1, keepdims=True))
        alpha = jnp.exp2(m - m_new)
        p = jnp.exp2(s - m_new)
        l = alpha * l + jnp.sum(p, axis=-1, keepdims=True)
        acc = alpha * acc + _dot(p.astype(BF16), v)
        return m_new, l, acc

    carry = (jnp.full((tq, 1), NEG_INF, F32), jnp.zeros((tq, 1), F32), jnp.zeros((tq, V_HEAD), F32))
    carry = lax.fori_loop(0, qi // 2, lambda j, c: step(2 * j + 1, step(2 * j, c, False), False), carry)
    carry = lax.cond(qi % 2 == 1, lambda c: step(qi - 1, c, False), lambda c: c, carry)
    _, l, acc = step(qi, carry, True)
    o_ref[...] = (acc / l).astype(BF16)


def _flash(cfg, q_cat, k_cat, v):
    b, s, npr = cfg["B"], cfg["S"], cfg["NP"]
    nq = s // ATT_T
    return pl.pallas_call(
        _flash_kernel,
        grid=(b, MLA_HEADS, nq),
        in_specs=[pl.BlockSpec((1, ATT_T, QK_HEAD), lambda bi, h, qi: (h, bi * nq + qi, 0)),
                  pl.BlockSpec((1, s, QK_HEAD), lambda bi, h, qi: (h, bi, 0)),
                  pl.BlockSpec((s, V_HEAD), lambda bi, h, qi: (bi, h))],
        out_specs=pl.BlockSpec((ATT_T, V_HEAD), lambda bi, h, qi: (bi * nq + qi, h)),
        out_shape=jax.ShapeDtypeStruct((npr, MLA_HEADS * V_HEAD), BF16),
        compiler_params=_cparams(("arbitrary", "arbitrary", "arbitrary")),
        name="mla_flash",
    )(q_cat, k_cat, v)


def _bmm_kernel(x_ref, w_ref, o_ref):
    o_ref[0] = _dot(x_ref[0], w_ref[0]).astype(o_ref.dtype)


def _head_bmm(x, w):
    hh, m, k = x.shape
    n = w.shape[2]
    return pl.pallas_call(
        _bmm_kernel,
        grid=(hh,),
        in_specs=[pl.BlockSpec((1, m, k), lambda h: (h, 0, 0)),
                  pl.BlockSpec((1, k, n), lambda h: (h, 0, 0))],
        out_specs=pl.BlockSpec((1, m, n), lambda h: (h, 0, 0)),
        out_shape=jax.ShapeDtypeStruct((hh, m, n), BF16),
        compiler_params=_cparams(("arbitrary",)),
        name="head_bmm",
    )(x, w)


NEW_PAD = 16


DEC_SEQS = 2


def _decode_kernel(pt_ref, ql_ref, qr_ref, ckn_ref, krn_ref, ck_hbm, kr_hbm, o_ref,
                   ckbuf, krbuf, sem, *, n_pages, npg, page, ds, n_steps):
    g = pl.program_id(0)
    nch = n_pages // npg
    nk = npg * page

    def start(step, c, slot):
        for u in range(DEC_SEQS):
            for p in range(npg):
                pg = pt_ref[step * DEC_SEQS + u, c * npg + p]
                pltpu.make_async_copy(ck_hbm.at[pg], ckbuf.at[slot, u, pl.ds(p * page, page)],
                                      sem.at[0, slot]).start()
                pltpu.make_async_copy(kr_hbm.at[pg], krbuf.at[slot, u, :, pl.ds(p * page, page)],
                                      sem.at[1, slot]).start()

    def wait(slot):
        pltpu.make_async_copy(ckbuf.at[slot], ckbuf.at[slot], sem.at[0, slot]).wait()
        pltpu.make_async_copy(krbuf.at[slot], krbuf.at[slot], sem.at[1, slot]).wait()

    qls = [ql_ref[u] for u in range(DEC_SEQS)]
    qrs = [qr_ref[u] for u in range(DEC_SEQS)]
    nq = qls[0].shape[0]

    def update(carry, s, vals):
        m, l, acc = carry
        m_new = jnp.maximum(m, jnp.max(s, axis=-1, keepdims=True))
        alpha = jnp.exp(m - m_new)
        p = jnp.exp(s - m_new)
        l = alpha * l + jnp.sum(p, axis=-1, keepdims=True)
        acc = alpha * acc + _dot(p.astype(BF16), vals)
        return m_new, l, acc

    @pl.when(g == 0)
    def _():
        start(0, 0, 0)

    def body(c, carry):
        slot = (g * nch + c) & 1

        @pl.when(c + 1 < nch)
        def _():
            start(g, c + 1, 1 - slot)

        @pl.when((c + 1 == nch) & (g + 1 < n_steps))
        def _():
            start(g + 1, 0, 1 - slot)

        wait(slot)
        out = []
        for u in range(DEC_SEQS):
            ck = ckbuf[slot, u].astype(BF16)
            kr_t = krbuf[slot, u].astype(BF16)
            s = (_dot_nt(qls[u], ck) + _dot(qrs[u], kr_t)) * SM_SCALE
            out.append(update(carry[u], s, ck))
        return tuple(out)

    init = tuple((jnp.full((nq, 1), NEG_INF, F32), jnp.zeros((nq, 1), F32), jnp.zeros((nq, KV_LORA), F32))
                 for _ in range(DEC_SEQS))
    carry = lax.fori_loop(0, nch, body, init)
    for u in range(DEC_SEQS):
        ckn = ckn_ref[u].astype(BF16)
        krn = krn_ref[u].astype(BF16)
        s = (_dot_nt(qls[u], ckn) + _dot_nt(qrs[u], krn)) * SM_SCALE
        row_t = lax.broadcasted_iota(jnp.int32, s.shape, 0) & (ds - 1)
        col = lax.broadcasted_iota(jnp.int32, s.shape, 1)
        s = jnp.where(col <= row_t, s, NEG_INF)
        _, l, acc = update(carry[u], s, ckn)
        o_ref[u] = (acc / l).astype(BF16)


def _decode(cfg, page_table, ql, qr, ckn, krn, cache_ckv, cache_krope_t):
    db, ds = cfg["DB"], cfg["DS"]
    n_pages = page_table.shape[1]
    page = cache_ckv.shape[1]
    npg = math.gcd(n_pages, 16)
    nq = ql.shape[1]
    assert db % DEC_SEQS == 0
    grid_spec = pltpu.PrefetchScalarGridSpec(
        num_scalar_prefetch=1,
        grid=(db // DEC_SEQS,),
        in_specs=[pl.BlockSpec((DEC_SEQS, nq, KV_LORA), lambda b, pt: (b, 0, 0)),
                  pl.BlockSpec((DEC_SEQS, nq, QK_ROPE), lambda b, pt: (b, 0, 0)),
                  pl.BlockSpec((DEC_SEQS, NEW_PAD, KV_LORA), lambda b, pt: (b, 0, 0)),
                  pl.BlockSpec((DEC_SEQS, NEW_PAD, QK_ROPE), lambda b, pt: (b, 0, 0)),
                  pl.BlockSpec(memory_space=pl.ANY),
                  pl.BlockSpec(memory_space=pl.ANY)],
        out_specs=pl.BlockSpec((DEC_SEQS, nq, KV_LORA), lambda b, pt: (b, 0, 0)),
        scratch_shapes=[pltpu.VMEM((2, DEC_SEQS, npg * page, KV_LORA), F32),
                        pltpu.VMEM((2, DEC_SEQS, QK_ROPE, npg * page), F32),
                        pltpu.SemaphoreType.DMA((2, 2))],
    )
    return pl.pallas_call(
        functools.partial(_decode_kernel, n_pages=n_pages, npg=npg, page=page, ds=ds,
                          n_steps=db // DEC_SEQS),
        grid_spec=grid_spec,
        out_shape=jax.ShapeDtypeStruct((db, nq, KV_LORA), BF16),
        compiler_params=_cparams(("arbitrary",)),
        name="mla_decode",
    )(page_table, ql, qr, ckn, krn, cache_ckv, cache_krope_t)


def _router_kernel(ap_ref, as_ref, wo_ref, res_ref, gp, gs, g_ref, shp, shs, scp, scs, wr_ref, br_ref,
                   h_ref, xn_ref, idx_ref, gate_ref, cnt_ref, *, npt):
    i = pl.program_id(0)
    gate = _pick(i < npt, gp, gs)

    def out_proj(a_ref):
        h_ref[...] = res_ref[...] + gate * _dot(a_ref[...], wo_ref[...])

    pl.when(i < npt)(functools.partial(out_proj, ap_ref))
    pl.when(i >= npt)(functools.partial(out_proj, as_ref))
    xn = _norm_mod(i, npt, h_ref, g_ref, shp, shs, scp, scs)
    for c in range(ROW_TILE):
        xn_ref[pl.ds(c, TM, stride=ROW_TILE), :] = xn[:, c * LANES:(c + 1) * LANES]
    logits = jnp.dot(xn, wr_ref[...], preferred_element_type=F32,
                     precision=lax.Precision.HIGHEST) + br_ref[...]
    lane_e = lax.broadcasted_iota(jnp.int32, logits.shape, 1).astype(F32)
    lane_o = lax.broadcasted_iota(jnp.int32, (logits.shape[0], LANES), 1)
    idx_out = jnp.zeros((logits.shape[0], LANES), F32)
    val_out = jnp.zeros((logits.shape[0], LANES), F32)
    vals = []
    member = jnp.zeros(logits.shape, F32)
    for kk in range(TOP_K):
        m = jnp.max(logits, axis=-1, keepdims=True)
        sel = jnp.min(jnp.where(logits == m, lane_e, float(N_EXPERTS)), axis=-1, keepdims=True)
        chosen = lane_e == sel
        member = jnp.where(chosen, 1.0, member)
        logits = jnp.where(chosen, -jnp.inf, logits)
        idx_out = jnp.where(lane_o == kk, sel, idx_out)
        vals.append(m)

    @pl.when(i == 0)
    def _():
        cnt_ref[...] = jnp.zeros_like(cnt_ref)

    cnt_ref[...] += jnp.sum(member, axis=0, keepdims=True)
    es = [jnp.exp(v - vals[0]) for v in vals]
    den = es[0] + es[1] + es[2] + es[3]
    for kk in range(TOP_K):
        val_out = jnp.where(lane_o == kk, es[kk] / den, val_out)
    idx_ref[...] = idx_out.astype(jnp.int32)
    gate_ref[...] = val_out


def _router(cfg, a_p, a_s, w_o, res, g, mods, w_r, b_r):
    t, npt, seq_tiles = cfg["T"], cfg["NPT"], cfg["SEQ_TILES"]
    k = w_o.shape[0]
    ga_specs, ga_args = _mod_specs(*mods, 2, npt, seq_tiles)
    sh_specs, sh_args = _mod_specs(*mods, 3, npt, seq_tiles)
    sc_specs, sc_args = _mod_specs(*mods, 4, npt, seq_tiles)
    return pl.pallas_call(
        functools.partial(_router_kernel, npt=npt),
        grid=(t // TM,),
        in_specs=[pl.BlockSpec((TM, k), lambda i: (jnp.minimum(i, npt - 1), 0)),
                  pl.BlockSpec((TM, k), lambda i: (0, 0)),
                  pl.BlockSpec((k, D_MODEL), lambda i: (0, 0)),
                  pl.BlockSpec((TM, D_MODEL), lambda i: (i, 0)),
                  *ga_specs,
                  pl.BlockSpec((1, D_MODEL), lambda i: (0, 0)),
                  *sh_specs, *sc_specs,
                  pl.BlockSpec((D_MODEL, N_EXPERTS), lambda i: (0, 0)),
                  pl.BlockSpec((1, N_EXPERTS), lambda i: (0, 0))],
        out_specs=[pl.BlockSpec((TM, D_MODEL), lambda i: (i, 0)),
                   pl.BlockSpec((TM * ROW_TILE, LANES), lambda i: (i, 0)),
                   pl.BlockSpec((TM, LANES), lambda i: (i, 0)),
                   pl.BlockSpec((TM, LANES), lambda i: (i, 0)),
                   pl.BlockSpec((1, N_EXPERTS), lambda i: (0, 0))],
        out_shape=[jax.ShapeDtypeStruct((t, D_MODEL), F32),
                   jax.ShapeDtypeStruct((t * ROW_TILE, LANES), F32),
                   jax.ShapeDtypeStruct((t, LANES), jnp.int32),
                   jax.ShapeDtypeStruct((t, LANES), F32),
                   jax.ShapeDtypeStruct((1, N_EXPERTS), F32)],
        compiler_params=_cparams(("arbitrary",)),
        name="moe_router",
    )(a_p, a_s, w_o, res, *ga_args, g.reshape(1, D_MODEL), *sh_args, *sc_args, w_r,
      b_r.reshape(1, N_EXPERTS))


IDX_ALIGN = 128
IDX_WIN = MOE_ROWS + IDX_ALIGN
IDX_SLOT = 512
IDX_SLOTS = 4


def _expert_kernel(be_ref, bo_ref, bv_ref, tok_hbm, dst_hbm, x_hbm, wgu_ref, bgu_ref, wdn_ref, bdn_ref,
                   y_hbm, tok_smem, dst_smem, xbuf, ybuf, wgu_bf, wdn_bf, isem, gsem, ssem,
                   *, n_blocks, n_slots):
    i = pl.program_id(0)
    slot = i & 1
    last = n_blocks - 1
    blk_rows = MOE_ROWS * ROW_TILE
    spill0 = n_slots * ROW_TILE

    def idx_copies(blk):
        s4 = blk & (IDX_SLOTS - 1)
        off = bo_ref[jnp.minimum(blk, last)]
        src = pl.ds(pl.multiple_of(off & -IDX_ALIGN, IDX_ALIGN), IDX_WIN)
        dst = pl.ds(pl.multiple_of(s4 * IDX_SLOT, IDX_ALIGN), IDX_WIN)
        return (pltpu.make_async_copy(tok_hbm.at[src], tok_smem.at[dst], isem.at[0, s4]),
                pltpu.make_async_copy(dst_hbm.at[src], dst_smem.at[dst], isem.at[1, s4]))

    def idx_base(blk):
        return (blk & (IDX_SLOTS - 1)) * IDX_SLOT + (bo_ref[jnp.minimum(blk, last)] & (IDX_ALIGN - 1))

    def issue_gather(blk, buf):
        base = idx_base(blk)
        for r in range(MOE_ROWS):
            row = pl.multiple_of(tok_smem[base + r], ROW_TILE)
            pltpu.make_async_copy(x_hbm.at[pl.ds(row, ROW_TILE)], xbuf.at[buf, pl.ds(r * ROW_TILE, ROW_TILE)],
                                  gsem.at[buf]).start(priority=r % 2)

    def wait_gather(buf):
        pltpu.make_async_copy(x_hbm.at[pl.ds(0, blk_rows)], xbuf.at[buf], gsem.at[buf]).wait()

    def wait_scatter(buf):
        pltpu.make_async_copy(ybuf.at[buf], y_hbm.at[pl.ds(0, blk_rows)], ssem.at[buf]).wait()

    @pl.when(i == 0)
    def _():
        for c in idx_copies(0):
            c.start()
        for c in idx_copies(0):
            c.wait()
        issue_gather(0, 0)
        for c in idx_copies(1):
            c.start()
        ybuf[...] = jnp.zeros_like(ybuf)
        for b in range(2):
            pltpu.make_async_copy(ybuf.at[b], y_hbm.at[pl.ds(spill0 + b * blk_rows, blk_rows)],
                                  ssem.at[b]).start()

    @pl.when((i == 0) | (be_ref[i] != be_ref[jnp.maximum(i - 1, 0)]))
    def _():
        wgu_bf[...] = wgu_ref[0].astype(BF16)
        wdn_bf[...] = wdn_ref[0].astype(BF16)

    def run(buf):
        for c in idx_copies(i + 1):
            c.wait()
        wait_gather(buf)
        wait_scatter(buf)

        issue_gather(i + 1, 1 - buf)
        for c in idx_copies(i + 2):
            c.start()

        xb = jnp.concatenate([xbuf[buf, pl.ds(c, MOE_ROWS, stride=ROW_TILE), :] for c in range(ROW_TILE)],
                             axis=1).astype(BF16)
        hgu = _dot(xb, wgu_bf[...]) + bgu_ref[0]
        gate = jnp.minimum(hgu[:, :D_EXPERT], SWIGLU_LIMIT)
        up = jnp.clip(hgu[:, D_EXPERT:], -SWIGLU_LIMIT, SWIGLU_LIMIT)
        act = (up + 1.0) * gate * jax.nn.sigmoid(SWIGLU_ALPHA * gate)
        y = _dot(act.astype(BF16), wdn_bf[...]) + bdn_ref[0]
        for c in range(ROW_TILE):
            ybuf[buf, pl.ds(c, MOE_ROWS, stride=ROW_TILE), :] = y[:, c * LANES:(c + 1) * LANES]

        n_valid = bv_ref[i]
        base = idx_base(i)
        spill = spill0 + buf * blk_rows
        for r in range(MOE_ROWS):
            row = pl.multiple_of(jnp.where(r < n_valid, dst_smem[base + r], spill + r * ROW_TILE), ROW_TILE)
            pltpu.make_async_copy(ybuf.at[buf, pl.ds(r * ROW_TILE, ROW_TILE)], y_hbm.at[pl.ds(row, ROW_TILE)],
                                  ssem.at[buf]).start(priority=r % 2)

    for b in range(2):
        pl.when(slot == b)(functools.partial(run, b))

    @pl.when(i == last)
    def _():
        wait_gather(1 - slot)
        for c in idx_copies(i + 2):
            c.wait()
        wait_scatter(1 - slot)
        wait_scatter(slot)


def _experts(cfg, xn_rows, tables, layer, w_gu, b_gu, w_dn, b_dn):
    t = cfg["T"]
    blk_exp, blk_off, blk_valid, tok_rows, dst_rows = tables
    n_blocks = blk_exp.shape[0]
    n_slots = t * TOP_K
    any_spec = pl.BlockSpec(memory_space=pl.ANY)
    e0 = layer * N_EXPERTS
    w_gu = w_gu.reshape(-1, D_MODEL, 2 * D_EXPERT)
    w_dn = w_dn.reshape(-1, D_EXPERT, D_MODEL)
    b_gu = b_gu.reshape(-1, 1, 2 * D_EXPERT)
    b_dn = b_dn.reshape(-1, 1, D_MODEL)
    grid_spec = pltpu.PrefetchScalarGridSpec(
        num_scalar_prefetch=3,
        grid=(n_blocks,),
        in_specs=[any_spec, any_spec, any_spec,
                  pl.BlockSpec((1, D_MODEL, 2 * D_EXPERT), lambda i, be, bo, bv: (e0 + be[i], 0, 0)),
                  pl.BlockSpec((1, 1, 2 * D_EXPERT), lambda i, be, bo, bv: (e0 + be[i], 0, 0)),
                  pl.BlockSpec((1, D_EXPERT, D_MODEL), lambda i, be, bo, bv: (e0 + be[i], 0, 0)),
                  pl.BlockSpec((1, 1, D_MODEL), lambda i, be, bo, bv: (e0 + be[i], 0, 0))],
        out_specs=any_spec,
        scratch_shapes=[pltpu.SMEM((IDX_SLOTS * IDX_SLOT,), jnp.int32),
                        pltpu.SMEM((IDX_SLOTS * IDX_SLOT,), jnp.int32),
                        pltpu.VMEM((2, MOE_ROWS * ROW_TILE, LANES), F32),
                        pltpu.VMEM((2, MOE_ROWS * ROW_TILE, LANES), F32),
                        pltpu.VMEM((D_MODEL, 2 * D_EXPERT), BF16),
                        pltpu.VMEM((D_EXPERT, D_MODEL), BF16),
                        pltpu.SemaphoreType.DMA((2, IDX_SLOTS)),
                        pltpu.SemaphoreType.DMA((2,)),
                        pltpu.SemaphoreType.DMA((2,))],
    )
    return pl.pallas_call(
        functools.partial(_expert_kernel, n_blocks=n_blocks, n_slots=n_slots),
        grid_spec=grid_spec,
        out_shape=jax.ShapeDtypeStruct(((n_slots + 2 * MOE_ROWS) * ROW_TILE, LANES), F32),
        compiler_params=_cparams(("arbitrary",)),
        name="moe_experts",
    )(blk_exp, blk_off, blk_valid, tok_rows, dst_rows, xn_rows, w_gu, b_gu, w_dn, b_dn)


def _combine_kernel(h_ref, y0, y1, y2, y3, gates_ref, gp, gs, o_ref, *, npt):
    i = pl.program_id(0)
    gates = gates_ref[...]
    gmod = _pick(i < npt, gp, gs)
    for c in range(ROW_TILE):
        sl = slice(c * LANES, (c + 1) * LANES)
        rows = pl.ds(c, TM, stride=ROW_TILE)
        y = gates[:, 0:1] * y0[rows, :]
        for kk, y_ref in ((1, y1), (2, y2), (3, y3)):
            y = y + gates[:, kk:kk + 1] * y_ref[rows, :]
        o_ref[:, sl] = h_ref[:, sl] + gmod[:, sl] * y


def _combine(cfg, h, y_rows, gates, gate_mods):
    t, npt, seq_tiles = cfg["T"], cfg["NPT"], cfg["SEQ_TILES"]
    g_specs, g_args = _mod_specs(*gate_mods, 5, npt, seq_tiles)
    nt = t // TM
    y_specs = [pl.BlockSpec((TM * ROW_TILE, LANES), functools.partial(lambda i, kk: (kk * nt + i, 0), kk=kk))
               for kk in range(TOP_K)]
    return pl.pallas_call(
        functools.partial(_combine_kernel, npt=npt),
        grid=(nt,),
        in_specs=[pl.BlockSpec((TM, D_MODEL), lambda i: (i, 0)),
                  *y_specs,
                  pl.BlockSpec((TM, LANES), lambda i: (i, 0)),
                  *g_specs],
        out_specs=pl.BlockSpec((TM, D_MODEL), lambda i: (i, 0)),
        out_shape=jax.ShapeDtypeStruct((t, D_MODEL), F32),
        compiler_params=_cparams(("arbitrary",)),
        name="moe_combine",
    )(h, y_rows, y_rows, y_rows, y_rows, gates, *g_args)


def _dispatch_tables(cfg, top_i, counts):
    t = cfg["T"]
    n_asg = t * TOP_K
    bits = (n_asg - 1).bit_length()
    flat = jnp.arange(n_asg, dtype=jnp.int32).reshape(t, TOP_K)
    keys = jnp.sort(((top_i << bits) | flat).reshape(-1))
    asg = keys & ((1 << bits) - 1)
    tok = asg // TOP_K
    pad = jnp.zeros((2 * MOE_ROWS,), jnp.int32)
    tok_rows = jnp.concatenate([tok * ROW_TILE, pad])
    dst_rows = jnp.concatenate([((asg % TOP_K) * t + tok) * ROW_TILE, pad])
    n_blocks = -(-n_asg // MOE_ROWS) + N_EXPERTS
    nblk = (counts + MOE_ROWS - 1) // MOE_ROWS
    blk_end = jnp.cumsum(nblk)
    grp_start = jnp.cumsum(counts) - counts
    b = jnp.arange(n_blocks, dtype=jnp.int32)
    blk_exp = jnp.minimum(jnp.sum((blk_end[None, :] <= b[:, None]).astype(jnp.int32), axis=1), N_EXPERTS - 1)
    within = (b - (blk_end - nblk)[blk_exp]) * MOE_ROWS
    blk_off = jnp.minimum(grp_start[blk_exp] + within, n_asg).astype(jnp.int32)
    blk_valid = jnp.clip(counts[blk_exp] - within, 0, MOE_ROWS).astype(jnp.int32)
    return blk_exp, blk_off, blk_valid, tok_rows, dst_rows


def _combine_final_kernel(h_ref, y0, y1, y2, y3, gates_ref, gp, gs, g_ref, shp, shs, scp, scs,
                          op_ref, os_ref, h_scr, *, npt):
    i = pl.program_id(0)
    _combine_kernel(h_ref, y0, y1, y2, y3, gates_ref, gp, gs, h_scr, npt=npt)
    y = _norm_mod(i, npt, h_scr, g_ref, shp, shs, scp, scs)

    @pl.when(i < npt)
    def _():
        op_ref[...] = y

    @pl.when(i >= npt)
    def _():
        os_ref[...] = y


def _combine_final(cfg, h, y_rows, gates, gate_mods, g, mods):
    t, npt, seq_tiles = cfg["T"], cfg["NPT"], cfg["SEQ_TILES"]
    g_specs, g_args = _mod_specs(*gate_mods, 5, npt, seq_tiles)
    sh_specs, sh_args = _mod_specs(*mods, 0, npt, seq_tiles)
    sc_specs, sc_args = _mod_specs(*mods, 1, npt, seq_tiles)
    nt = t // TM
    y_specs = [pl.BlockSpec((TM * ROW_TILE, LANES), functools.partial(lambda i, kk: (kk * nt + i, 0), kk=kk))
               for kk in range(TOP_K)]
    return pl.pallas_call(
        functools.partial(_combine_final_kernel, npt=npt),
        grid=(nt,),
        in_specs=[pl.BlockSpec((TM, D_MODEL), lambda i: (i, 0)),
                  *y_specs,
                  pl.BlockSpec((TM, LANES), lambda i: (i, 0)),
                  *g_specs,
                  pl.BlockSpec((1, D_MODEL), lambda i: (0, 0)),
                  *sh_specs, *sc_specs],
        out_specs=[pl.BlockSpec((TM, D_MODEL), lambda i: (jnp.minimum(i, npt - 1), 0)),
                   pl.BlockSpec((TM, D_MODEL), lambda i: (0, 0))],
        out_shape=[jax.ShapeDtypeStruct((npt * TM, D_MODEL), F32),
                   jax.ShapeDtypeStruct((TM, D_MODEL), F32)],
        scratch_shapes=[pltpu.VMEM((TM, D_MODEL), F32)],
        compiler_params=_cparams(("arbitrary",)),
        name="moe_combine_final",
    )(h, y_rows, y_rows, y_rows, y_rows, gates, *g_args, g.reshape(1, D_MODEL), *sh_args, *sc_args)


def _moe(cfg, a_p, a_s, w_o, res, g, mods, w_r, b_r, layer, w_gu, b_gu, w_dn, b_dn, final=None):
    h, xn, idx, gates, counts = _router(cfg, a_p, a_s, w_o, res, g, mods, w_r, b_r)
    tables = _dispatch_tables(cfg, idx[:, :TOP_K], counts[0].astype(jnp.int32))
    yk = _experts(cfg, xn, tables, layer, w_gu, b_gu, w_dn, b_dn)
    if final is None:
        return _combine(cfg, h, yk, gates, mods)
    return _combine_final(cfg, h, yk, gates, mods, *final)


def _rope_tables(pos, group):
    half = group // 2
    inv = ROPE_BASE ** (-jnp.arange(half, dtype=F32) / half)
    ang = pos.astype(F32)[:, None] * inv[None, :]
    cos = jnp.cos(ang)
    sin = jnp.sin(ang)
    reps = LANES // group
    cos_t = jnp.tile(jnp.concatenate([cos, cos], axis=-1), (1, reps))
    sin_t = jnp.tile(jnp.concatenate([-sin, sin], axis=-1), (1, reps))
    return cos_t, sin_t


def _split_mods(cfg, ada):
    b, ds = cfg["B"], cfg["DS"]
    return ada[:b].reshape(b, 1, ada.shape[1]), jnp.repeat(ada[b:], ds, axis=0)


def kernel(x_prompt, x_sample, c_prompt, c_sample, state_ret, cache_ckv, cache_krope, page_table,
           w_ada, b_ada, g_mix, g_ffn, ret_w_in, ret_gn, ret_w_o,
           w_ada_kv, b_ada_kv, g_kv_in, mla_w_kv_a, g_ckv, mla_w_uk, mla_w_uv,
           mla_w_dq, g_cq, mla_w_uq, mla_w_o,
           w_router, b_router, w_gu, b_gu, w_dn, b_dn,
           w_ada_f, b_ada_f, g_final):
    b, s, d = x_prompt.shape
    db, ds, _ = x_sample.shape
    assert d == D_MODEL and db * ds == TM and s % TM == 0 and s % ATT_T == 0
    npr = b * s
    cfg = dict(B=b, S=s, DB=db, DS=ds, NP=npr, T=npr + TM, NPT=npr // TM, SEQ_TILES=s // TM)
    past_len = page_table.shape[1] * cache_ckv.shape[1]

    h = jnp.concatenate([x_prompt.reshape(npr, d), x_sample.reshape(TM, d)], axis=0)
    c_all = jnp.concatenate([c_prompt, c_sample], axis=0)

    pos = jnp.concatenate([jnp.arange(s, dtype=jnp.int32),
                           jnp.tile(past_len + jnp.arange(ds, dtype=jnp.int32), db)])
    cos_r, sin_r = _rope_tables(pos, RET_DK)
    k_scale = RET_DK ** -0.5
    ret_cos = jnp.stack([cos_r, cos_r * k_scale])
    ret_sin = jnp.stack([sin_r, sin_r * k_scale])
    mla_cos, mla_sin = _rope_tables(pos, QK_ROPE)

    mods_kv = _split_mods(cfg, _adaln(c_all, w_ada_kv, b_ada_kv))
    mods_f = _split_mods(cfg, _adaln(c_all, w_ada_f, b_ada_f))

    n_a = state_ret.shape[0]
    depth = w_ada.shape[0]
    ret_p, ret_s = [], []
    outs_kv = None
    for l in range(depth):
        mods = _split_mods(cfg, _adaln(c_all, w_ada, b_ada, layer=l))
        mm = (mods, mods)
        if l < n_a:
            z = _ret_inproj(cfg, h, g_mix[l], mm, ret_w_in[l].astype(BF16), ret_cos, ret_sin)
            a_p, st_p = _ret_prompt(cfg, z, ret_gn[l])
            a_s, st_s = _ret_sample(cfg, z, ret_gn[l], state_ret[l])
            ret_p.append(st_p)
            ret_s.append(st_s)
            w_o = ret_w_o[l].astype(BF16)
        else:
            bl = l - n_a
            if bl == 0:
                wa_pad = jnp.pad(mla_w_kv_a, ((0, 0), (0, KV_A_PAD - mla_w_kv_a.shape[1]))).astype(BF16)
                w_ukv = jnp.concatenate([mla_w_uk.reshape(KV_LORA, -1), mla_w_uv.reshape(KV_LORA, -1)],
                                        axis=1).astype(BF16)
                ckv_p, ckv_s, kr_p_t, kr_s, k_cat, v_all = _mla_kv(
                    cfg, h, g_kv_in, (mods_kv, mods_kv), wa_pad, g_ckv, mla_cos, mla_sin, w_ukv)
                outs_kv = (ckv_p, ckv_s, kr_p_t, kr_s)
                ckn = jnp.pad(ckv_s.reshape(db, ds, KV_LORA), ((0, 0), (0, NEW_PAD - ds), (0, 0)))
                krn = jnp.pad(kr_s.reshape(db, ds, QK_ROPE), ((0, 0), (0, NEW_PAD - ds), (0, 0)))
                w_uk_t = jnp.transpose(mla_w_uk, (1, 2, 0)).astype(BF16)
                w_uv_h = jnp.transpose(mla_w_uv, (1, 0, 2)).astype(BF16)
            cq = _mla_dq(cfg, h, g_mix[l], mm, mla_w_dq[bl].astype(BF16), g_cq[bl])
            w_uq = mla_w_uq[bl].reshape(Q_LORA, MLA_HEADS, QK_HEAD)
            w_uq_perm = jnp.concatenate([w_uq[:, :, :QK_NOPE].reshape(Q_LORA, -1),
                                         w_uq[:, :, QK_NOPE:].reshape(Q_LORA, -1)], axis=1).astype(BF16)
            q_cat = _mla_uq(cfg, cq, w_uq_perm, mla_cos, mla_sin)
            o_p = _flash(cfg, q_cat, k_cat, v_all)
            q_s = q_cat[:, npr:, :]
            ql = _head_bmm(q_s[:, :, :QK_NOPE], w_uk_t)
            to_seq = lambda x: x.reshape(MLA_HEADS, db, ds, x.shape[-1]).transpose(1, 0, 2, 3).reshape(
                db, MLA_HEADS * ds, x.shape[-1])
            ctx = _decode(cfg, page_table, to_seq(ql), to_seq(q_s[:, :, QK_NOPE:]), ckn, krn,
                          cache_ckv, jnp.swapaxes(cache_krope, 1, 2))
            ctx_h = ctx.reshape(db, MLA_HEADS, ds, KV_LORA).transpose(1, 0, 2, 3).reshape(MLA_HEADS, TM, KV_LORA)
            a_s = _head_bmm(ctx_h, w_uv_h).transpose(1, 0, 2).reshape(TM, MLA_HEADS * V_HEAD)
            a_p = o_p
            w_o = mla_w_o[bl].astype(BF16)
        final = (g_final, mods_f) if l == depth - 1 else None
        h = _moe(cfg, a_p, a_s, w_o, h, g_ffn[l], mods, w_router[l], b_router[l],
                 l, w_gu, b_gu, w_dn, b_dn, final=final)

    y_p, y_s = h
    ckv_p, ckv_s, kr_p_t, kr_s = outs_kv
    return (y_p.reshape(b, s, d), y_s.reshape(db, ds, d),
            jnp.stack(ret_p, axis=0), jnp.stack(ret_s, axis=0),
            ckv_p.reshape(b, s, KV_LORA), jnp.swapaxes(kr_p_t, 1, 2),
            ckv_s.reshape(db, ds, KV_LORA), kr_s.reshape(db, ds, QK_ROPE))
```

```python
import functools
import math

import jax
import jax.numpy as jnp
from jax import lax
from jax.experimental import pallas as pl
from jax.experimental.pallas import tpu as pltpu

F32 = jnp.float32
BF16 = jnp.bfloat16

D_MODEL = 1024
RET_HEADS = 8
RET_DK = 128
RET_DV = 256
RET_CHUNK = 256
MLA_HEADS = 8
QK_NOPE = 128
QK_ROPE = 64
QK_HEAD = QK_NOPE + QK_ROPE
V_HEAD = 128
KV_LORA = 256
Q_LORA = 768
SM_SCALE = QK_HEAD ** -0.5
N_EXPERTS = 32
TOP_K = 4
D_EXPERT = 1024
SWIGLU_LIMIT = 7.0
SWIGLU_ALPHA = 1.702
ROPE_BASE = 10000.0
EPS = 1e-6
NEG_INF = -1e30

LANES = 128
ROW_TILE = D_MODEL // LANES
TM = 512
MOE_ROWS = 256
ATT_T = 512
LOG2_E = math.log2(math.e)
VMEM_LIMIT = 56 * 1024 * 1024

LOG_DECAY = tuple(math.log1p(-2.0 ** (-5.0 - h)) for h in range(RET_HEADS))


def _cparams(sem):
    return pltpu.CompilerParams(dimension_semantics=sem, vmem_limit_bytes=VMEM_LIMIT)


def _rms(x, g):
    return x * lax.rsqrt(jnp.mean(x * x, axis=-1, keepdims=True) + EPS) * g


def _dot(a, b):
    return jnp.dot(a, b, preferred_element_type=F32)


def _dot_nt(a, b):
    return lax.dot_general(a, b, (((1,), (1,)), ((), ())), preferred_element_type=F32)


def _dot_tn(a, b):
    return lax.dot_general(a, b, (((0,), (0,)), ((), ())), preferred_element_type=F32)


def _rope_group(x, cos, sin_signed, group):
    half = group // 2
    lane = lax.broadcasted_iota(jnp.int32, x.shape, 1)
    x_up = pltpu.roll(x, LANES - half, 1)
    x_dn = pltpu.roll(x, half, 1)
    rot = jnp.where((lane & (group - 1)) < half, x_up, x_dn)
    return x * cos + rot * sin_signed


def _pick(is_prompt, p_ref, s_ref):
    return jnp.where(is_prompt, p_ref[0], s_ref[...])


def _mod_specs(mod_p, mod_s, chunk, npt, seq_tiles):
    dm = D_MODEL
    sp = pl.BlockSpec((1, 1, dm), lambda i, *_: (jnp.minimum(i, npt - 1) // seq_tiles, 0, chunk))
    ss = pl.BlockSpec((TM, dm), lambda i, *_: (0, chunk))
    return [sp, ss], [mod_p, mod_s]


def _norm_mod(i, npt, x_ref, g_ref, shp, shs, scp, scs):
    is_p = i < npt
    xn = _rms(x_ref[...], g_ref[...])
    return xn * (1.0 + _pick(is_p, scp, scs)) + _pick(is_p, shp, shs)


def _adaln_kernel(c_ref, w_ref, b_ref, o_ref):
    c = c_ref[...]
    a = (c * jax.nn.sigmoid(c)).astype(BF16)
    o_ref[...] = _dot(a, w_ref[...].astype(BF16)) + b_ref[...]


def _adaln(c_all, w, b, layer=None):
    n = w.shape[-1]
    tn = 512
    rows = c_all.shape[0]
    if layer is None:
        w_spec = pl.BlockSpec((D_MODEL, tn), lambda j: (0, j))
        b_spec = pl.BlockSpec((1, tn), lambda j: (0, j))
        b = b.reshape(1, n)
    else:
        w_spec = pl.BlockSpec((None, D_MODEL, tn), lambda j: (layer, 0, j))
        b_spec = pl.BlockSpec((None, 1, tn), lambda j: (layer, 0, j))
        b = b.reshape(b.shape[0], 1, n)
    return pl.pallas_call(
        _adaln_kernel,
        grid=(n // tn,),
        in_specs=[pl.BlockSpec((rows, D_MODEL), lambda j: (0, 0)), w_spec, b_spec],
        out_specs=pl.BlockSpec((rows, tn), lambda j: (0, j)),
        out_shape=jax.ShapeDtypeStruct((rows, n), F32),
        compiler_params=_cparams(("arbitrary",)),
        name="adaln",
    )(c_all, w, b)


def _inproj_kernel(x_ref, g_ref, shp, shs, scp, scs, w_ref, cos_ref, sin_ref, o_ref, xn_scr,
                   *, npt, tn, n_rope):
    i = pl.program_id(0)
    j = pl.program_id(1)

    @pl.when(j == 0)
    def _():
        xn_scr[...] = _norm_mod(i, npt, x_ref, g_ref, shp, shs, scp, scs).astype(BF16)

    acc = _dot(xn_scr[...], w_ref[...])

    @pl.when(j < n_rope)
    def _():
        cos = cos_ref[0]
        sin = sin_ref[0]
        for c in range(tn // LANES):
            sl = slice(c * LANES, (c + 1) * LANES)
            o_ref[:, sl] = _rope_group(acc[:, sl], cos, sin, RET_DK).astype(BF16)

    @pl.when(j >= n_rope)
    def _():
        o_ref[...] = acc.astype(BF16)


def _ret_inproj(cfg, h, g, mods, w, cos_tab, sin_tab):
    t, npt, seq_tiles = cfg["T"], cfg["NPT"], cfg["SEQ_TILES"]
    n = w.shape[1]
    tn = 1024
    n_rope = (2 * RET_HEADS * RET_DK) // tn
    (shp, shs), (scp, scs) = mods
    sh_specs, sh_args = _mod_specs(shp, shs, 0, npt, seq_tiles)
    sc_specs, sc_args = _mod_specs(scp, scs, 1, npt, seq_tiles)
    tab_spec = pl.BlockSpec(
        (1, TM, LANES),
        lambda i, j: (jnp.minimum(j, n_rope - 1), jnp.where(i < npt, i % seq_tiles, seq_tiles), 0))
    return pl.pallas_call(
        functools.partial(_inproj_kernel, npt=npt, tn=tn, n_rope=n_rope),
        grid=(t // TM, n // tn),
        in_specs=[pl.BlockSpec((TM, D_MODEL), lambda i, j: (i, 0)),
                  pl.BlockSpec((1, D_MODEL), lambda i, j: (0, 0)),
                  *sh_specs, *sc_specs,
                  pl.BlockSpec((D_MODEL, tn), lambda i, j: (0, j)),
                  tab_spec, tab_spec],
        out_specs=pl.BlockSpec((TM, tn), lambda i, j: (i, j)),
        out_shape=jax.ShapeDtypeStruct((t, n), BF16),
        scratch_shapes=[pltpu.VMEM((TM, D_MODEL), BF16)],
        compiler_params=_cparams(("arbitrary", "arbitrary")),
        name="ret_inproj",
    )(h, g.reshape(1, D_MODEL), *sh_args, *sc_args, w, cos_tab, sin_tab)


def _group_norm_gate(o, gn_row, g_bf16):
    mu = jnp.mean(o, axis=-1, keepdims=True)
    var = jnp.mean(jnp.square(o - mu), axis=-1, keepdims=True)
    on = (o - mu) * lax.rsqrt(var + EPS) * gn_row
    gg = g_bf16.astype(F32)
    return (gg * jax.nn.sigmoid(gg) * on).astype(BF16)


def _ret_prompt_kernel(q_ref, k_ref, v_ref, g_ref, gn_ref, a_ref, st_ref):
    c = pl.program_id(1)
    ch = RET_CHUNK

    @pl.when(c == 0)
    def _():
        st_ref[...] = jnp.zeros_like(st_ref)

    row = lax.broadcasted_iota(jnp.int32, (ch, ch), 0)
    col = lax.broadcasted_iota(jnp.int32, (ch, ch), 1)
    causal = row >= col
    diff = jnp.where(causal, (row - col).astype(F32), 0.0)
    ridx = lax.broadcasted_iota(jnp.int32, (ch, 1), 0).astype(F32)
    for h in range(RET_HEADS):
        lg = LOG_DECAY[h]
        ks = slice(h * RET_DK, (h + 1) * RET_DK)
        vs = slice(h * RET_DV, (h + 1) * RET_DV)
        q = q_ref[:, ks]
        k = k_ref[:, ks]
        v = v_ref[:, vs]
        st = st_ref[0, h]
        decay = jnp.where(causal, jnp.exp(diff * lg), 0.0)
        s = _dot_nt(q, k) * decay
        o = _dot(s.astype(BF16), v) + jnp.exp((ridx + 1.0) * lg) * _dot(q, st.astype(BF16))
        kd = (k.astype(F32) * jnp.exp((ch - 1.0 - ridx) * lg)).astype(BF16)
        st_ref[0, h] = math.exp(ch * lg) * st + _dot_tn(kd, v)
        a_ref[:, vs] = _group_norm_gate(o, gn_ref[h:h + 1, :], g_ref[:, vs])


def _ret_prompt(cfg, z, gn):
    b, s = cfg["B"], cfg["S"]
    nc = s // RET_CHUNK
    hk = RET_HEADS * RET_DK
    hv = RET_HEADS * RET_DV
    return pl.pallas_call(
        _ret_prompt_kernel,
        grid=(b, nc),
        in_specs=[pl.BlockSpec((RET_CHUNK, hk), lambda bi, c: (bi * nc + c, 0)),
                  pl.BlockSpec((RET_CHUNK, hk), lambda bi, c: (bi * nc + c, 1)),
                  pl.BlockSpec((RET_CHUNK, hv), lambda bi, c: (bi * nc + c, 1)),
                  pl.BlockSpec((RET_CHUNK, hv), lambda bi, c: (bi * nc + c, 2)),
                  pl.BlockSpec((RET_HEADS, RET_DV), lambda bi, c: (0, 0))],
        out_specs=[pl.BlockSpec((RET_CHUNK, hv), lambda bi, c: (bi * nc + c, 0)),
                   pl.BlockSpec((1, RET_HEADS, RET_DK, RET_DV), lambda bi, c: (bi, 0, 0, 0))],
        out_shape=[jax.ShapeDtypeStruct((b * s, hv), BF16),
                   jax.ShapeDtypeStruct((b, RET_HEADS, RET_DK, RET_DV), F32)],
        compiler_params=_cparams(("arbitrary", "arbitrary")),
        name="ret_prompt",
    )(z, z, z, z, gn)


RS_SEQ = 4


def _ret_sample_kernel(q_ref, k_ref, v_ref, g_ref, gn_ref, st_in_ref, a_ref, st_out_ref, *, ds):
    rows = RS_SEQ * ds
    shift = ds.bit_length() - 1
    row = lax.broadcasted_iota(jnp.int32, (rows, rows), 0)
    col = lax.broadcasted_iota(jnp.int32, (rows, rows), 1)
    causal = jnp.where(row >= col, (row >> shift) - (col >> shift), -1) == 0
    diff = jnp.where(causal, (row - col).astype(F32), 0.0)
    r1 = lax.broadcasted_iota(jnp.int32, (rows, 1), 0)
    t_idx = (r1 & (ds - 1)).astype(F32)
    seq_of_row = r1 >> shift
    for h in range(RET_HEADS):
        lg = LOG_DECAY[h]
        ks = slice(h * RET_DK, (h + 1) * RET_DK)
        vs = slice(h * RET_DV, (h + 1) * RET_DV)
        q = q_ref[:, ks]
        k = k_ref[:, ks]
        v = v_ref[:, vs]
        decay = jnp.where(causal, jnp.exp(diff * lg), 0.0)
        s = _dot_nt(q, k) * decay
        o = _dot(s.astype(BF16), v)
        qd = jnp.exp((t_idx + 1.0) * lg)
        kd = k.astype(F32) * jnp.exp((ds - 1.0 - t_idx) * lg)
        for sq in range(RS_SEQ):
            st = st_in_ref[sq, h]
            mine = seq_of_row == sq
            o = o + jnp.where(mine, qd * _dot(q, st.astype(BF16)), 0.0)
            kds = jnp.where(mine, kd, 0.0).astype(BF16)
            st_out_ref[sq, h] = math.exp(ds * lg) * st + _dot_tn(kds, v)
        a_ref[:, vs] = _group_norm_gate(o, gn_ref[h:h + 1, :], g_ref[:, vs])


def _ret_sample(cfg, z, gn, state):
    db, ds, npr = cfg["DB"], cfg["DS"], cfg["NP"]
    rows = RS_SEQ * ds
    assert ds & (ds - 1) == 0 and db % RS_SEQ == 0 and npr % rows == 0
    base = npr // rows
    hk = RET_HEADS * RET_DK
    hv = RET_HEADS * RET_DV
    st_spec = pl.BlockSpec((RS_SEQ, RET_HEADS, RET_DK, RET_DV), lambda i: (i, 0, 0, 0))
    return pl.pallas_call(
        functools.partial(_ret_sample_kernel, ds=ds),
        grid=(db // RS_SEQ,),
        in_specs=[pl.BlockSpec((rows, hk), lambda i: (base + i, 0)),
                  pl.BlockSpec((rows, hk), lambda i: (base + i, 1)),
                  pl.BlockSpec((rows, hv), lambda i: (base + i, 1)),
                  pl.BlockSpec((rows, hv), lambda i: (base + i, 2)),
                  pl.BlockSpec((RET_HEADS, RET_DV), lambda i: (0, 0)),
                  st_spec],
        out_specs=[pl.BlockSpec((rows, hv), lambda i: (i, 0)), st_spec],
        out_shape=[jax.ShapeDtypeStruct((db * ds, hv), BF16),
                   jax.ShapeDtypeStruct((db, RET_HEADS, RET_DK, RET_DV), F32)],
        compiler_params=_cparams(("arbitrary",)),
        name="ret_sample",
    )(z, z, z, z, gn, state)


KV_A_PAD = KV_LORA + LANES


def _kv_kernel(x_ref, g_ref, shp, shs, scp, scs, wa_ref, gck_ref, cos_ref, sin_ref, wukv_ref,
               ckvp_ref, ckvs_ref, krp_ref, krs_ref, kcat_ref, v_ref, *, npt):
    i = pl.program_id(0)
    xn = _norm_mod(i, npt, x_ref, g_ref, shp, shs, scp, scs).astype(BF16)
    z = _dot(xn, wa_ref[...])
    ckv = _rms(z[:, :KV_LORA], gck_ref[...])
    kr_lanes = _rope_group(z[:, KV_LORA:], cos_ref[...], sin_ref[...], QK_ROPE)
    kr = kr_lanes[:, :QK_ROPE]

    @pl.when(i < npt)
    def _():
        ckvp_ref[...] = ckv
        krp_ref[0] = kr_lanes.T[:QK_ROPE, :]

    @pl.when(i >= npt)
    def _():
        ckvs_ref[...] = ckv
        krs_ref[...] = kr

    up = _dot(ckv.astype(BF16), wukv_ref[...])
    kr16 = kr.astype(BF16)
    for h in range(MLA_HEADS):
        kcat_ref[h, :, :QK_NOPE] = up[:, h * QK_NOPE:(h + 1) * QK_NOPE].astype(BF16)
        kcat_ref[h, :, QK_NOPE:] = kr16
    v_ref[...] = up[:, MLA_HEADS * QK_NOPE:].astype(BF16)


def _mla_kv(cfg, h, g, mods, wa_pad, g_ckv, cos_tab, sin_tab, w_ukv):
    t, npt, seq_tiles = cfg["T"], cfg["NPT"], cfg["SEQ_TILES"]
    (shp, shs), (scp, scs) = mods
    sh_specs, sh_args = _mod_specs(shp, shs, 0, npt, seq_tiles)
    sc_specs, sc_args = _mod_specs(scp, scs, 1, npt, seq_tiles)
    tab_spec = pl.BlockSpec((TM, LANES), lambda i: (jnp.where(i < npt, i % seq_tiles, seq_tiles), 0))
    n_up = w_ukv.shape[1]
    return pl.pallas_call(
        functools.partial(_kv_kernel, npt=npt),
        grid=(t // TM,),
        in_specs=[pl.BlockSpec((TM, D_MODEL), lambda i: (i, 0)),
                  pl.BlockSpec((1, D_MODEL), lambda i: (0, 0)),
                  *sh_specs, *sc_specs,
                  pl.BlockSpec((D_MODEL, KV_A_PAD), lambda i: (0, 0)),
                  pl.BlockSpec((1, KV_LORA), lambda i: (0, 0)),
                  tab_spec, tab_spec,
                  pl.BlockSpec((KV_LORA, n_up), lambda i: (0, 0))],
        out_specs=[pl.BlockSpec((TM, KV_LORA), lambda i: (jnp.minimum(i, npt - 1), 0)),
                   pl.BlockSpec((TM, KV_LORA), lambda i: (0, 0)),
                   pl.BlockSpec((1, QK_ROPE, TM),
                                lambda i: (jnp.minimum(i, npt - 1) // seq_tiles, 0, jnp.minimum(i, npt - 1) % seq_tiles)),
                   pl.BlockSpec((TM, QK_ROPE), lambda i: (0, 0)),
                   pl.BlockSpec((MLA_HEADS, TM, QK_HEAD), lambda i: (0, i, 0)),
                   pl.BlockSpec((TM, MLA_HEADS * V_HEAD), lambda i: (i, 0))],
        out_shape=[jax.ShapeDtypeStruct((npt * TM, KV_LORA), F32),
                   jax.ShapeDtypeStruct((TM, KV_LORA), F32),
                   jax.ShapeDtypeStruct((npt // seq_tiles, QK_ROPE, seq_tiles * TM), F32),
                   jax.ShapeDtypeStruct((TM, QK_ROPE), F32),
                   jax.ShapeDtypeStruct((MLA_HEADS, t, QK_HEAD), BF16),
                   jax.ShapeDtypeStruct((t, MLA_HEADS * V_HEAD), BF16)],
        compiler_params=_cparams(("arbitrary",)),
        name="mla_kv",
    )(h, g.reshape(1, D_MODEL), *sh_args, *sc_args, wa_pad, g_ckv.reshape(1, KV_LORA),
      cos_tab, sin_tab, w_ukv)


def _dq_kernel(x_ref, g_ref, shp, shs, scp, scs, w_ref, gcq_ref, o_ref, *, npt):
    i = pl.program_id(0)
    xn = _norm_mod(i, npt, x_ref, g_ref, shp, shs, scp, scs).astype(BF16)
    o_ref[...] = _rms(_dot(xn, w_ref[...]), gcq_ref[...]).astype(BF16)


def _mla_dq(cfg, h, g, mods, w_dq, g_cq):
    t, npt, seq_tiles = cfg["T"], cfg["NPT"], cfg["SEQ_TILES"]
    (shp, shs), (scp, scs) = mods
    sh_specs, sh_args = _mod_specs(shp, shs, 0, npt, seq_tiles)
    sc_specs, sc_args = _mod_specs(scp, scs, 1, npt, seq_tiles)
    return pl.pallas_call(
        functools.partial(_dq_kernel, npt=npt),
        grid=(t // TM,),
        in_specs=[pl.BlockSpec((TM, D_MODEL), lambda i: (i, 0)),
                  pl.BlockSpec((1, D_MODEL), lambda i: (0, 0)),
                  *sh_specs, *sc_specs,
                  pl.BlockSpec((D_MODEL, Q_LORA), lambda i: (0, 0)),
                  pl.BlockSpec((1, Q_LORA), lambda i: (0, 0))],
        out_specs=pl.BlockSpec((TM, Q_LORA), lambda i: (i, 0)),
        out_shape=jax.ShapeDtypeStruct((t, Q_LORA), BF16),
        compiler_params=_cparams(("arbitrary",)),
        name="mla_dq",
    )(h, g.reshape(1, D_MODEL), *sh_args, *sc_args, w_dq, g_cq.reshape(1, Q_LORA))


def _uq_kernel(cq_ref, w_ref, cos_ref, sin_ref, o_ref):
    z = _dot(cq_ref[...], w_ref[...])
    n_nope = MLA_HEADS * QK_NOPE
    for h in range(MLA_HEADS):
        o_ref[h, :, :QK_NOPE] = z[:, h * QK_NOPE:(h + 1) * QK_NOPE].astype(BF16)
    cos = cos_ref[...]
    sin = sin_ref[...]
    heads_per_group = LANES // QK_ROPE
    for c in range(MLA_HEADS // heads_per_group):
        r = _rope_group(z[:, n_nope + c * LANES:n_nope + (c + 1) * LANES], cos, sin, QK_ROPE).astype(BF16)
        for u in range(heads_per_group):
            o_ref[c * heads_per_group + u, :, QK_NOPE:] = r[:, u * QK_ROPE:(u + 1) * QK_ROPE]


def _mla_uq(cfg, cq, w_uq_perm, cos_tab, sin_tab):
    t, npt, seq_tiles = cfg["T"], cfg["NPT"], cfg["SEQ_TILES"]
    n = w_uq_perm.shape[1]
    tab_spec = pl.BlockSpec((TM, LANES), lambda i: (jnp.where(i < npt, i % seq_tiles, seq_tiles), 0))
    return pl.pallas_call(
        _uq_kernel,
        grid=(t // TM,),
        in_specs=[pl.BlockSpec((TM, Q_LORA), lambda i: (i, 0)),
                  pl.BlockSpec((Q_LORA, n), lambda i: (0, 0)),
                  tab_spec, tab_spec],
        out_specs=pl.BlockSpec((MLA_HEADS, TM, QK_HEAD), lambda i: (0, i, 0)),
        out_shape=jax.ShapeDtypeStruct((MLA_HEADS, t, QK_HEAD), BF16),
        compiler_params=_cparams(("arbitrary",)),
        name="mla_uq",
    )(cq, w_uq_perm, cos_tab, sin_tab)


def _flash_kernel(q_ref, k_ref, v_ref, o_ref):
    qi = pl.program_id(2)
    q = q_ref[0]
    tq = q.shape[0]

    def step(ki, carry, masked):
        m, l, acc = carry
        start = pl.multiple_of(ki * ATT_T, ATT_T)
        k = k_ref[0, pl.ds(start, ATT_T), :]
        v = v_ref[pl.ds(start, ATT_T), :]
        s = _dot_nt(q, k) * (SM_SCALE * LOG2_E)
        if masked:
            row = lax.broadcasted_iota(jnp.int32, s.shape, 0)
            col = lax.broadcasted_iota(jnp.int32, s.shape, 1)
            s = jnp.where(col <= row, s, NEG_INF)
        m_new = jnp.maximum(m, jnp.max(s, axis=-1, keepdims=True))
        alpha = jnp.exp2(m - m_new)
        p = jnp.exp2(s - m_new)
        l = alpha * l + jnp.sum(p, axis=-1, keepdims=True)
        acc = alpha * acc + _dot(p.astype(BF16), v)
        return m_new, l, acc

    carry = (jnp.full((tq, 1), NEG_INF, F32), jnp.zeros((tq, 1), F32), jnp.zeros((tq, V_HEAD), F32))
    carry = lax.fori_loop(0, qi // 2, lambda j, c: step(2 * j + 1, step(2 * j, c, False), False), carry)
    carry = lax.cond(qi % 2 == 1, lambda c: step(qi - 1, c, False), lambda c: c, carry)
    _, l, acc = step(qi, carry, True)
    o_ref[...] = (acc / l).astype(BF16)


def _flash(cfg, q_cat, k_cat, v):
    b, s, npr = cfg["B"], cfg["S"], cfg["NP"]
    nq = s // ATT_T
    return pl.pallas_call(
        _flash_kernel,
        grid=(b, MLA_HEADS, nq),
        in_specs=[pl.BlockSpec((1, ATT_T, QK_HEAD), lambda bi, h, qi: (h, bi * nq + qi, 0)),
                  pl.BlockSpec((1, s, QK_HEAD), lambda bi, h, qi: (h, bi, 0)),
                  pl.BlockSpec((s, V_HEAD), lambda bi, h, qi: (bi, h))],
        out_specs=pl.BlockSpec((ATT_T, V_HEAD), lambda bi, h, qi: (bi * nq + qi, h)),
        out_shape=jax.ShapeDtypeStruct((npr, MLA_HEADS * V_HEAD), BF16),
        compiler_params=_cparams(("arbitrary", "arbitrary", "arbitrary")),
        name="mla_flash",
    )(q_cat, k_cat, v)


def _bmm_kernel(x_ref, w_ref, o_ref):
    o_ref[0] = _dot(x_ref[0], w_ref[0]).astype(o_ref.dtype)


def _head_bmm(x, w):
    hh, m, k = x.shape
    n = w.shape[2]
    return pl.pallas_call(
        _bmm_kernel,
        grid=(hh,),
        in_specs=[pl.BlockSpec((1, m, k), lambda h: (h, 0, 0)),
                  pl.BlockSpec((1, k, n), lambda h: (h, 0, 0))],
        out_specs=pl.BlockSpec((1, m, n), lambda h: (h, 0, 0)),
        out_shape=jax.ShapeDtypeStruct((hh, m, n), BF16),
        compiler_params=_cparams(("arbitrary",)),
        name="head_bmm",
    )(x, w)


NEW_PAD = 16


DEC_PAGES = 16
DEC_SEQS = 2


def _decode_kernel(pt_ref, ql_ref, qr_ref, ckn_ref, krn_ref, ck_hbm, kr_hbm, o_ref,
                   ckbuf, krbuf, sem, *, n_pages, npg, page, ds, n_steps):
    g = pl.program_id(0)
    nch = n_pages // npg
    nk = npg * page

    def start(step, c, slot):
        for u in range(DEC_SEQS):
            for p in range(npg):
                pg = pt_ref[step * DEC_SEQS + u, c * npg + p]
                pltpu.make_async_copy(ck_hbm.at[pg], ckbuf.at[slot, u, pl.ds(p * page, page)],
                                      sem.at[0, slot]).start()
                pltpu.make_async_copy(kr_hbm.at[pg], krbuf.at[slot, u, :, pl.ds(p * page, page)],
                                      sem.at[1, slot]).start()

    def wait(slot):
        pltpu.make_async_copy(ckbuf.at[slot], ckbuf.at[slot], sem.at[0, slot]).wait()
        pltpu.make_async_copy(krbuf.at[slot], krbuf.at[slot], sem.at[1, slot]).wait()

    qls = [ql_ref[u] for u in range(DEC_SEQS)]
    qrs = [qr_ref[u] for u in range(DEC_SEQS)]
    nq = qls[0].shape[0]

    def update(carry, s, vals):
        m, l, acc = carry
        m_new = jnp.maximum(m, jnp.max(s, axis=-1, keepdims=True))
        alpha = jnp.exp(m - m_new)
        p = jnp.exp(s - m_new)
        l = alpha * l + jnp.sum(p, axis=-1, keepdims=True)
        acc = alpha * acc + _dot(p.astype(BF16), vals)
        return m_new, l, acc

    @pl.when(g == 0)
    def _():
        start(0, 0, 0)

    def body(c, carry):
        slot = (g * nch + c) & 1

        @pl.when(c + 1 < nch)
        def _():
            start(g, c + 1, 1 - slot)

        @pl.when((c + 1 == nch) & (g + 1 < n_steps))
        def _():
            start(g + 1, 0, 1 - slot)

        wait(slot)
        out = []
        for u in range(DEC_SEQS):
            ck = ckbuf[slot, u].astype(BF16)
            kr_t = krbuf[slot, u].astype(BF16)
            s = (_dot_nt(qls[u], ck) + _dot(qrs[u], kr_t)) * SM_SCALE
            out.append(update(carry[u], s, ck))
        return tuple(out)

    init = tuple((jnp.full((nq, 1), NEG_INF, F32), jnp.zeros((nq, 1), F32), jnp.zeros((nq, KV_LORA), F32))
                 for _ in range(DEC_SEQS))
    carry = lax.fori_loop(0, nch, body, init)
    for u in range(DEC_SEQS):
        ckn = ckn_ref[u].astype(BF16)
        krn = krn_ref[u].astype(BF16)
        s = (_dot_nt(qls[u], ckn) + _dot_nt(qrs[u], krn)) * SM_SCALE
        row_t = lax.broadcasted_iota(jnp.int32, s.shape, 0) & (ds - 1)
        col = lax.broadcasted_iota(jnp.int32, s.shape, 1)
        s = jnp.where(col <= row_t, s, NEG_INF)
        _, l, acc = update(carry[u], s, ckn)
        o_ref[u] = (acc / l).astype(BF16)


def _decode(cfg, page_table, ql, qr, ckn, krn, cache_ckv, cache_krope_t):
    db, ds = cfg["DB"], cfg["DS"]
    n_pages = page_table.shape[1]
    page = cache_ckv.shape[1]
    npg = math.gcd(n_pages, DEC_PAGES)
    nq = ql.shape[1]
    assert db % DEC_SEQS == 0
    grid_spec = pltpu.PrefetchScalarGridSpec(
        num_scalar_prefetch=1,
        grid=(db // DEC_SEQS,),
        in_specs=[pl.BlockSpec((DEC_SEQS, nq, KV_LORA), lambda b, pt: (b, 0, 0)),
                  pl.BlockSpec((DEC_SEQS, nq, QK_ROPE), lambda b, pt: (b, 0, 0)),
                  pl.BlockSpec((DEC_SEQS, NEW_PAD, KV_LORA), lambda b, pt: (b, 0, 0)),
                  pl.BlockSpec((DEC_SEQS, NEW_PAD, QK_ROPE), lambda b, pt: (b, 0, 0)),
                  pl.BlockSpec(memory_space=pl.ANY),
                  pl.BlockSpec(memory_space=pl.ANY)],
        out_specs=pl.BlockSpec((DEC_SEQS, nq, KV_LORA), lambda b, pt: (b, 0, 0)),
        scratch_shapes=[pltpu.VMEM((2, DEC_SEQS, npg * page, KV_LORA), F32),
                        pltpu.VMEM((2, DEC_SEQS, QK_ROPE, npg * page), F32),
                        pltpu.SemaphoreType.DMA((2, 2))],
    )
    return pl.pallas_call(
        functools.partial(_decode_kernel, n_pages=n_pages, npg=npg, page=page, ds=ds,
                          n_steps=db // DEC_SEQS),
        grid_spec=grid_spec,
        out_shape=jax.ShapeDtypeStruct((db, nq, KV_LORA), BF16),
        compiler_params=_cparams(("arbitrary",)),
        name="mla_decode",
    )(page_table, ql, qr, ckn, krn, cache_ckv, cache_krope_t)


def _router_kernel(ap_ref, as_ref, wo_ref, res_ref, gp, gs, g_ref, shp, shs, scp, scs, wr_ref, br_ref,
                   h_ref, xn_ref, idx_ref, gate_ref, cnt_ref, *, npt):
    i = pl.program_id(0)
    gate = _pick(i < npt, gp, gs)

    def out_proj(a_ref):
        h_ref[...] = res_ref[...] + gate * _dot(a_ref[...], wo_ref[...])

    pl.when(i < npt)(functools.partial(out_proj, ap_ref))
    pl.when(i >= npt)(functools.partial(out_proj, as_ref))
    xn = _norm_mod(i, npt, h_ref, g_ref, shp, shs, scp, scs)
    for c in range(ROW_TILE):
        xn_ref[pl.ds(c, TM, stride=ROW_TILE), :] = xn[:, c * LANES:(c + 1) * LANES]
    w_r = wr_ref[...]
    x_hi = xn.astype(BF16)
    x_lo = (xn - x_hi.astype(F32)).astype(BF16)
    w_hi = w_r.astype(BF16)
    w_lo = (w_r - w_hi.astype(F32)).astype(BF16)
    logits = _dot(x_hi, w_hi) + (_dot(x_lo, w_hi) + _dot(x_hi, w_lo)) + br_ref[...]
    lane_e = lax.broadcasted_iota(jnp.int32, logits.shape, 1).astype(F32)
    lane_o = lax.broadcasted_iota(jnp.int32, (logits.shape[0], LANES), 1)
    idx_out = jnp.zeros((logits.shape[0], LANES), F32)
    val_out = jnp.zeros((logits.shape[0], LANES), F32)
    vals = []
    member = jnp.zeros(logits.shape, F32)
    for kk in range(TOP_K):
        m = jnp.max(logits, axis=-1, keepdims=True)
        sel = jnp.min(jnp.where(logits == m, lane_e, float(N_EXPERTS)), axis=-1, keepdims=True)
        chosen = lane_e == sel
        member = jnp.where(chosen, 1.0, member)
        logits = jnp.where(chosen, -jnp.inf, logits)
        idx_out = jnp.where(lane_o == kk, sel, idx_out)
        vals.append(m)

    @pl.when(i == 0)
    def _():
        cnt_ref[...] = jnp.zeros_like(cnt_ref)

    cnt_ref[...] += jnp.sum(member, axis=0, keepdims=True)
    es = [jnp.exp(v - vals[0]) for v in vals]
    den = es[0] + es[1] + es[2] + es[3]
    for kk in range(TOP_K):
        val_out = jnp.where(lane_o == kk, es[kk] / den, val_out)
    idx_ref[...] = idx_out.astype(jnp.int32)
    gate_ref[...] = val_out


def _router(cfg, a_p, a_s, w_o, res, g, mods, w_r, b_r):
    t, npt, seq_tiles = cfg["T"], cfg["NPT"], cfg["SEQ_TILES"]
    k = w_o.shape[0]
    ga_specs, ga_args = _mod_specs(*mods, 2, npt, seq_tiles)
    sh_specs, sh_args = _mod_specs(*mods, 3, npt, seq_tiles)
    sc_specs, sc_args = _mod_specs(*mods, 4, npt, seq_tiles)
    return pl.pallas_call(
        functools.partial(_router_kernel, npt=npt),
        grid=(t // TM,),
        in_specs=[pl.BlockSpec((TM, k), lambda i: (jnp.minimum(i, npt - 1), 0)),
                  pl.BlockSpec((TM, k), lambda i: (0, 0)),
                  pl.BlockSpec((k, D_MODEL), lambda i: (0, 0)),
                  pl.BlockSpec((TM, D_MODEL), lambda i: (i, 0)),
                  *ga_specs,
                  pl.BlockSpec((1, D_MODEL), lambda i: (0, 0)),
                  *sh_specs, *sc_specs,
                  pl.BlockSpec((D_MODEL, N_EXPERTS), lambda i: (0, 0)),
                  pl.BlockSpec((1, N_EXPERTS), lambda i: (0, 0))],
        out_specs=[pl.BlockSpec((TM, D_MODEL), lambda i: (i, 0)),
                   pl.BlockSpec((TM * ROW_TILE, LANES), lambda i: (i, 0)),
                   pl.BlockSpec((TM, LANES), lambda i: (i, 0)),
                   pl.BlockSpec((TM, LANES), lambda i: (i, 0)),
                   pl.BlockSpec((1, N_EXPERTS), lambda i: (0, 0))],
        out_shape=[jax.ShapeDtypeStruct((t, D_MODEL), F32),
                   jax.ShapeDtypeStruct((t * ROW_TILE, LANES), F32),
                   jax.ShapeDtypeStruct((t, LANES), jnp.int32),
                   jax.ShapeDtypeStruct((t, LANES), F32),
                   jax.ShapeDtypeStruct((1, N_EXPERTS), F32)],
        compiler_params=_cparams(("arbitrary",)),
        name="moe_router",
    )(a_p, a_s, w_o, res, *ga_args, g.reshape(1, D_MODEL), *sh_args, *sc_args, w_r,
      b_r.reshape(1, N_EXPERTS))


IDX_ALIGN = 128
IDX_WIN = MOE_ROWS + IDX_ALIGN
IDX_SLOT = 512
IDX_SLOTS = 4


def _expert_kernel(be_ref, bo_ref, bv_ref, tok_hbm, dst_hbm, x_hbm, wgu_ref, bgu_ref, wdn_ref, bdn_ref,
                   y_hbm, tok_smem, dst_smem, xbuf, ybuf, wgu_bf, wdn_bf, isem, gsem, ssem,
                   *, n_blocks, n_slots):
    i = pl.program_id(0)
    slot = i & 1
    last = n_blocks - 1
    blk_rows = MOE_ROWS * ROW_TILE
    spill0 = n_slots * ROW_TILE

    def idx_copies(blk):
        s4 = blk & (IDX_SLOTS - 1)
        off = bo_ref[jnp.minimum(blk, last)]
        src = pl.ds(pl.multiple_of(off & -IDX_ALIGN, IDX_ALIGN), IDX_WIN)
        dst = pl.ds(pl.multiple_of(s4 * IDX_SLOT, IDX_ALIGN), IDX_WIN)
        return (pltpu.make_async_copy(tok_hbm.at[src], tok_smem.at[dst], isem.at[0, s4]),
                pltpu.make_async_copy(dst_hbm.at[src], dst_smem.at[dst], isem.at[1, s4]))

    def idx_base(blk):
        return (blk & (IDX_SLOTS - 1)) * IDX_SLOT + (bo_ref[jnp.minimum(blk, last)] & (IDX_ALIGN - 1))

    def issue_gather(blk, buf):
        base = idx_base(blk)
        for r in range(MOE_ROWS):
            row = pl.multiple_of(tok_smem[base + r], ROW_TILE)
            pltpu.make_async_copy(x_hbm.at[pl.ds(row, ROW_TILE)], xbuf.at[buf, pl.ds(r * ROW_TILE, ROW_TILE)],
                                  gsem.at[buf]).start(priority=r % 2)

    def wait_gather(buf):
        pltpu.make_async_copy(x_hbm.at[pl.ds(0, blk_rows)], xbuf.at[buf], gsem.at[buf]).wait()

    def wait_scatter(buf):
        pltpu.make_async_copy(ybuf.at[buf], y_hbm.at[pl.ds(0, blk_rows)], ssem.at[buf]).wait()

    @pl.when(i == 0)
    def _():
        for c in idx_copies(0):
            c.start()
        for c in idx_copies(0):
            c.wait()
        issue_gather(0, 0)
        for c in idx_copies(1):
            c.start()
        ybuf[...] = jnp.zeros_like(ybuf)
        for b in range(2):
            pltpu.make_async_copy(ybuf.at[b], y_hbm.at[pl.ds(spill0 + b * blk_rows, blk_rows)],
                                  ssem.at[b]).start()

    @pl.when((i == 0) | (be_ref[i] != be_ref[jnp.maximum(i - 1, 0)]))
    def _():
        wgu_bf[...] = wgu_ref[0].astype(BF16)
        wdn_bf[...] = wdn_ref[0].astype(BF16)

    def run(buf):
        for c in idx_copies(i + 1):
            c.wait()
        wait_gather(buf)
        wait_scatter(buf)

        issue_gather(i + 1, 1 - buf)
        for c in idx_copies(i + 2):
            c.start()

        xb = jnp.concatenate([xbuf[buf, pl.ds(c, MOE_ROWS, stride=ROW_TILE), :] for c in range(ROW_TILE)],
                             axis=1).astype(BF16)
        hgu = _dot(xb, wgu_bf[...]) + bgu_ref[0]
        gate = jnp.minimum(hgu[:, :D_EXPERT], SWIGLU_LIMIT)
        up = jnp.clip(hgu[:, D_EXPERT:], -SWIGLU_LIMIT, SWIGLU_LIMIT)
        act = (up + 1.0) * gate * jax.nn.sigmoid(SWIGLU_ALPHA * gate)
        y = _dot(act.astype(BF16), wdn_bf[...]) + bdn_ref[0]
        for c in range(ROW_TILE):
            ybuf[buf, pl.ds(c, MOE_ROWS, stride=ROW_TILE), :] = y[:, c * LANES:(c + 1) * LANES]

        n_valid = bv_ref[i]
        base = idx_base(i)
        spill = spill0 + buf * blk_rows
        for r in range(MOE_ROWS):
            row = pl.multiple_of(jnp.where(r < n_valid, dst_smem[base + r], spill + r * ROW_TILE), ROW_TILE)
            pltpu.make_async_copy(ybuf.at[buf, pl.ds(r * ROW_TILE, ROW_TILE)], y_hbm.at[pl.ds(row, ROW_TILE)],
                                  ssem.at[buf]).start(priority=r % 2)

    for b in range(2):
        pl.when(slot == b)(functools.partial(run, b))

    @pl.when(i == last)
    def _():
        wait_gather(1 - slot)
        for c in idx_copies(i + 2):
            c.wait()
        wait_scatter(1 - slot)
        wait_scatter(slot)


def _experts(cfg, xn_rows, tables, layer, w_gu, b_gu, w_dn, b_dn):
    t = cfg["T"]
    blk_exp, blk_off, blk_valid, tok_rows, dst_rows = tables
    n_blocks = blk_exp.shape[0]
    n_slots = t * TOP_K
    any_spec = pl.BlockSpec(memory_space=pl.ANY)
    e0 = layer * N_EXPERTS
    w_gu = w_gu.reshape(-1, D_MODEL, 2 * D_EXPERT)
    w_dn = w_dn.reshape(-1, D_EXPERT, D_MODEL)
    b_gu = b_gu.reshape(-1, 1, 2 * D_EXPERT)
    b_dn = b_dn.reshape(-1, 1, D_MODEL)
    grid_spec = pltpu.PrefetchScalarGridSpec(
        num_scalar_prefetch=3,
        grid=(n_blocks,),
        in_specs=[any_spec, any_spec, any_spec,
                  pl.BlockSpec((1, D_MODEL, 2 * D_EXPERT), lambda i, be, bo, bv: (e0 + be[i], 0, 0)),
                  pl.BlockSpec((1, 1, 2 * D_EXPERT), lambda i, be, bo, bv: (e0 + be[i], 0, 0)),
                  pl.BlockSpec((1, D_EXPERT, D_MODEL), lambda i, be, bo, bv: (e0 + be[i], 0, 0)),
                  pl.BlockSpec((1, 1, D_MODEL), lambda i, be, bo, bv: (e0 + be[i], 0, 0))],
        out_specs=any_spec,
        scratch_shapes=[pltpu.SMEM((IDX_SLOTS * IDX_SLOT,), jnp.int32),
                        pltpu.SMEM((IDX_SLOTS * IDX_SLOT,), jnp.int32),
                        pltpu.VMEM((2, MOE_ROWS * ROW_TILE, LANES), F32),
                        pltpu.VMEM((2, MOE_ROWS * ROW_TILE, LANES), F32),
                        pltpu.VMEM((D_MODEL, 2 * D_EXPERT), BF16),
                        pltpu.VMEM((D_EXPERT, D_MODEL), BF16),
                        pltpu.SemaphoreType.DMA((2, IDX_SLOTS)),
                        pltpu.SemaphoreType.DMA((2,)),
                        pltpu.SemaphoreType.DMA((2,))],
    )
    return pl.pallas_call(
        functools.partial(_expert_kernel, n_blocks=n_blocks, n_slots=n_slots),
        grid_spec=grid_spec,
        out_shape=jax.ShapeDtypeStruct(((n_slots + 2 * MOE_ROWS) * ROW_TILE, LANES), F32),
        compiler_params=_cparams(("arbitrary",)),
        name="moe_experts",
    )(blk_exp, blk_off, blk_valid, tok_rows, dst_rows, xn_rows, w_gu, b_gu, w_dn, b_dn)


def _combine_kernel(h_ref, y0, y1, y2, y3, gates_ref, gp, gs, o_ref, *, npt):
    i = pl.program_id(0)
    gates = gates_ref[...]
    gmod = _pick(i < npt, gp, gs)
    for c in range(ROW_TILE):
        sl = slice(c * LANES, (c + 1) * LANES)
        rows = pl.ds(c, TM, stride=ROW_TILE)
        y = gates[:, 0:1] * y0[rows, :]
        for kk, y_ref in ((1, y1), (2, y2), (3, y3)):
            y = y + gates[:, kk:kk + 1] * y_ref[rows, :]
        o_ref[:, sl] = h_ref[:, sl] + gmod[:, sl] * y


def _combine(cfg, h, y_rows, gates, gate_mods):
    t, npt, seq_tiles = cfg["T"], cfg["NPT"], cfg["SEQ_TILES"]
    g_specs, g_args = _mod_specs(*gate_mods, 5, npt, seq_tiles)
    nt = t // TM
    y_specs = [pl.BlockSpec((TM * ROW_TILE, LANES), functools.partial(lambda i, kk: (kk * nt + i, 0), kk=kk))
               for kk in range(TOP_K)]
    return pl.pallas_call(
        functools.partial(_combine_kernel, npt=npt),
        grid=(nt,),
        in_specs=[pl.BlockSpec((TM, D_MODEL), lambda i: (i, 0)),
                  *y_specs,
                  pl.BlockSpec((TM, LANES), lambda i: (i, 0)),
                  *g_specs],
        out_specs=pl.BlockSpec((TM, D_MODEL), lambda i: (i, 0)),
        out_shape=jax.ShapeDtypeStruct((t, D_MODEL), F32),
        compiler_params=_cparams(("arbitrary",)),
        name="moe_combine",
    )(h, y_rows, y_rows, y_rows, y_rows, gates, *g_args)


def _dispatch_tables(cfg, top_i, counts):
    t = cfg["T"]
    n_asg = t * TOP_K
    bits = (n_asg - 1).bit_length()
    flat = jnp.arange(n_asg, dtype=jnp.int32).reshape(t, TOP_K)
    keys = jnp.sort(((top_i << bits) | flat).reshape(-1))
    asg = keys & ((1 << bits) - 1)
    tok = asg // TOP_K
    pad = jnp.zeros((2 * MOE_ROWS,), jnp.int32)
    tok_rows = jnp.concatenate([tok * ROW_TILE, pad])
    dst_rows = jnp.concatenate([((asg % TOP_K) * t + tok) * ROW_TILE, pad])
    n_blocks = -(-n_asg // MOE_ROWS) + N_EXPERTS
    nblk = (counts + MOE_ROWS - 1) // MOE_ROWS
    blk_end = jnp.cumsum(nblk)
    grp_start = jnp.cumsum(counts) - counts
    b = jnp.arange(n_blocks, dtype=jnp.int32)
    blk_exp = jnp.minimum(jnp.sum((blk_end[None, :] <= b[:, None]).astype(jnp.int32), axis=1), N_EXPERTS - 1)
    within = (b - (blk_end - nblk)[blk_exp]) * MOE_ROWS
    blk_off = jnp.minimum(grp_start[blk_exp] + within, n_asg).astype(jnp.int32)
    blk_valid = jnp.clip(counts[blk_exp] - within, 0, MOE_ROWS).astype(jnp.int32)
    return blk_exp, blk_off, blk_valid, tok_rows, dst_rows


def _combine_final_kernel(h_ref, y0, y1, y2, y3, gates_ref, gp, gs, g_ref, shp, shs, scp, scs,
                          op_ref, os_ref, h_scr, *, npt):
    i = pl.program_id(0)
    _combine_kernel(h_ref, y0, y1, y2, y3, gates_ref, gp, gs, h_scr, npt=npt)
    y = _norm_mod(i, npt, h_scr, g_ref, shp, shs, scp, scs)

    @pl.when(i < npt)
    def _():
        op_ref[...] = y

    @pl.when(i >= npt)
    def _():
        os_ref[...] = y


def _combine_final(cfg, h, y_rows, gates, gate_mods, g, mods):
    t, npt, seq_tiles = cfg["T"], cfg["NPT"], cfg["SEQ_TILES"]
    g_specs, g_args = _mod_specs(*gate_mods, 5, npt, seq_tiles)
    sh_specs, sh_args = _mod_specs(*mods, 0, npt, seq_tiles)
    sc_specs, sc_args = _mod_specs(*mods, 1, npt, seq_tiles)
    nt = t // TM
    y_specs = [pl.BlockSpec((TM * ROW_TILE, LANES), functools.partial(lambda i, kk: (kk * nt + i, 0), kk=kk))
               for kk in range(TOP_K)]
    return pl.pallas_call(
        functools.partial(_combine_final_kernel, npt=npt),
        grid=(nt,),
        in_specs=[pl.BlockSpec((TM, D_MODEL), lambda i: (i, 0)),
                  *y_specs,
                  pl.BlockSpec((TM, LANES), lambda i: (i, 0)),
                  *g_specs,
                  pl.BlockSpec((1, D_MODEL), lambda i: (0, 0)),
                  *sh_specs, *sc_specs],
        out_specs=[pl.BlockSpec((TM, D_MODEL), lambda i: (jnp.minimum(i, npt - 1), 0)),
                   pl.BlockSpec((TM, D_MODEL), lambda i: (0, 0))],
        out_shape=[jax.ShapeDtypeStruct((npt * TM, D_MODEL), F32),
                   jax.ShapeDtypeStruct((TM, D_MODEL), F32)],
        scratch_shapes=[pltpu.VMEM((TM, D_MODEL), F32)],
        compiler_params=_cparams(("arbitrary",)),
        name="moe_combine_final",
    )(h, y_rows, y_rows, y_rows, y_rows, gates, *g_args, g.reshape(1, D_MODEL), *sh_args, *sc_args)


def _moe(cfg, a_p, a_s, w_o, res, g, mods, w_r, b_r, layer, w_gu, b_gu, w_dn, b_dn, final=None):
    h, xn, idx, gates, counts = _router(cfg, a_p, a_s, w_o, res, g, mods, w_r, b_r)
    tables = _dispatch_tables(cfg, idx[:, :TOP_K], counts[0].astype(jnp.int32))
    yk = _experts(cfg, xn, tables, layer, w_gu, b_gu, w_dn, b_dn)
    if final is None:
        return _combine(cfg, h, yk, gates, mods)
    return _combine_final(cfg, h, yk, gates, mods, *final)


def _rope_tables(pos, group):
    half = group // 2
    inv = ROPE_BASE ** (-jnp.arange(half, dtype=F32) / half)
    ang = pos.astype(F32)[:, None] * inv[None, :]
    cos = jnp.cos(ang)
    sin = jnp.sin(ang)
    reps = LANES // group
    cos_t = jnp.tile(jnp.concatenate([cos, cos], axis=-1), (1, reps))
    sin_t = jnp.tile(jnp.concatenate([-sin, sin], axis=-1), (1, reps))
    return cos_t, sin_t


def _split_mods(cfg, ada):
    b, ds = cfg["B"], cfg["DS"]
    return ada[:b].reshape(b, 1, ada.shape[1]), jnp.repeat(ada[b:], ds, axis=0)


def kernel(x_prompt, x_sample, c_prompt, c_sample, state_ret, cache_ckv, cache_krope, page_table,
           w_ada, b_ada, g_mix, g_ffn, ret_w_in, ret_gn, ret_w_o,
           w_ada_kv, b_ada_kv, g_kv_in, mla_w_kv_a, g_ckv, mla_w_uk, mla_w_uv,
           mla_w_dq, g_cq, mla_w_uq, mla_w_o,
           w_router, b_router, w_gu, b_gu, w_dn, b_dn,
           w_ada_f, b_ada_f, g_final):
    b, s, d = x_prompt.shape
    db, ds, _ = x_sample.shape
    assert d == D_MODEL and db * ds == TM and s % TM == 0 and s % ATT_T == 0
    npr = b * s
    cfg = dict(B=b, S=s, DB=db, DS=ds, NP=npr, T=npr + TM, NPT=npr // TM, SEQ_TILES=s // TM)
    past_len = page_table.shape[1] * cache_ckv.shape[1]

    h = jnp.concatenate([x_prompt.reshape(npr, d), x_sample.reshape(TM, d)], axis=0)
    c_all = jnp.concatenate([c_prompt, c_sample], axis=0)

    pos = jnp.concatenate([jnp.arange(s, dtype=jnp.int32),
                           jnp.tile(past_len + jnp.arange(ds, dtype=jnp.int32), db)])
    cos_r, sin_r = _rope_tables(pos, RET_DK)
    k_scale = RET_DK ** -0.5
    ret_cos = jnp.stack([cos_r, cos_r * k_scale])
    ret_sin = jnp.stack([sin_r, sin_r * k_scale])
    mla_cos, mla_sin = _rope_tables(pos, QK_ROPE)

    mods_kv = _split_mods(cfg, _adaln(c_all, w_ada_kv, b_ada_kv))
    mods_f = _split_mods(cfg, _adaln(c_all, w_ada_f, b_ada_f))

    n_a = state_ret.shape[0]
    depth = w_ada.shape[0]
    ret_p, ret_s = [], []
    outs_kv = None
    for l in range(depth):
        mods = _split_mods(cfg, _adaln(c_all, w_ada, b_ada, layer=l))
        mm = (mods, mods)
        if l < n_a:
            z = _ret_inproj(cfg, h, g_mix[l], mm, ret_w_in[l].astype(BF16), ret_cos, ret_sin)
            a_p, st_p = _ret_prompt(cfg, z, ret_gn[l])
            a_s, st_s = _ret_sample(cfg, z, ret_gn[l], state_ret[l])
            ret_p.append(st_p)
            ret_s.append(st_s)
            w_o = ret_w_o[l].astype(BF16)
        else:
            bl = l - n_a
            if bl == 0:
                wa_pad = jnp.pad(mla_w_kv_a, ((0, 0), (0, KV_A_PAD - mla_w_kv_a.shape[1]))).astype(BF16)
                w_ukv = jnp.concatenate([mla_w_uk.reshape(KV_LORA, -1), mla_w_uv.reshape(KV_LORA, -1)],
                                        axis=1).astype(BF16)
                ckv_p, ckv_s, kr_p_t, kr_s, k_cat, v_all = _mla_kv(
                    cfg, h, g_kv_in, (mods_kv, mods_kv), wa_pad, g_ckv, mla_cos, mla_sin, w_ukv)
                outs_kv = (ckv_p, ckv_s, kr_p_t, kr_s)
                ckn = jnp.pad(ckv_s.reshape(db, ds, KV_LORA), ((0, 0), (0, NEW_PAD - ds), (0, 0)))
                krn = jnp.pad(kr_s.reshape(db, ds, QK_ROPE), ((0, 0), (0, NEW_PAD - ds), (0, 0)))
                w_uk_t = jnp.transpose(mla_w_uk, (1, 2, 0)).astype(BF16)
                w_uv_h = jnp.transpose(mla_w_uv, (1, 0, 2)).astype(BF16)
            cq = _mla_dq(cfg, h, g_mix[l], mm, mla_w_dq[bl].astype(BF16), g_cq[bl])
            w_uq = mla_w_uq[bl].reshape(Q_LORA, MLA_HEADS, QK_HEAD)
            w_uq_perm = jnp.concatenate([w_uq[:, :, :QK_NOPE].reshape(Q_LORA, -1),
                                         w_uq[:, :, QK_NOPE:].reshape(Q_LORA, -1)], axis=1).astype(BF16)
            q_cat = _mla_uq(cfg, cq, w_uq_perm, mla_cos, mla_sin)
            o_p = _flash(cfg, q_cat, k_cat, v_all)
            q_s = q_cat[:, npr:, :]
            ql = _head_bmm(q_s[:, :, :QK_NOPE], w_uk_t)
            to_seq = lambda x: x.reshape(MLA_HEADS, db, ds, x.shape[-1]).transpose(1, 0, 2, 3).reshape(
                db, MLA_HEADS * ds, x.shape[-1])
            ctx = _decode(cfg, page_table, to_seq(ql), to_seq(q_s[:, :, QK_NOPE:]), ckn, krn,
                          cache_ckv, jnp.swapaxes(cache_krope, 1, 2))
            ctx_h = ctx.reshape(db, MLA_HEADS, ds, KV_LORA).transpose(1, 0, 2, 3).reshape(MLA_HEADS, TM, KV_LORA)
            a_s = _head_bmm(ctx_h, w_uv_h).transpose(1, 0, 2).reshape(TM, MLA_HEADS * V_HEAD)
            a_p = o_p
            w_o = mla_w_o[bl].astype(BF16)
        final = (g_final, mods_f) if l == depth - 1 else None
        h = _moe(cfg, a_p, a_s, w_o, h, g_ffn[l], mods, w_router[l], b_router[l],
                 l, w_gu, b_gu, w_dn, b_dn, final=final)

    y_p, y_s = h
    ckv_p, ckv_s, kr_p_t, kr_s = outs_kv
    return (y_p.reshape(b, s, d), y_s.reshape(db, ds, d),
            jnp.stack(ret_p, axis=0), jnp.stack(ret_s, axis=0),
            ckv_p.reshape(b, s, KV_LORA), jnp.swapaxes(kr_p_t, 1, 2),
            ckv_s.reshape(db, ds, KV_LORA), kr_s.reshape(db, ds, QK_ROPE))
```

```python
import functools
import math

import jax
import jax.numpy as jnp
from jax import lax
from jax.experimental import pallas as pl
from jax.experimental.pallas import tpu as pltpu

F32 = jnp.float32
BF16 = jnp.bfloat16

D_MODEL = 1024
RET_HEADS = 8
RET_DK = 128
RET_DV = 256
RET_CHUNK = 256
MLA_HEADS = 8
QK_NOPE = 128
QK_ROPE = 64
QK_HEAD = QK_NOPE + QK_ROPE
V_HEAD = 128
KV_LORA = 256
Q_LORA = 768
SM_SCALE = QK_HEAD ** -0.5
N_EXPERTS = 32
TOP_K = 4
D_EXPERT = 1024
SWIGLU_LIMIT = 7.0
SWIGLU_ALPHA = 1.702
ROPE_BASE = 10000.0
EPS = 1e-6
NEG_INF = -1e30

LANES = 128
ROW_TILE = D_MODEL // LANES
TM = 512
FFN_CHUNKS = 2
MOE_ROWS = 256
ATT_T = 512
LOG2_E = math.log2(math.e)
VMEM_LIMIT = 56 * 1024 * 1024

LOG_DECAY = tuple(math.log1p(-2.0 ** (-5.0 - h)) for h in range(RET_HEADS))


def _cparams(sem):
    return pltpu.CompilerParams(dimension_semantics=sem, vmem_limit_bytes=VMEM_LIMIT)


def _rms(x, g):
    return x * lax.rsqrt(jnp.mean(x * x, axis=-1, keepdims=True) + EPS) * g


def _dot(a, b):
    return jnp.dot(a, b, preferred_element_type=F32)


def _dot_nt(a, b):
    return lax.dot_general(a, b, (((1,), (1,)), ((), ())), preferred_element_type=F32)


def _dot_tn(a, b):
    return lax.dot_general(a, b, (((0,), (0,)), ((), ())), preferred_element_type=F32)


def _rope_group(x, cos, sin_signed, group):
    half = group // 2
    lane = lax.broadcasted_iota(jnp.int32, x.shape, 1)
    x_up = pltpu.roll(x, LANES - half, 1)
    x_dn = pltpu.roll(x, half, 1)
    rot = jnp.where((lane & (group - 1)) < half, x_up, x_dn)
    return x * cos + rot * sin_signed


def _pick(is_prompt, p_ref, s_ref):
    return jnp.where(is_prompt, p_ref[0], s_ref[...])


def _mod_specs(mod_p, mod_s, chunk, npt, seq_tiles):
    dm = D_MODEL
    sp = pl.BlockSpec((1, 1, dm), lambda i, *_: (jnp.minimum(i, npt - 1) // seq_tiles, 0, chunk))
    ss = pl.BlockSpec((TM, dm), lambda i, *_: (0, chunk))
    return [sp, ss], [mod_p, mod_s]


def _norm_mod(i, npt, x_ref, g_ref, shp, shs, scp, scs):
    is_p = i < npt
    xn = _rms(x_ref[...], g_ref[...])
    return xn * (1.0 + _pick(is_p, scp, scs)) + _pick(is_p, shp, shs)


def _adaln_kernel(c_ref, w_ref, b_ref, o_ref):
    c = c_ref[...]
    a = (c * jax.nn.sigmoid(c)).astype(BF16)
    o_ref[...] = _dot(a, w_ref[...].astype(BF16)) + b_ref[...]


def _adaln(c_all, w, b, layer=None):
    n = w.shape[-1]
    tn = 512
    rows = c_all.shape[0]
    if layer is None:
        w_spec = pl.BlockSpec((D_MODEL, tn), lambda j: (0, j))
        b_spec = pl.BlockSpec((1, tn), lambda j: (0, j))
        b = b.reshape(1, n)
    else:
        w_spec = pl.BlockSpec((None, D_MODEL, tn), lambda j: (layer, 0, j))
        b_spec = pl.BlockSpec((None, 1, tn), lambda j: (layer, 0, j))
        b = b.reshape(b.shape[0], 1, n)
    return pl.pallas_call(
        _adaln_kernel,
        grid=(n // tn,),
        in_specs=[pl.BlockSpec((rows, D_MODEL), lambda j: (0, 0)), w_spec, b_spec],
        out_specs=pl.BlockSpec((rows, tn), lambda j: (0, j)),
        out_shape=jax.ShapeDtypeStruct((rows, n), F32),
        compiler_params=_cparams(("arbitrary",)),
        name="adaln",
    )(c_all, w, b)


def _inproj_kernel(x_ref, g_ref, shp, shs, scp, scs, w_ref, cos_ref, sin_ref, o_ref, xn_scr,
                   *, npt, tn, n_rope):
    i = pl.program_id(0)
    j = pl.program_id(1)

    @pl.when(j == 0)
    def _():
        xn_scr[...] = _norm_mod(i, npt, x_ref, g_ref, shp, shs, scp, scs).astype(BF16)

    acc = _dot(xn_scr[...], w_ref[...])

    @pl.when(j < n_rope)
    def _():
        cos = cos_ref[0]
        sin = sin_ref[0]
        for c in range(tn // LANES):
            sl = slice(c * LANES, (c + 1) * LANES)
            o_ref[:, sl] = _rope_group(acc[:, sl], cos, sin, RET_DK).astype(BF16)

    @pl.when(j >= n_rope)
    def _():
        o_ref[...] = acc.astype(BF16)


def _ret_inproj(cfg, h, g, mods, w, cos_tab, sin_tab):
    t, npt, seq_tiles = cfg["T"], cfg["NPT"], cfg["SEQ_TILES"]
    n = w.shape[1]
    tn = 1024
    n_rope = (2 * RET_HEADS * RET_DK) // tn
    (shp, shs), (scp, scs) = mods
    sh_specs, sh_args = _mod_specs(shp, shs, 0, npt, seq_tiles)
    sc_specs, sc_args = _mod_specs(scp, scs, 1, npt, seq_tiles)
    tab_spec = pl.BlockSpec(
        (1, TM, LANES),
        lambda i, j: (jnp.minimum(j, n_rope - 1), jnp.where(i < npt, i % seq_tiles, seq_tiles), 0))
    return pl.pallas_call(
        functools.partial(_inproj_kernel, npt=npt, tn=tn, n_rope=n_rope),
        grid=(t // TM, n // tn),
        in_specs=[pl.BlockSpec((TM, D_MODEL), lambda i, j: (i, 0)),
                  pl.BlockSpec((1, D_MODEL), lambda i, j: (0, 0)),
                  *sh_specs, *sc_specs,
                  pl.BlockSpec((D_MODEL, tn), lambda i, j: (0, j)),
                  tab_spec, tab_spec],
        out_specs=pl.BlockSpec((TM, tn), lambda i, j: (i, j)),
        out_shape=jax.ShapeDtypeStruct((t, n), BF16),
        scratch_shapes=[pltpu.VMEM((TM, D_MODEL), BF16)],
        compiler_params=_cparams(("arbitrary", "arbitrary")),
        name="ret_inproj",
    )(h, g.reshape(1, D_MODEL), *sh_args, *sc_args, w, cos_tab, sin_tab)


def _group_norm_gate(o, gn_row, g_bf16):
    mu = jnp.mean(o, axis=-1, keepdims=True)
    var = jnp.mean(jnp.square(o - mu), axis=-1, keepdims=True)
    on = (o - mu) * lax.rsqrt(var + EPS) * gn_row
    gg = g_bf16.astype(F32)
    return (gg * jax.nn.sigmoid(gg) * on).astype(BF16)


def _ret_prompt_kernel(q_ref, k_ref, v_ref, g_ref, gn_ref, a_ref, st_ref):
    c = pl.program_id(1)
    ch = RET_CHUNK

    @pl.when(c == 0)
    def _():
        st_ref[...] = jnp.zeros_like(st_ref)

    row = lax.broadcasted_iota(jnp.int32, (ch, ch), 0)
    col = lax.broadcasted_iota(jnp.int32, (ch, ch), 1)
    causal = row >= col
    diff = jnp.where(causal, (row - col).astype(F32), 0.0)
    ridx = lax.broadcasted_iota(jnp.int32, (ch, 1), 0).astype(F32)
    for h in range(RET_HEADS):
        lg = LOG_DECAY[h]
        ks = slice(h * RET_DK, (h + 1) * RET_DK)
        vs = slice(h * RET_DV, (h + 1) * RET_DV)
        q = q_ref[:, ks]
        k = k_ref[:, ks]
        v = v_ref[:, vs]
        st = st_ref[0, h]
        decay = jnp.where(causal, jnp.exp(diff * lg), 0.0)
        s = _dot_nt(q, k) * decay
        o = _dot(s.astype(BF16), v) + jnp.exp((ridx + 1.0) * lg) * _dot(q, st.astype(BF16))
        kd = (k.astype(F32) * jnp.exp((ch - 1.0 - ridx) * lg)).astype(BF16)
        st_ref[0, h] = math.exp(ch * lg) * st + _dot_tn(kd, v)
        a_ref[:, vs] = _group_norm_gate(o, gn_ref[h:h + 1, :], g_ref[:, vs])


def _ret_prompt(cfg, z, gn):
    b, s = cfg["B"], cfg["S"]
    nc = s // RET_CHUNK
    hk = RET_HEADS * RET_DK
    hv = RET_HEADS * RET_DV
    return pl.pallas_call(
        _ret_prompt_kernel,
        grid=(b, nc),
        in_specs=[pl.BlockSpec((RET_CHUNK, hk), lambda bi, c: (bi * nc + c, 0)),
                  pl.BlockSpec((RET_CHUNK, hk), lambda bi, c: (bi * nc + c, 1)),
                  pl.BlockSpec((RET_CHUNK, hv), lambda bi, c: (bi * nc + c, 1)),
                  pl.BlockSpec((RET_CHUNK, hv), lambda bi, c: (bi * nc + c, 2)),
                  pl.BlockSpec((RET_HEADS, RET_DV), lambda bi, c: (0, 0))],
        out_specs=[pl.BlockSpec((RET_CHUNK, hv), lambda bi, c: (bi * nc + c, 0)),
                   pl.BlockSpec((1, RET_HEADS, RET_DK, RET_DV), lambda bi, c: (bi, 0, 0, 0))],
        out_shape=[jax.ShapeDtypeStruct((b * s, hv), BF16),
                   jax.ShapeDtypeStruct((b, RET_HEADS, RET_DK, RET_DV), F32)],
        compiler_params=_cparams(("arbitrary", "arbitrary")),
        name="ret_prompt",
    )(z, z, z, z, gn)


RS_SEQ = 4


def _ret_sample_kernel(q_ref, k_ref, v_ref, g_ref, gn_ref, st_in_ref, a_ref, st_out_ref, *, ds):
    rows = RS_SEQ * ds
    shift = ds.bit_length() - 1
    row = lax.broadcasted_iota(jnp.int32, (rows, rows), 0)
    col = lax.broadcasted_iota(jnp.int32, (rows, rows), 1)
    causal = jnp.where(row >= col, (row >> shift) - (col >> shift), -1) == 0
    diff = jnp.where(causal, (row - col).astype(F32), 0.0)
    r1 = lax.broadcasted_iota(jnp.int32, (rows, 1), 0)
    t_idx = (r1 & (ds - 1)).astype(F32)
    seq_of_row = r1 >> shift
    for h in range(RET_HEADS):
        lg = LOG_DECAY[h]
        ks = slice(h * RET_DK, (h + 1) * RET_DK)
        vs = slice(h * RET_DV, (h + 1) * RET_DV)
        q = q_ref[:, ks]
        k = k_ref[:, ks]
        v = v_ref[:, vs]
        decay = jnp.where(causal, jnp.exp(diff * lg), 0.0)
        s = _dot_nt(q, k) * decay
        o = _dot(s.astype(BF16), v)
        qd = jnp.exp((t_idx + 1.0) * lg)
        kd = k.astype(F32) * jnp.exp((ds - 1.0 - t_idx) * lg)
        for sq in range(RS_SEQ):
            st = st_in_ref[sq, h]
            mine = seq_of_row == sq
            o = o + jnp.where(mine, qd * _dot(q, st.astype(BF16)), 0.0)
            kds = jnp.where(mine, kd, 0.0).astype(BF16)
            st_out_ref[sq, h] = math.exp(ds * lg) * st + _dot_tn(kds, v)
        a_ref[:, vs] = _group_norm_gate(o, gn_ref[h:h + 1, :], g_ref[:, vs])


def _ret_sample(cfg, z, gn, state):
    db, ds, npr = cfg["DB"], cfg["DS"], cfg["NP"]
    rows = RS_SEQ * ds
    assert ds & (ds - 1) == 0 and db % RS_SEQ == 0 and npr % rows == 0
    base = npr // rows
    hk = RET_HEADS * RET_DK
    hv = RET_HEADS * RET_DV
    st_spec = pl.BlockSpec((RS_SEQ, RET_HEADS, RET_DK, RET_DV), lambda i: (i, 0, 0, 0))
    return pl.pallas_call(
        functools.partial(_ret_sample_kernel, ds=ds),
        grid=(db // RS_SEQ,),
        in_specs=[pl.BlockSpec((rows, hk), lambda i: (base + i, 0)),
                  pl.BlockSpec((rows, hk), lambda i: (base + i, 1)),
                  pl.BlockSpec((rows, hv), lambda i: (base + i, 1)),
                  pl.BlockSpec((rows, hv), lambda i: (base + i, 2)),
                  pl.BlockSpec((RET_HEADS, RET_DV), lambda i: (0, 0)),
                  st_spec],
        out_specs=[pl.BlockSpec((rows, hv), lambda i: (i, 0)), st_spec],
        out_shape=[jax.ShapeDtypeStruct((db * ds, hv), BF16),
                   jax.ShapeDtypeStruct((db, RET_HEADS, RET_DK, RET_DV), F32)],
        compiler_params=_cparams(("arbitrary",)),
        name="ret_sample",
    )(z, z, z, z, gn, state)


KV_A_PAD = KV_LORA + LANES


def _kv_kernel(x_ref, g_ref, shp, shs, scp, scs, wa_ref, gck_ref, cos_ref, sin_ref, wukv_ref,
               ckvp_ref, ckvs_ref, krp_ref, krs_ref, kcat_ref, v_ref, *, npt):
    i = pl.program_id(0)
    xn = _norm_mod(i, npt, x_ref, g_ref, shp, shs, scp, scs).astype(BF16)
    z = _dot(xn, wa_ref[...])
    ckv = _rms(z[:, :KV_LORA], gck_ref[...])
    kr_lanes = _rope_group(z[:, KV_LORA:], cos_ref[...], sin_ref[...], QK_ROPE)
    kr = kr_lanes[:, :QK_ROPE]

    @pl.when(i < npt)
    def _():
        ckvp_ref[...] = ckv
        krp_ref[0] = kr_lanes.T[:QK_ROPE, :]

    @pl.when(i >= npt)
    def _():
        ckvs_ref[...] = ckv
        krs_ref[...] = kr

    up = _dot(ckv.astype(BF16), wukv_ref[...])
    kr16 = kr.astype(BF16)
    for h in range(MLA_HEADS):
        kcat_ref[h, :, :QK_NOPE] = up[:, h * QK_NOPE:(h + 1) * QK_NOPE].astype(BF16)
        kcat_ref[h, :, QK_NOPE:] = kr16
    v_ref[...] = up[:, MLA_HEADS * QK_NOPE:].astype(BF16)


def _mla_kv(cfg, h, g, mods, wa_pad, g_ckv, cos_tab, sin_tab, w_ukv):
    t, npt, seq_tiles = cfg["T"], cfg["NPT"], cfg["SEQ_TILES"]
    (shp, shs), (scp, scs) = mods
    sh_specs, sh_args = _mod_specs(shp, shs, 0, npt, seq_tiles)
    sc_specs, sc_args = _mod_specs(scp, scs, 1, npt, seq_tiles)
    tab_spec = pl.BlockSpec((TM, LANES), lambda i: (jnp.where(i < npt, i % seq_tiles, seq_tiles), 0))
    n_up = w_ukv.shape[1]
    return pl.pallas_call(
        functools.partial(_kv_kernel, npt=npt),
        grid=(t // TM,),
        in_specs=[pl.BlockSpec((TM, D_MODEL), lambda i: (i, 0)),
                  pl.BlockSpec((1, D_MODEL), lambda i: (0, 0)),
                  *sh_specs, *sc_specs,
                  pl.BlockSpec((D_MODEL, KV_A_PAD), lambda i: (0, 0)),
                  pl.BlockSpec((1, KV_LORA), lambda i: (0, 0)),
                  tab_spec, tab_spec,
                  pl.BlockSpec((KV_LORA, n_up), lambda i: (0, 0))],
        out_specs=[pl.BlockSpec((TM, KV_LORA), lambda i: (jnp.minimum(i, npt - 1), 0)),
                   pl.BlockSpec((TM, KV_LORA), lambda i: (0, 0)),
                   pl.BlockSpec((1, QK_ROPE, TM),
                                lambda i: (jnp.minimum(i, npt - 1) // seq_tiles, 0, jnp.minimum(i, npt - 1) % seq_tiles)),
                   pl.BlockSpec((TM, QK_ROPE), lambda i: (0, 0)),
                   pl.BlockSpec((MLA_HEADS, TM, QK_HEAD), lambda i: (0, i, 0)),
                   pl.BlockSpec((TM, MLA_HEADS * V_HEAD), lambda i: (i, 0))],
        out_shape=[jax.ShapeDtypeStruct((npt * TM, KV_LORA), F32),
                   jax.ShapeDtypeStruct((TM, KV_LORA), F32),
                   jax.ShapeDtypeStruct((npt // seq_tiles, QK_ROPE, seq_tiles * TM), F32),
                   jax.ShapeDtypeStruct((TM, QK_ROPE), F32),
                   jax.ShapeDtypeStruct((MLA_HEADS, t, QK_HEAD), BF16),
                   jax.ShapeDtypeStruct((t, MLA_HEADS * V_HEAD), BF16)],
        compiler_params=_cparams(("arbitrary",)),
        name="mla_kv",
    )(h, g.reshape(1, D_MODEL), *sh_args, *sc_args, wa_pad, g_ckv.reshape(1, KV_LORA),
      cos_tab, sin_tab, w_ukv)


def _dq_kernel(x_ref, g_ref, shp, shs, scp, scs, w_ref, gcq_ref, o_ref, *, npt):
    i = pl.program_id(0)
    xn = _norm_mod(i, npt, x_ref, g_ref, shp, shs, scp, scs).astype(BF16)
    o_ref[...] = _rms(_dot(xn, w_ref[...]), gcq_ref[...]).astype(BF16)


def _mla_dq(cfg, h, g, mods, w_dq, g_cq):
    t, npt, seq_tiles = cfg["T"], cfg["NPT"], cfg["SEQ_TILES"]
    (shp, shs), (scp, scs) = mods
    sh_specs, sh_args = _mod_specs(shp, shs, 0, npt, seq_tiles)
    sc_specs, sc_args = _mod_specs(scp, scs, 1, npt, seq_tiles)
    return pl.pallas_call(
        functools.partial(_dq_kernel, npt=npt),
        grid=(t // TM,),
        in_specs=[pl.BlockSpec((TM, D_MODEL), lambda i: (i, 0)),
                  pl.BlockSpec((1, D_MODEL), lambda i: (0, 0)),
                  *sh_specs, *sc_specs,
                  pl.BlockSpec((D_MODEL, Q_LORA), lambda i: (0, 0)),
                  pl.BlockSpec((1, Q_LORA), lambda i: (0, 0))],
        out_specs=pl.BlockSpec((TM, Q_LORA), lambda i: (i, 0)),
        out_shape=jax.ShapeDtypeStruct((t, Q_LORA), BF16),
        compiler_params=_cparams(("arbitrary",)),
        name="mla_dq",
    )(h, g.reshape(1, D_MODEL), *sh_args, *sc_args, w_dq, g_cq.reshape(1, Q_LORA))


def _uq_kernel(cq_ref, w_ref, cos_ref, sin_ref, o_ref):
    z = _dot(cq_ref[...], w_ref[...])
    n_nope = MLA_HEADS * QK_NOPE
    for h in range(MLA_HEADS):
        o_ref[h, :, :QK_NOPE] = z[:, h * QK_NOPE:(h + 1) * QK_NOPE].astype(BF16)
    cos = cos_ref[...]
    sin = sin_ref[...]
    heads_per_group = LANES // QK_ROPE
    for c in range(MLA_HEADS // heads_per_group):
        r = _rope_group(z[:, n_nope + c * LANES:n_nope + (c + 1) * LANES], cos, sin, QK_ROPE).astype(BF16)
        for u in range(heads_per_group):
            o_ref[c * heads_per_group + u, :, QK_NOPE:] = r[:, u * QK_ROPE:(u + 1) * QK_ROPE]


def _mla_uq(cfg, cq, w_uq_perm, cos_tab, sin_tab):
    t, npt, seq_tiles = cfg["T"], cfg["NPT"], cfg["SEQ_TILES"]
    n = w_uq_perm.shape[1]
    tab_spec = pl.BlockSpec((TM, LANES), lambda i: (jnp.where(i < npt, i % seq_tiles, seq_tiles), 0))
    return pl.pallas_call(
        _uq_kernel,
        grid=(t // TM,),
        in_specs=[pl.BlockSpec((TM, Q_LORA), lambda i: (i, 0)),
                  pl.BlockSpec((Q_LORA, n), lambda i: (0, 0)),
                  tab_spec, tab_spec],
        out_specs=pl.BlockSpec((MLA_HEADS, TM, QK_HEAD), lambda i: (0, i, 0)),
        out_shape=jax.ShapeDtypeStruct((MLA_HEADS, t, QK_HEAD), BF16),
        compiler_params=_cparams(("arbitrary",)),
        name="mla_uq",
    )(cq, w_uq_perm, cos_tab, sin_tab)


def _flash_kernel(q_ref, k_ref, v_ref, o_ref):
    qi = pl.program_id(2)
    q = q_ref[0]
    tq = q.shape[0]

    def step(ki, carry, masked):
        m, l, acc = carry
        start = pl.multiple_of(ki * ATT_T, ATT_T)
        k = k_ref[0, pl.ds(start, ATT_T), :]
        v = v_ref[pl.ds(start, ATT_T), :]
        s = _dot_nt(q, k) * (SM_SCALE * LOG2_E)
        if masked:
            row = lax.broadcasted_iota(jnp.int32, s.shape, 0)
            col = lax.broadcasted_iota(jnp.int32, s.shape, 1)
            s = jnp.where(col <= row, s, NEG_INF)
        m_new = jnp.maximum(m, jnp.max(s, axis=-1, keepdims=True))
        alpha = jnp.exp2(m - m_new)
        p = jnp.exp2(s - m_new)
        l = alpha * l + jnp.sum(p, axis=-1, keepdims=True)
        acc = alpha * acc + _dot(p.astype(BF16), v)
        return m_new, l, acc

    carry = (jnp.full((tq, 1), NEG_INF, F32), jnp.zeros((tq, 1), F32), jnp.zeros((tq, V_HEAD), F32))
    carry = lax.fori_loop(0, qi // 2, lambda j, c: step(2 * j + 1, step(2 * j, c, False), False), carry)
    _, l, acc = lax.cond(qi % 2 == 1,
                         lambda c: step(qi, step(qi - 1, c, False), True),
                         lambda c: step(qi, c, True), carry)
    o_ref[...] = (acc / l).astype(BF16)


def _flash(cfg, q_cat, k_cat, v):
    b, s, npr = cfg["B"], cfg["S"], cfg["NP"]
    nq = s // ATT_T
    return pl.pallas_call(
        _flash_kernel,
        grid=(b, MLA_HEADS, nq),
        in_specs=[pl.BlockSpec((1, ATT_T, QK_HEAD), lambda bi, h, qi: (h, bi * nq + qi, 0)),
                  pl.BlockSpec((1, s, QK_HEAD), lambda bi, h, qi: (h, bi, 0)),
                  pl.BlockSpec((s, V_HEAD), lambda bi, h, qi: (bi, h))],
        out_specs=pl.BlockSpec((ATT_T, V_HEAD), lambda bi, h, qi: (bi * nq + qi, h)),
        out_shape=jax.ShapeDtypeStruct((npr, MLA_HEADS * V_HEAD), BF16),
        compiler_params=_cparams(("arbitrary", "arbitrary", "arbitrary")),
        name="mla_flash",
    )(q_cat, k_cat, v)


def _bmm_kernel(x_ref, w_ref, o_ref):
    o_ref[0] = _dot(x_ref[0], w_ref[0]).astype(o_ref.dtype)


def _head_bmm(x, w):
    hh, m, k = x.shape
    n = w.shape[2]
    return pl.pallas_call(
        _bmm_kernel,
        grid=(hh,),
        in_specs=[pl.BlockSpec((1, m, k), lambda h: (h, 0, 0)),
                  pl.BlockSpec((1, k, n), lambda h: (h, 0, 0))],
        out_specs=pl.BlockSpec((1, m, n), lambda h: (h, 0, 0)),
        out_shape=jax.ShapeDtypeStruct((hh, m, n), BF16),
        compiler_params=_cparams(("arbitrary",)),
        name="head_bmm",
    )(x, w)


NEW_PAD = 16


DEC_PAGES = 16
DEC_SEQS = 2


def _decode_kernel(pt_ref, ql_ref, qr_ref, ckn_ref, krn_ref, ck_hbm, kr_hbm, o_ref,
                   ckbuf, krbuf, sem, *, n_pages, npg, page, ds, n_steps):
    g = pl.program_id(0)
    nch = n_pages // npg
    nk = npg * page

    def start(step, c, slot):
        for u in range(DEC_SEQS):
            for p in range(npg):
                pg = pt_ref[step * DEC_SEQS + u, c * npg + p]
                pltpu.make_async_copy(ck_hbm.at[pg], ckbuf.at[slot, u, pl.ds(p * page, page)],
                                      sem.at[0, slot]).start()
                pltpu.make_async_copy(kr_hbm.at[pg], krbuf.at[slot, u, :, pl.ds(p * page, page)],
                                      sem.at[1, slot]).start()

    def wait(slot):
        pltpu.make_async_copy(ckbuf.at[slot], ckbuf.at[slot], sem.at[0, slot]).wait()
        pltpu.make_async_copy(krbuf.at[slot], krbuf.at[slot], sem.at[1, slot]).wait()

    qls = [ql_ref[u] for u in range(DEC_SEQS)]
    qrs = [qr_ref[u] for u in range(DEC_SEQS)]
    nq = qls[0].shape[0]

    def update(carry, s, vals):
        m, l, acc = carry
        m_new = jnp.maximum(m, jnp.max(s, axis=-1, keepdims=True))
        alpha = jnp.exp(m - m_new)
        p = jnp.exp(s - m_new)
        l = alpha * l + jnp.sum(p, axis=-1, keepdims=True)
        acc = alpha * acc + _dot(p.astype(BF16), vals)
        return m_new, l, acc

    @pl.when(g == 0)
    def _():
        start(0, 0, 0)

    def body(c, carry):
        slot = (g * nch + c) & 1

        @pl.when(c + 1 < nch)
        def _():
            start(g, c + 1, 1 - slot)

        @pl.when((c + 1 == nch) & (g + 1 < n_steps))
        def _():
            start(g + 1, 0, 1 - slot)

        wait(slot)
        out = []
        for u in range(DEC_SEQS):
            ck = ckbuf[slot, u].astype(BF16)
            kr_t = krbuf[slot, u].astype(BF16)
            s = (_dot_nt(qls[u], ck) + _dot(qrs[u], kr_t)) * SM_SCALE
            out.append(update(carry[u], s, ck))
        return tuple(out)

    init = tuple((jnp.full((nq, 1), NEG_INF, F32), jnp.zeros((nq, 1), F32), jnp.zeros((nq, KV_LORA), F32))
                 for _ in range(DEC_SEQS))
    carry = lax.fori_loop(0, nch, body, init)
    for u in range(DEC_SEQS):
        ckn = ckn_ref[u].astype(BF16)
        krn = krn_ref[u].astype(BF16)
        s = (_dot_nt(qls[u], ckn) + _dot_nt(qrs[u], krn)) * SM_SCALE
        row_t = lax.broadcasted_iota(jnp.int32, s.shape, 0) & (ds - 1)
        col = lax.broadcasted_iota(jnp.int32, s.shape, 1)
        s = jnp.where(col <= row_t, s, NEG_INF)
        _, l, acc = update(carry[u], s, ckn)
        o_ref[u] = (acc / l).astype(BF16)


def _decode(cfg, page_table, ql, qr, ckn, krn, cache_ckv, cache_krope_t):
    db, ds = cfg["DB"], cfg["DS"]
    n_pages = page_table.shape[1]
    page = cache_ckv.shape[1]
    npg = math.gcd(n_pages, DEC_PAGES)
    nq = ql.shape[1]
    assert db % DEC_SEQS == 0
    grid_spec = pltpu.PrefetchScalarGridSpec(
        num_scalar_prefetch=1,
        grid=(db // DEC_SEQS,),
        in_specs=[pl.BlockSpec((DEC_SEQS, nq, KV_LORA), lambda b, pt: (b, 0, 0)),
                  pl.BlockSpec((DEC_SEQS, nq, QK_ROPE), lambda b, pt: (b, 0, 0)),
                  pl.BlockSpec((DEC_SEQS, NEW_PAD, KV_LORA), lambda b, pt: (b, 0, 0)),
                  pl.BlockSpec((DEC_SEQS, NEW_PAD, QK_ROPE), lambda b, pt: (b, 0, 0)),
                  pl.BlockSpec(memory_space=pl.ANY),
                  pl.BlockSpec(memory_space=pl.ANY)],
        out_specs=pl.BlockSpec((DEC_SEQS, nq, KV_LORA), lambda b, pt: (b, 0, 0)),
        scratch_shapes=[pltpu.VMEM((2, DEC_SEQS, npg * page, KV_LORA), F32),
                        pltpu.VMEM((2, DEC_SEQS, QK_ROPE, npg * page), F32),
                        pltpu.SemaphoreType.DMA((2, 2))],
    )
    return pl.pallas_call(
        functools.partial(_decode_kernel, n_pages=n_pages, npg=npg, page=page, ds=ds,
                          n_steps=db // DEC_SEQS),
        grid_spec=grid_spec,
        out_shape=jax.ShapeDtypeStruct((db, nq, KV_LORA), BF16),
        compiler_params=_cparams(("arbitrary",)),
        name="mla_decode",
    )(page_table, ql, qr, ckn, krn, cache_ckv, cache_krope_t)


def _router_kernel(ap_ref, as_ref, wo_ref, res_ref, gp, gs, g_ref, shp, shs, scp, scs, wr_ref, br_ref,
                   h_ref, xn_ref, idx_ref, gate_ref, cnt_ref, *, npt):
    i = pl.program_id(0)
    gate = _pick(i < npt, gp, gs)

    def out_proj(a_ref):
        h_ref[...] = res_ref[...] + gate * _dot(a_ref[...], wo_ref[...])

    pl.when(i < npt)(functools.partial(out_proj, ap_ref))
    pl.when(i >= npt)(functools.partial(out_proj, as_ref))
    xn = _norm_mod(i, npt, h_ref, g_ref, shp, shs, scp, scs)
    for c in range(ROW_TILE):
        xn_ref[pl.ds(c, TM, stride=ROW_TILE), :] = xn[:, c * LANES:(c + 1) * LANES]
    w_r = wr_ref[...]
    x_hi = xn.astype(BF16)
    x_lo = (xn - x_hi.astype(F32)).astype(BF16)
    w_hi = w_r.astype(BF16)
    w_lo = (w_r - w_hi.astype(F32)).astype(BF16)
    logits = _dot(x_hi, w_hi) + (_dot(x_lo, w_hi) + _dot(x_hi, w_lo)) + br_ref[...]
    lane_e = lax.broadcasted_iota(jnp.int32, logits.shape, 1).astype(F32)
    lane_o = lax.broadcasted_iota(jnp.int32, (logits.shape[0], LANES), 1)
    idx_out = jnp.zeros((logits.shape[0], LANES), F32)
    val_out = jnp.zeros((logits.shape[0], LANES), F32)
    vals = []
    member = jnp.zeros(logits.shape, F32)
    for kk in range(TOP_K):
        m = jnp.max(logits, axis=-1, keepdims=True)
        sel = jnp.min(jnp.where(logits == m, lane_e, float(N_EXPERTS)), axis=-1, keepdims=True)
        chosen = lane_e == sel
        member = jnp.where(chosen, 1.0, member)
        logits = jnp.where(chosen, -jnp.inf, logits)
        idx_out = jnp.where(lane_o == kk, sel, idx_out)
        vals.append(m)

    @pl.when(i == 0)
    def _():
        cnt_ref[...] = jnp.zeros_like(cnt_ref)

    cnt_ref[...] += jnp.sum(member, axis=0, keepdims=True)
    es = [jnp.exp(v - vals[0]) for v in vals]
    den = es[0] + es[1] + es[2] + es[3]
    for kk in range(TOP_K):
        val_out = jnp.where(lane_o == kk, es[kk] / den, val_out)
    idx_ref[...] = idx_out.astype(jnp.int32)
    gate_ref[...] = val_out


def _router(cfg, a_p, a_s, w_o, res, g, mods, w_r, b_r):
    t, npt, seq_tiles = cfg["T"], cfg["NPT"], cfg["SEQ_TILES"]
    k = w_o.shape[0]
    ga_specs, ga_args = _mod_specs(*mods, 2, npt, seq_tiles)
    sh_specs, sh_args = _mod_specs(*mods, 3, npt, seq_tiles)
    sc_specs, sc_args = _mod_specs(*mods, 4, npt, seq_tiles)
    return pl.pallas_call(
        functools.partial(_router_kernel, npt=npt),
        grid=(t // TM,),
        in_specs=[pl.BlockSpec((TM, k), lambda i: (jnp.minimum(i, npt - 1), 0)),
                  pl.BlockSpec((TM, k), lambda i: (0, 0)),
                  pl.BlockSpec((k, D_MODEL), lambda i: (0, 0)),
                  pl.BlockSpec((TM, D_MODEL), lambda i: (i, 0)),
                  *ga_specs,
                  pl.BlockSpec((1, D_MODEL), lambda i: (0, 0)),
                  *sh_specs, *sc_specs,
                  pl.BlockSpec((D_MODEL, N_EXPERTS), lambda i: (0, 0)),
                  pl.BlockSpec((1, N_EXPERTS), lambda i: (0, 0))],
        out_specs=[pl.BlockSpec((TM, D_MODEL), lambda i: (i, 0)),
                   pl.BlockSpec((TM * ROW_TILE, LANES), lambda i: (i, 0)),
                   pl.BlockSpec((TM, LANES), lambda i: (i, 0)),
                   pl.BlockSpec((TM, LANES), lambda i: (i, 0)),
                   pl.BlockSpec((1, N_EXPERTS), lambda i: (0, 0))],
        out_shape=[jax.ShapeDtypeStruct((t, D_MODEL), F32),
                   jax.ShapeDtypeStruct((t * ROW_TILE, LANES), F32),
                   jax.ShapeDtypeStruct((t, LANES), jnp.int32),
                   jax.ShapeDtypeStruct((t, LANES), F32),
                   jax.ShapeDtypeStruct((1, N_EXPERTS), F32)],
        compiler_params=_cparams(("arbitrary",)),
        name="moe_router",
    )(a_p, a_s, w_o, res, *ga_args, g.reshape(1, D_MODEL), *sh_args, *sc_args, w_r,
      b_r.reshape(1, N_EXPERTS))


IDX_ALIGN = 128
IDX_WIN = MOE_ROWS + IDX_ALIGN
IDX_SLOT = 512
IDX_SLOTS = 4


def _expert_kernel(be_ref, bo_ref, bv_ref, tok_hbm, dst_hbm, x_hbm, wgu_ref, bgu_ref, wdn_ref, bdn_ref,
                   y_hbm, tok_smem, dst_smem, xbuf, ybuf, wgu_bf, wdn_bf, isem, gsem, ssem,
                   *, n_blocks, n_slots):
    i = pl.program_id(0)
    slot = i & 1
    last = n_blocks - 1
    blk_rows = MOE_ROWS * ROW_TILE
    spill0 = n_slots * ROW_TILE

    def idx_copies(blk):
        s4 = blk & (IDX_SLOTS - 1)
        off = bo_ref[jnp.minimum(blk, last)]
        src = pl.ds(pl.multiple_of(off & -IDX_ALIGN, IDX_ALIGN), IDX_WIN)
        dst = pl.ds(pl.multiple_of(s4 * IDX_SLOT, IDX_ALIGN), IDX_WIN)
        return (pltpu.make_async_copy(tok_hbm.at[src], tok_smem.at[dst], isem.at[0, s4]),
                pltpu.make_async_copy(dst_hbm.at[src], dst_smem.at[dst], isem.at[1, s4]))

    def idx_base(blk):
        return (blk & (IDX_SLOTS - 1)) * IDX_SLOT + (bo_ref[jnp.minimum(blk, last)] & (IDX_ALIGN - 1))

    def issue_gather(blk, buf):
        base = idx_base(blk)
        for r in range(MOE_ROWS):
            row = pl.multiple_of(tok_smem[base + r], ROW_TILE)
            pltpu.make_async_copy(x_hbm.at[pl.ds(row, ROW_TILE)], xbuf.at[buf, pl.ds(r * ROW_TILE, ROW_TILE)],
                                  gsem.at[buf]).start(priority=r % 2)

    def wait_gather(buf):
        pltpu.make_async_copy(x_hbm.at[pl.ds(0, blk_rows)], xbuf.at[buf], gsem.at[buf]).wait()

    def wait_scatter(buf):
        pltpu.make_async_copy(ybuf.at[buf], y_hbm.at[pl.ds(0, blk_rows)], ssem.at[buf]).wait()

    @pl.when(i == 0)
    def _():
        for c in idx_copies(0):
            c.start()
        for c in idx_copies(0):
            c.wait()
        issue_gather(0, 0)
        for c in idx_copies(1):
            c.start()
        ybuf[...] = jnp.zeros_like(ybuf)
        for b in range(2):
            pltpu.make_async_copy(ybuf.at[b], y_hbm.at[pl.ds(spill0 + b * blk_rows, blk_rows)],
                                  ssem.at[b]).start()

    @pl.when((i == 0) | (be_ref[i] != be_ref[jnp.maximum(i - 1, 0)]))
    def _():
        wgu_bf[...] = wgu_ref[0].astype(BF16)
        wdn_bf[...] = wdn_ref[0].astype(BF16)

    def run(buf):
        for c in idx_copies(i + 1):
            c.wait()
        wait_gather(buf)
        wait_scatter(buf)

        issue_gather(i + 1, 1 - buf)
        for c in idx_copies(i + 2):
            c.start()

        xb = jnp.concatenate([xbuf[buf, pl.ds(c, MOE_ROWS, stride=ROW_TILE), :] for c in range(ROW_TILE)],
                             axis=1).astype(BF16)
        y = bdn_ref[0]
        cw = D_EXPERT // FFN_CHUNKS
        for j in range(FFN_CHUNKS):
            gs = slice(j * cw, (j + 1) * cw)
            us = slice(D_EXPERT + j * cw, D_EXPERT + (j + 1) * cw)
            gate = jnp.minimum(_dot(xb, wgu_bf[:, gs]) + bgu_ref[0, :, gs], SWIGLU_LIMIT)
            up = jnp.clip(_dot(xb, wgu_bf[:, us]) + bgu_ref[0, :, us], -SWIGLU_LIMIT, SWIGLU_LIMIT)
            act = (up + 1.0) * gate * jax.nn.sigmoid(SWIGLU_ALPHA * gate)
            y = y + _dot(act.astype(BF16), wdn_bf[gs, :])
        for c in range(ROW_TILE):
            ybuf[buf, pl.ds(c, MOE_ROWS, stride=ROW_TILE), :] = y[:, c * LANES:(c + 1) * LANES]

        n_valid = bv_ref[i]
        base = idx_base(i)
        spill = spill0 + buf * blk_rows
        for r in range(MOE_ROWS):
            row = pl.multiple_of(jnp.where(r < n_valid, dst_smem[base + r], spill + r * ROW_TILE), ROW_TILE)
            pltpu.make_async_copy(ybuf.at[buf, pl.ds(r * ROW_TILE, ROW_TILE)], y_hbm.at[pl.ds(row, ROW_TILE)],
                                  ssem.at[buf]).start(priority=r % 2)

    for b in range(2):
        pl.when(slot == b)(functools.partial(run, b))

    @pl.when(i == last)
    def _():
        wait_gather(1 - slot)
        for c in idx_copies(i + 2):
            c.wait()
        wait_scatter(1 - slot)
        wait_scatter(slot)


def _experts(cfg, xn_rows, tables, layer, w_gu, b_gu, w_dn, b_dn):
    t = cfg["T"]
    blk_exp, blk_off, blk_valid, tok_rows, dst_rows = tables
    n_blocks = blk_exp.shape[0]
    n_slots = t * TOP_K
    any_spec = pl.BlockSpec(memory_space=pl.ANY)
    e0 = layer * N_EXPERTS
    w_gu = w_gu.reshape(-1, D_MODEL, 2 * D_EXPERT)
    w_dn = w_dn.reshape(-1, D_EXPERT, D_MODEL)
    b_gu = b_gu.reshape(-1, 1, 2 * D_EXPERT)
    b_dn = b_dn.reshape(-1, 1, D_MODEL)
    grid_spec = pltpu.PrefetchScalarGridSpec(
        num_scalar_prefetch=3,
        grid=(n_blocks,),
        in_specs=[any_spec, any_spec, any_spec,
                  pl.BlockSpec((1, D_MODEL, 2 * D_EXPERT), lambda i, be, bo, bv: (e0 + be[i], 0, 0)),
                  pl.BlockSpec((1, 1, 2 * D_EXPERT), lambda i, be, bo, bv: (e0 + be[i], 0, 0)),
                  pl.BlockSpec((1, D_EXPERT, D_MODEL), lambda i, be, bo, bv: (e0 + be[i], 0, 0)),
                  pl.BlockSpec((1, 1, D_MODEL), lambda i, be, bo, bv: (e0 + be[i], 0, 0))],
        out_specs=any_spec,
        scratch_shapes=[pltpu.SMEM((IDX_SLOTS * IDX_SLOT,), jnp.int32),
                        pltpu.SMEM((IDX_SLOTS * IDX_SLOT,), jnp.int32),
                        pltpu.VMEM((2, MOE_ROWS * ROW_TILE, LANES), F32),
                        pltpu.VMEM((2, MOE_ROWS * ROW_TILE, LANES), F32),
                        pltpu.VMEM((D_MODEL, 2 * D_EXPERT), BF16),
                        pltpu.VMEM((D_EXPERT, D_MODEL), BF16),
                        pltpu.SemaphoreType.DMA((2, IDX_SLOTS)),
                        pltpu.SemaphoreType.DMA((2,)),
                        pltpu.SemaphoreType.DMA((2,))],
    )
    return pl.pallas_call(
        functools.partial(_expert_kernel, n_blocks=n_blocks, n_slots=n_slots),
        grid_spec=grid_spec,
        out_shape=jax.ShapeDtypeStruct(((n_slots + 2 * MOE_ROWS) * ROW_TILE, LANES), F32),
        compiler_params=_cparams(("arbitrary",)),
        name="moe_experts",
    )(blk_exp, blk_off, blk_valid, tok_rows, dst_rows, xn_rows, w_gu, b_gu, w_dn, b_dn)


def _combine_kernel(h_ref, y0, y1, y2, y3, gates_ref, gp, gs, o_ref, *, npt):
    i = pl.program_id(0)
    gates = gates_ref[...]
    gmod = _pick(i < npt, gp, gs)
    for c in range(ROW_TILE):
        sl = slice(c * LANES, (c + 1) * LANES)
        rows = pl.ds(c, TM, stride=ROW_TILE)
        y = gates[:, 0:1] * y0[rows, :]
        for kk, y_ref in ((1, y1), (2, y2), (3, y3)):
            y = y + gates[:, kk:kk + 1] * y_ref[rows, :]
        o_ref[:, sl] = h_ref[:, sl] + gmod[:, sl] * y


def _combine(cfg, h, y_rows, gates, gate_mods):
    t, npt, seq_tiles = cfg["T"], cfg["NPT"], cfg["SEQ_TILES"]
    g_specs, g_args = _mod_specs(*gate_mods, 5, npt, seq_tiles)
    nt = t // TM
    y_specs = [pl.BlockSpec((TM * ROW_TILE, LANES), functools.partial(lambda i, kk: (kk * nt + i, 0), kk=kk))
               for kk in range(TOP_K)]
    return pl.pallas_call(
        functools.partial(_combine_kernel, npt=npt),
        grid=(nt,),
        in_specs=[pl.BlockSpec((TM, D_MODEL), lambda i: (i, 0)),
                  *y_specs,
                  pl.BlockSpec((TM, LANES), lambda i: (i, 0)),
                  *g_specs],
        out_specs=pl.BlockSpec((TM, D_MODEL), lambda i: (i, 0)),
        out_shape=jax.ShapeDtypeStruct((t, D_MODEL), F32),
        compiler_params=_cparams(("arbitrary",)),
        name="moe_combine",
    )(h, y_rows, y_rows, y_rows, y_rows, gates, *g_args)


def _dispatch_tables(cfg, top_i, counts):
    t = cfg["T"]
    n_asg = t * TOP_K
    bits = (n_asg - 1).bit_length()
    flat = jnp.arange(n_asg, dtype=jnp.int32).reshape(t, TOP_K)
    keys = jnp.sort(((top_i << bits) | flat).reshape(-1))
    asg = keys & ((1 << bits) - 1)
    tok = asg // TOP_K
    pad = jnp.zeros((2 * MOE_ROWS,), jnp.int32)
    tok_rows = jnp.concatenate([tok * ROW_TILE, pad])
    dst_rows = jnp.concatenate([((asg % TOP_K) * t + tok) * ROW_TILE, pad])
    n_blocks = -(-n_asg // MOE_ROWS) + N_EXPERTS
    nblk = (counts + MOE_ROWS - 1) // MOE_ROWS
    blk_end = jnp.cumsum(nblk)
    grp_start = jnp.cumsum(counts) - counts
    b = jnp.arange(n_blocks, dtype=jnp.int32)
    blk_exp = jnp.minimum(jnp.sum((blk_end[None, :] <= b[:, None]).astype(jnp.int32), axis=1), N_EXPERTS - 1)
    within = (b - (blk_end - nblk)[blk_exp]) * MOE_ROWS
    blk_off = jnp.minimum(grp_start[blk_exp] + within, n_asg).astype(jnp.int32)
    blk_valid = jnp.clip(counts[blk_exp] - within, 0, MOE_ROWS).astype(jnp.int32)
    return blk_exp, blk_off, blk_valid, tok_rows, dst_rows


def _combine_final_kernel(h_ref, y0, y1, y2, y3, gates_ref, gp, gs, g_ref, shp, shs, scp, scs,
                          op_ref, os_ref, h_scr, *, npt):
    i = pl.program_id(0)
    _combine_kernel(h_ref, y0, y1, y2, y3, gates_ref, gp, gs, h_scr, npt=npt)
    y = _norm_mod(i, npt, h_scr, g_ref, shp, shs, scp, scs)

    @pl.when(i < npt)
    def _():
        op_ref[...] = y

    @pl.when(i >= npt)
    def _():
        os_ref[...] = y


def _combine_final(cfg, h, y_rows, gates, gate_mods, g, mods):
    t, npt, seq_tiles = cfg["T"], cfg["NPT"], cfg["SEQ_TILES"]
    g_specs, g_args = _mod_specs(*gate_mods, 5, npt, seq_tiles)
    sh_specs, sh_args = _mod_specs(*mods, 0, npt, seq_tiles)
    sc_specs, sc_args = _mod_specs(*mods, 1, npt, seq_tiles)
    nt = t // TM
    y_specs = [pl.BlockSpec((TM * ROW_TILE, LANES), functools.partial(lambda i, kk: (kk * nt + i, 0), kk=kk))
               for kk in range(TOP_K)]
    return pl.pallas_call(
        functools.partial(_combine_final_kernel, npt=npt),
        grid=(nt,),
        in_specs=[pl.BlockSpec((TM, D_MODEL), lambda i: (i, 0)),
                  *y_specs,
                  pl.BlockSpec((TM, LANES), lambda i: (i, 0)),
                  *g_specs,
                  pl.BlockSpec((1, D_MODEL), lambda i: (0, 0)),
                  *sh_specs, *sc_specs],
        out_specs=[pl.BlockSpec((TM, D_MODEL), lambda i: (jnp.minimum(i, npt - 1), 0)),
                   pl.BlockSpec((TM, D_MODEL), lambda i: (0, 0))],
        out_shape=[jax.ShapeDtypeStruct((npt * TM, D_MODEL), F32),
                   jax.ShapeDtypeStruct((TM, D_MODEL), F32)],
        scratch_shapes=[pltpu.VMEM((TM, D_MODEL), F32)],
        compiler_params=_cparams(("arbitrary",)),
        name="moe_combine_final",
    )(h, y_rows, y_rows, y_rows, y_rows, gates, *g_args, g.reshape(1, D_MODEL), *sh_args, *sc_args)


def _moe(cfg, a_p, a_s, w_o, res, g, mods, w_r, b_r, layer, w_gu, b_gu, w_dn, b_dn, final=None):
    h, xn, idx, gates, counts = _router(cfg, a_p, a_s, w_o, res, g, mods, w_r, b_r)
    tables = _dispatch_tables(cfg, idx[:, :TOP_K], counts[0].astype(jnp.int32))
    yk = _experts(cfg, xn, tables, layer, w_gu, b_gu, w_dn, b_dn)
    if final is None:
        return _combine(cfg, h, yk, gates, mods)
    return _combine_final(cfg, h, yk, gates, mods, *final)


def _rope_tables(pos, group):
    half = group // 2
    inv = ROPE_BASE ** (-jnp.arange(half, dtype=F32) / half)
    ang = pos.astype(F32)[:, None] * inv[None, :]
    cos = jnp.cos(ang)
    sin = jnp.sin(ang)
    reps = LANES // group
    cos_t = jnp.tile(jnp.concatenate([cos, cos], axis=-1), (1, reps))
    sin_t = jnp.tile(jnp.concatenate([-sin, sin], axis=-1), (1, reps))
    return cos_t, sin_t


def _split_mods(cfg, ada):
    b, ds = cfg["B"], cfg["DS"]
    return ada[:b].reshape(b, 1, ada.shape[1]), jnp.repeat(ada[b:], ds, axis=0)


def kernel(x_prompt, x_sample, c_prompt, c_sample, state_ret, cache_ckv, cache_krope, page_table,
           w_ada, b_ada, g_mix, g_ffn, ret_w_in, ret_gn, ret_w_o,
           w_ada_kv, b_ada_kv, g_kv_in, mla_w_kv_a, g_ckv, mla_w_uk, mla_w_uv,
           mla_w_dq, g_cq, mla_w_uq, mla_w_o,
           w_router, b_router, w_gu, b_gu, w_dn, b_dn,
           w_ada_f, b_ada_f, g_final):
    b, s, d = x_prompt.shape
    db, ds, _ = x_sample.shape
    assert d == D_MODEL and db * ds == TM and s % TM == 0 and s % ATT_T == 0
    npr = b * s
    cfg = dict(B=b, S=s, DB=db, DS=ds, NP=npr, T=npr + TM, NPT=npr // TM, SEQ_TILES=s // TM)
    past_len = page_table.shape[1] * cache_ckv.shape[1]

    h = jnp.concatenate([x_prompt.reshape(npr, d), x_sample.reshape(TM, d)], axis=0)
    c_all = jnp.concatenate([c_prompt, c_sample], axis=0)

    pos = jnp.concatenate([jnp.arange(s, dtype=jnp.int32),
                           jnp.tile(past_len + jnp.arange(ds, dtype=jnp.int32), db)])
    cos_r, sin_r = _rope_tables(pos, RET_DK)
    k_scale = RET_DK ** -0.5
    ret_cos = jnp.stack([cos_r, cos_r * k_scale])
    ret_sin = jnp.stack([sin_r, sin_r * k_scale])
    mla_cos, mla_sin = _rope_tables(pos, QK_ROPE)

    mods_kv = _split_mods(cfg, _adaln(c_all, w_ada_kv, b_ada_kv))
    mods_f = _split_mods(cfg, _adaln(c_all, w_ada_f, b_ada_f))

    n_a = state_ret.shape[0]
    depth = w_ada.shape[0]
    ret_p, ret_s = [], []
    outs_kv = None
    for l in range(depth):
        mods = _split_mods(cfg, _adaln(c_all, w_ada, b_ada, layer=l))
        mm = (mods, mods)
        if l < n_a:
            z = _ret_inproj(cfg, h, g_mix[l], mm, ret_w_in[l].astype(BF16), ret_cos, ret_sin)
            a_p, st_p = _ret_prompt(cfg, z, ret_gn[l])
            a_s, st_s = _ret_sample(cfg, z, ret_gn[l], state_ret[l])
            ret_p.append(st_p)
            ret_s.append(st_s)
            w_o = ret_w_o[l].astype(BF16)
        else:
            bl = l - n_a
            if bl == 0:
                wa_pad = jnp.pad(mla_w_kv_a, ((0, 0), (0, KV_A_PAD - mla_w_kv_a.shape[1]))).astype(BF16)
                w_ukv = jnp.concatenate([mla_w_uk.reshape(KV_LORA, -1), mla_w_uv.reshape(KV_LORA, -1)],
                                        axis=1).astype(BF16)
                ckv_p, ckv_s, kr_p_t, kr_s, k_cat, v_all = _mla_kv(
                    cfg, h, g_kv_in, (mods_kv, mods_kv), wa_pad, g_ckv, mla_cos, mla_sin, w_ukv)
                outs_kv = (ckv_p, ckv_s, kr_p_t, kr_s)
                ckn = jnp.pad(ckv_s.reshape(db, ds, KV_LORA), ((0, 0), (0, NEW_PAD - ds), (0, 0)))
                krn = jnp.pad(kr_s.reshape(db, ds, QK_ROPE), ((0, 0), (0, NEW_PAD - ds), (0, 0)))
                w_uk_t = jnp.transpose(mla_w_uk, (1, 2, 0)).astype(BF16)
                w_uv_h = jnp.transpose(mla_w_uv, (1, 0, 2)).astype(BF16)
            cq = _mla_dq(cfg, h, g_mix[l], mm, mla_w_dq[bl].astype(BF16), g_cq[bl])
            w_uq = mla_w_uq[bl].reshape(Q_LORA, MLA_HEADS, QK_HEAD)
            w_uq_perm = jnp.concatenate([w_uq[:, :, :QK_NOPE].reshape(Q_LORA, -1),
                                         w_uq[:, :, QK_NOPE:].reshape(Q_LORA, -1)], axis=1).astype(BF16)
            q_cat = _mla_uq(cfg, cq, w_uq_perm, mla_cos, mla_sin)
            o_p = _flash(cfg, q_cat, k_cat, v_all)
            q_s = q_cat[:, npr:, :]
            ql = _head_bmm(q_s[:, :, :QK_NOPE], w_uk_t)
            to_seq = lambda x: x.reshape(MLA_HEADS, db, ds, x.shape[-1]).transpose(1, 0, 2, 3).reshape(
                db, MLA_HEADS * ds, x.shape[-1])
            ctx = _decode(cfg, page_table, to_seq(ql), to_seq(q_s[:, :, QK_NOPE:]), ckn, krn,
                          cache_ckv, jnp.swapaxes(cache_krope, 1, 2))
            ctx_h = ctx.reshape(db, MLA_HEADS, ds, KV_LORA).transpose(1, 0, 2, 3).reshape(MLA_HEADS, TM, KV_LORA)
            a_s = _head_bmm(ctx_h, w_uv_h).transpose(1, 0, 2).reshape(TM, MLA_HEADS * V_HEAD)
            a_p = o_p
            w_o = mla_w_o[bl].astype(BF16)
        final = (g_final, mods_f) if l == depth - 1 else None
        h = _moe(cfg, a_p, a_s, w_o, h, g_ffn[l], mods, w_router[l], b_router[l],
                 l, w_gu, b_gu, w_dn, b_dn, final=final)

    y_p, y_s = h
    ckv_p, ckv_s, kr_p_t, kr_s = outs_kv
    return (y_p.reshape(b, s, d), y_s.reshape(db, ds, d),
            jnp.stack(ret_p, axis=0), jnp.stack(ret_s, axis=0),
            ckv_p.reshape(b, s, KV_LORA), jnp.swapaxes(kr_p_t, 1, 2),
            ckv_s.reshape(db, ds, KV_LORA), kr_s.reshape(db, ds, QK_ROPE))
```
